```python
import jax
import jax.numpy as jnp
from jax import lax
import numpy as np

D_MODEL = 1024
BATCH = 2
SEQ = 16384
DEPTH = 2

GRID_W = 64
CTX_LEN = 256
N_SUB = 3
D_FF = 2816
NA_HEADS = 8
NA_HEAD_DIM = 64
NA_KH = 8
NA_KW = 16
ML_HEADS = 4
ML_HEAD_DIM = 128
ML_CHUNK = 128
ROPE_THETA = 10000.0
SG_CHUNK = 128
SG_WIDTH = 2048
SG_GROUPS = 8
NA_WIDTH = NA_HEADS * NA_HEAD_DIM
ML_WIDTH = ML_HEADS * ML_HEAD_DIM
MIX_WIDTH = NA_WIDTH + ML_WIDTH
N_GATES = 4 * ML_HEADS
P_EVEN = 3 * NA_WIDTH + 4 * ML_WIDTH + N_GATES
SPLITS = (NA_WIDTH, 2 * NA_WIDTH, 3 * NA_WIDTH, 3 * NA_WIDTH + ML_WIDTH, 3 * NA_WIDTH + 2 * ML_WIDTH,
          3 * NA_WIDTH + 3 * ML_WIDTH, 3 * NA_WIDTH + 4 * ML_WIDTH)
N_EVEN = (DEPTH + 1) // 2
N_ODD = DEPTH // 2
EPS = 1e-6

kernel_name = 'hybrid_natten_mlstm_sgmlp_dit_block'


def rms_norm(x, g):
    xf = x.astype(jnp.float32)
    y = xf * lax.rsqrt(jnp.mean(jnp.square(xf), axis=-1, keepdims=True) + EPS)
    return (y * g.astype(jnp.float32)).astype(x.dtype)


def layer_norm(x, g, b):
    xf = x.astype(jnp.float32)
    mu = jnp.mean(xf, axis=-1, keepdims=True)
    var = jnp.mean(jnp.square(xf - mu), axis=-1, keepdims=True)
    y = (xf - mu) * lax.rsqrt(var + EPS)
    return (y * g.astype(jnp.float32) + b.astype(jnp.float32)).astype(x.dtype)


def modulate(x, g, shift, scale):
    return rms_norm(x, g) * (1.0 + scale) + shift


def swiglu(h, w_in, w_out):
    a, b = jnp.split(h @ w_in, 2, axis=-1)
    return (jax.nn.silu(a) * b) @ w_out


def macaron_half_ffn(x, g, shift, scale, gate, w_in, w_out):
    return x + 0.5 * gate * swiglu(modulate(x, g, shift, scale), w_in, w_out)


def mod_terms(m, j):
    return m[:, 3 * j], m[:, 3 * j + 1], m[:, 3 * j + 2]


def axial_rope(x):
    n, dh = x.shape[1], x.shape[-1]
    n_pairs = dh // 4
    pos = jnp.arange(n)
    row = (pos // GRID_W).astype(jnp.float32)
    col = (pos % GRID_W).astype(jnp.float32)
    inv_freq = ROPE_THETA ** (-jnp.arange(n_pairs, dtype=jnp.float32) / n_pairs)
    ang = jnp.concatenate([row[:, None] * inv_freq, col[:, None] * inv_freq], axis=-1)
    cos = jnp.cos(ang)[None, :, None, :]
    sin = jnp.sin(ang)[None, :, None, :]
    xp = x.astype(jnp.float32).reshape(*x.shape[:-1], dh // 2, 2)
    x1, x2 = xp[..., 0], xp[..., 1]
    out = jnp.stack([x1 * cos - x2 * sin, x1 * sin + x2 * cos], axis=-1)
    return out.reshape(x.shape).astype(x.dtype)


def dense_attention(q, k, v):
    s = jnp.einsum('bqhd,bkhd->bhqk', q, k) * (q.shape[-1] ** -0.5)
    p = jax.nn.softmax(s.astype(jnp.float32), axis=-1).astype(v.dtype)
    return jnp.einsum('bhqk,bkhd->bqhd', p, v)


def neighbourhood_attention(q, k, v, kx, vx, rpb):
    B, N, H, Dh = q.shape
    rows = N // GRID_W
    kh = min(NA_KH, rows)
    kw = NA_KW
    scale = Dh ** -0.5
    qg = q.reshape(B, rows, GRID_W, H, Dh)
    kg = k.reshape(B, rows, GRID_W, H, Dh)
    vg = v.reshape(B, rows, GRID_W, H, Dh)
    cols = jnp.arange(GRID_W)
    col_idx = jnp.clip(cols - kw // 2, 0, GRID_W - kw)[:, None] + jnp.arange(kw)[None, :]
    col_rel = col_idx - cols[:, None] + (NA_KW - 1)

    def row_block(i):
        r0 = jnp.clip(i - kh // 2, 0, rows - kh)
        q_i = lax.dynamic_index_in_dim(qg, i, axis=1, keepdims=False)
        k_win = lax.dynamic_slice_in_dim(kg, r0, kh, axis=1)[:, :, col_idx]
        v_win = lax.dynamic_slice_in_dim(vg, r0, kh, axis=1)[:, :, col_idx]
        row_rel = r0 + jnp.arange(kh) - i + (NA_KH - 1)
        bias = rpb[:, row_rel[None, :, None], col_rel[:, None, :]]
        s_loc = jnp.einsum('bqhd,baqkhd->bhqak', q_i, k_win) * scale + bias
        s_ctx = jnp.einsum('bqhd,bchd->bhqc', q_i, kx) * scale
        s = jnp.concatenate([s_loc.reshape(B, H, GRID_W, kh * kw), s_ctx], axis=-1).astype(jnp.float32)
        p = jax.nn.softmax(s, axis=-1).astype(v.dtype)
        p_loc = p[..., :kh * kw].reshape(B, H, GRID_W, kh, kw)
        p_ctx = p[..., kh * kw:]
        return (jnp.einsum('bhqak,baqkhd->bqhd', p_loc, v_win)
                + jnp.einsum('bhqc,bchd->bqhd', p_ctx, vx))

    out = lax.map(row_block, jnp.arange(rows))
    return jnp.moveaxis(out, 0, 1).reshape(B, N, H, Dh)


def mlstm_scan(q, k, v, log_i, log_f, state, with_output):
    B, H, N, Dk = q.shape
    nc = N // ML_CHUNK

    def chunks(a):
        return jnp.moveaxis(a.reshape(B, H, nc, ML_CHUNK, *a.shape[3:]), 2, 0)

    tri = jnp.tril(jnp.ones((ML_CHUNK, ML_CHUNK), dtype=bool))

    def step(carry, inp):
        C, n, m = carry
        qc, kc, vc, ic, fc = inp
        b = jnp.cumsum(fc, axis=-1)
        b_end = b[..., -1]
        w_end = b_end[..., None] - b + ic
        m_new = jnp.maximum(b_end + m, jnp.max(w_end, axis=-1))
        a_prev = jnp.exp(b_end + m - m_new)
        a_tok = jnp.exp(w_end - m_new[..., None])
        C_new = a_prev[..., None, None] * C + jnp.einsum('bhs,bhsv,bhsk->bhvk', a_tok, vc, kc)
        n_new = a_prev[..., None] * n + jnp.einsum('bhs,bhsk->bhk', a_tok, kc)
        if not with_output:
            return (C_new, n_new, m_new), None
        log_w = jnp.where(tri, b[..., :, None] - b[..., None, :] + ic[..., None, :], -jnp.inf)
        log_inter = b + m[..., None]
        m_t = jnp.maximum(log_inter, jnp.max(log_w, axis=-1))
        w_intra = jnp.exp(log_w - m_t[..., None])
        w_inter = jnp.exp(log_inter - m_t)
        s = jnp.einsum('bhtk,bhsk->bhts', qc, kc) * w_intra
        num = (w_inter[..., None] * jnp.einsum('bhvk,bhtk->bhtv', C, qc)
               + jnp.einsum('bhts,bhsv->bhtv', s, vc))
        den = w_inter * jnp.einsum('bhk,bhtk->bht', n, qc) + jnp.sum(s, axis=-1)
        h = num / jnp.maximum(jnp.abs(den), jnp.exp(-m_t))[..., None]
        return (C_new, n_new, m_new), h

    state, hs = lax.scan(step, state, (chunks(q), chunks(k), chunks(v), chunks(log_i), chunks(log_f)))
    if not with_output:
        return state, None
    return state, jnp.moveaxis(hs, 0, 2).reshape(B, H, N, Dk)


def _flip(a, rev):
    return jnp.flip(a, axis=2) if rev else a


def mlstm_bidirectional(q, k, v, log_i, log_f, qx, kx, vx, log_ix, log_fx, ctx_out):
    B, _, H, Dk = q.shape
    bh = lambda a: jnp.moveaxis(a.astype(jnp.float32), 1, 2)
    lat = [bh(a) for a in (q, k, v)]
    cx = [bh(a) for a in (qx, kx, vx)]
    h_lat, h_ctx = [], []
    for d in range(2):
        rev = d == 1
        init = (jnp.zeros((B, H, Dk, Dk), jnp.float32), jnp.zeros((B, H, Dk), jnp.float32),
                jnp.zeros((B, H), jnp.float32))
        st, hc = mlstm_scan(*[_flip(a, rev) for a in cx], _flip(bh(log_ix[..., d]), rev),
                            _flip(bh(log_fx[..., d]), rev), init, ctx_out)
        _, hl = mlstm_scan(*[_flip(a, rev) for a in lat], _flip(bh(log_i[..., d]), rev),
                           _flip(bh(log_f[..., d]), rev), st, True)
        h_lat.append(_flip(hl, rev))
        if ctx_out:
            h_ctx.append(_flip(hc, rev))
    y = jnp.moveaxis(h_lat[0] + h_lat[1], 2, 1)
    yx = jnp.moveaxis(h_ctx[0] + h_ctx[1], 2, 1) if ctx_out else None
    return y, yx


def mlstm_readout(h, o, head_g):
    B, L, H, Dk = h.shape
    y = rms_norm(h, head_g) * jax.nn.sigmoid(o.astype(jnp.float32)).reshape(B, L, H, Dk)
    return y.reshape(B, L, H * Dk).astype(o.dtype)


def even_mixer(h, hx, w_in, rpb, gate_b, head_g, w_out, ctx_out):
    def project(t):
        p = t @ w_in
        Bt, L, _ = p.shape
        qa, ka, va, qb, kb, vb, ob, g = jnp.split(p, SPLITS, axis=-1)
        na = [a.reshape(Bt, L, NA_HEADS, NA_HEAD_DIM) for a in (qa, ka, va)]
        ml = [a.reshape(Bt, L, ML_HEADS, ML_HEAD_DIM) for a in (qb, kb, vb)]
        g = g.reshape(Bt, L, ML_HEADS, 2, 2).astype(jnp.float32) + gate_b
        return na, ml, ob, g[..., 0], jax.nn.log_sigmoid(g[..., 1])

    (qa, ka, va), (qb, kb, vb), ob, li, lf = project(h)
    (qax, kax, vax), (qbx, kbx, vbx), obx, lix, lfx = project(hx)
    B, N, _ = h.shape
    y_a = neighbourhood_attention(qa, ka, va, kax, vax, rpb).reshape(B, N, NA_WIDTH)
    k_scale = ML_HEAD_DIM ** -0.5
    h_b, h_bx = mlstm_bidirectional(axial_rope(qb), axial_rope(kb) * k_scale, vb, li, lf,
                                    qbx, kbx * k_scale, vbx, lix, lfx, ctx_out)
    y = jnp.concatenate([y_a, mlstm_readout(h_b, ob, head_g)], axis=-1) @ w_out
    if not ctx_out:
        return y, None
    Bx, Lc, _ = hx.shape
    y_ax = dense_attention(qax, kax, vax).reshape(Bx, Lc, NA_WIDTH)
    yx = jnp.concatenate([y_ax, mlstm_readout(h_bx, obx, head_g)], axis=-1) @ w_out
    return y, yx


def spatial_gating(h, w_in, ln_g, ln_b, w_s, b_s, w_out):
    B, L, _ = h.shape
    u, v = jnp.split(jax.nn.gelu(h @ w_in), 2, axis=-1)
    v = layer_norm(v, ln_g, ln_b)
    nc = L // SG_CHUNK
    vg = v.reshape(B, nc, SG_CHUNK, SG_GROUPS, SG_WIDTH // SG_GROUPS)
    mixed = jnp.einsum('gts,bcsgd->bctgd', w_s, vg) + b_s.T[None, None, :, :, None]
    return (u * mixed.reshape(B, L, SG_WIDTH)) @ w_out


def setup_inputs(seed: int = 0) -> dict:
    key = jax.random.key(seed)
    ks = jax.random.split(key, 24)
    nrm = lambda k, shape, std: jax.random.normal(k, shape, jnp.float32) * std
    D = D_MODEL
    forget_base = jnp.linspace(3.0, 6.0, ML_HEADS, dtype=jnp.float32)[:, None]
    gate_b = jnp.stack([nrm(ks[11], (N_EVEN, ML_HEADS, 2), 0.1),
                        forget_base + nrm(ks[12], (N_EVEN, ML_HEADS, 2), 0.1)], axis=-1)
    return {
        'x': nrm(ks[0], (BATCH, SEQ, D), 1.0),
        'c': nrm(ks[1], (BATCH, D), 1.0),
        'ctx': nrm(ks[2], (BATCH, CTX_LEN, D), 1.0),
        'c_ctx': nrm(ks[3], (D,), 1.0),
        'w_mod': nrm(ks[4], (DEPTH, D, 3 * N_SUB * D), 0.5 * D ** -0.5),
        'b_mod': nrm(ks[5], (DEPTH, 3 * N_SUB * D), 0.02),
        'norm_g': 1.0 + nrm(ks[6], (DEPTH, N_SUB, D), 0.05),
        'ffn_w_in': nrm(ks[7], (DEPTH, 2, D, 2 * D_FF), D ** -0.5),
        'ffn_w_out': nrm(ks[8], (DEPTH, 2, D_FF, D), D_FF ** -0.5),
        'mix_w_in': nrm(ks[9], (N_EVEN, D, P_EVEN), D ** -0.5),
        'na_rpb': nrm(ks[10], (N_EVEN, NA_HEADS, 2 * NA_KH - 1, 2 * NA_KW - 1), 0.1),
        'ml_gate_b': gate_b,
        'ml_head_g': 1.0 + nrm(ks[13], (N_EVEN, ML_HEADS, ML_HEAD_DIM), 0.05),
        'mix_w_out': nrm(ks[14], (N_EVEN, MIX_WIDTH, D), MIX_WIDTH ** -0.5),
        'sg_w_in': nrm(ks[15], (N_ODD, D, 2 * SG_WIDTH), D ** -0.5),
        'sg_ln_g': 1.0 + nrm(ks[16], (N_ODD, SG_WIDTH), 0.05),
        'sg_ln_b': nrm(ks[17], (N_ODD, SG_WIDTH), 0.02),
        'sg_w_s': nrm(ks[18], (N_ODD, SG_GROUPS, SG_CHUNK, SG_CHUNK), 0.5 * SG_CHUNK ** -0.5),
        'sg_b_s': 1.0 + nrm(ks[19], (N_ODD, SG_GROUPS, SG_CHUNK), 0.1),
        'sg_w_out': nrm(ks[20], (N_ODD, SG_WIDTH, D), SG_WIDTH ** -0.5),
        'final_g': 1.0 + nrm(ks[21], (D,), 0.05),
    }


def reference(x, c, ctx, c_ctx, w_mod, b_mod, norm_g, ffn_w_in, ffn_w_out, mix_w_in, na_rpb, ml_gate_b,
              ml_head_g, mix_w_out, sg_w_in, sg_ln_g, sg_ln_b, sg_w_s, sg_b_s, sg_w_out, final_g):
    B, N, D = x.shape
    last_ctx_layer = ((DEPTH - 1) // 2) * 2
    silu_c = jax.nn.silu(c)
    silu_cx = jax.nn.silu(c_ctx)[None]
    xc = ctx
    for l in range(DEPTH):
        ctx_in = l <= last_ctx_layer
        ctx_out = l < last_ctx_layer
        mod = (silu_c @ w_mod[l] + b_mod[l]).reshape(B, 3 * N_SUB, 1, D)
        x = macaron_half_ffn(x, norm_g[l, 0], *mod_terms(mod, 0), ffn_w_in[l, 0], ffn_w_out[l, 0])
        sh, sc, gt = mod_terms(mod, 1)
        h = modulate(x, norm_g[l, 1], sh, sc)
        hx = None
        if ctx_in:
            modx = (silu_cx @ w_mod[l] + b_mod[l]).reshape(1, 3 * N_SUB, 1, D)
            xc = macaron_half_ffn(xc, norm_g[l, 0], *mod_terms(modx, 0), ffn_w_in[l, 0], ffn_w_out[l, 0])
            shx, scx, gtx = mod_terms(modx, 1)
            hx = modulate(xc, norm_g[l, 1], shx, scx)
        if l % 2 == 0:
            e = l // 2
            y, yx = even_mixer(h, hx, mix_w_in[e], na_rpb[e], ml_gate_b[e], ml_head_g[e], mix_w_out[e], ctx_out)
        else:
            o = l // 2
            y = spatial_gating(h, sg_w_in[o], sg_ln_g[o], sg_ln_b[o], sg_w_s[o], sg_b_s[o], sg_w_out[o])
            yx = (spatial_gating(hx, sg_w_in[o], sg_ln_g[o], sg_ln_b[o], sg_w_s[o], sg_b_s[o], sg_w_out[o])
                  if ctx_out else None)
        x = x + gt * y
        x = macaron_half_ffn(x, norm_g[l, 2], *mod_terms(mod, 2), ffn_w_in[l, 1], ffn_w_out[l, 1])
        if ctx_out:
            xc = xc + gtx * yx
            xc = macaron_half_ffn(xc, norm_g[l, 2], *mod_terms(modx, 2), ffn_w_in[l, 1], ffn_w_out[l, 1])
    return rms_norm(x, final_g)
```

```python
import functools

import numpy as np
import jax
import jax.numpy as jnp
from jax import lax
from jax.experimental import pallas as pl
from jax.experimental.pallas import tpu as pltpu

GRID_W = 64
NA_HEADS = 8
NA_HEAD_DIM = 64
NA_KH = 8
NA_KW = 16
ML_HEADS = 4
ML_HEAD_DIM = 128
ML_CHUNK = 128
ROPE_THETA = 10000.0
SG_CHUNK = 128
SG_GROUPS = 8
EPS = 1e-6
NA_WIDTH = NA_HEADS * NA_HEAD_DIM
ML_WIDTH = ML_HEADS * ML_HEAD_DIM
N_GATES = 4 * ML_HEADS

V7X_LANES = 128
V7X_SUBLANES = 8
V7X_VMEM_LIMIT_BYTES = 56 * 1024 * 1024

NEG_BIG = -1e30
NA_ROWS_PER_BLOCK = 8
NA_BAND_ROWS = 16

BF16 = jnp.bfloat16
F32 = jnp.float32


def _dot(a, b):
    return jnp.dot(a, b, preferred_element_type=F32)


def _dot_nt(a, b):
    return lax.dot_general(a, b, (((1,), (1,)), ((), ())), preferred_element_type=F32)


def _dot_tn(a, b):
    return lax.dot_general(a, b, (((0,), (0,)), ((), ())), preferred_element_type=F32)


def _params(n_axes):
    return pltpu.CompilerParams(
        dimension_semantics=("arbitrary",) * n_axes,
        vmem_limit_bytes=V7X_VMEM_LIMIT_BYTES,
    )


def _modulated(x, g, mod, d):
    shift = mod[:, :d]
    scale = mod[:, d:2 * d]
    gs = g * (1.0 + scale)
    ms = jnp.mean(x * x, axis=-1, keepdims=True)
    return x * lax.rsqrt(ms + EPS) * gs + shift


def _mod_body(c_ref, w_ref, b_ref, o_ref):
    c = c_ref[...]
    s = (c * jax.nn.sigmoid(c)).astype(BF16)
    o_ref[...] = _dot(s, w_ref[...].astype(BF16)) + b_ref[...]


def _mod_vectors(cvec, w_mod, b_mod):
    depth, d, width = w_mod.shape
    tn = width // 8
    return pl.pallas_call(
        _mod_body,
        out_shape=jax.ShapeDtypeStruct((depth, V7X_SUBLANES, width), F32),
        grid=(depth, width // tn),
        in_specs=[
            pl.BlockSpec((V7X_SUBLANES, d), lambda l, j: (0, 0)),
            pl.BlockSpec((None, d, tn), lambda l, j: (l, 0, j)),
            pl.BlockSpec((None, 1, tn), lambda l, j: (l, 0, j)),
        ],
        out_specs=pl.BlockSpec((None, V7X_SUBLANES, tn), lambda l, j: (l, 0, j)),
        compiler_params=_params(2),
        name="mod_vectors",
    )(cvec, w_mod, b_mod.reshape(depth, 1, width))


def _ffn_body(x_ref, mod_ref, g_ref, win_ref, wout_ref, *rest, d, d_ff, chunk, final):
    if final:
        fg_ref, o_ref = rest
    else:
        (o_ref,) = rest
    x = x_ref[...]
    mod = mod_ref[...]
    h = _modulated(x, g_ref[...], mod, d).astype(BF16)
    acc = None
    for c in range(d_ff // chunk):
        a = _dot(h, win_ref[:, c * chunk:(c + 1) * chunk])
        b = _dot(h, win_ref[:, d_ff + c * chunk:d_ff + (c + 1) * chunk])
        t = (a * jax.nn.sigmoid(a) * b).astype(BF16)
        y = _dot(t, wout_ref[c * chunk:(c + 1) * chunk, :])
        acc = y if acc is None else acc + y
    out = x + (0.5 * mod[:, 2 * d:]) * acc
    if final:
        ms = jnp.mean(out * out, axis=-1, keepdims=True)
        out = out * lax.rsqrt(ms + EPS) * fg_ref[...]
    o_ref[...] = out


def _ffn(x, mod3, mod_row, g, w_in, w_out, final_g=None, tm=512):
    bsz, n, d = x.shape
    d_ff = w_out.shape[0]
    tm = min(tm, n)
    final = final_g is not None
    in_specs = [
        pl.BlockSpec((None, tm, d), lambda b, i: (b, i, 0)),
        pl.BlockSpec((None, 1, 3 * d), lambda b, i: (mod_row(b), 0, 0)),
        pl.BlockSpec((1, d), lambda b, i: (0, 0)),
        pl.BlockSpec((d, 2 * d_ff), lambda b, i: (0, 0)),
        pl.BlockSpec((d_ff, d), lambda b, i: (0, 0)),
    ]
    args = [x, mod3, g.reshape(1, d), w_in, w_out]
    if final:
        in_specs.append(pl.BlockSpec((1, d), lambda b, i: (0, 0)))
        args.append(final_g.reshape(1, d))
    return pl.pallas_call(
        functools.partial(_ffn_body, d=d, d_ff=d_ff, chunk=256, final=final),
        out_shape=jax.ShapeDtypeStruct(x.shape, F32),
        grid=(bsz, n // tm),
        in_specs=in_specs,
        out_specs=pl.BlockSpec((None, tm, d), lambda b, i: (b, i, 0)),
        compiler_params=_params(2),
        name="macaron_ffn",
    )(*args)


def _log_gates(z, is_forget):
    ls = jnp.minimum(z, 0.0) - jnp.log1p(jnp.exp(-jnp.abs(z)))
    return jnp.where(is_forget, ls, z)


def _evenproj_body(x_ref, mod_ref, g_ref, w_ref, wg_ref, wgt_ref, gbc_ref, gbr_ref, cos_ref, sin_ref,
                   qa_ref, ka_ref, va_ref, qb_ref, kb_ref, vb_ref, ob_ref, gc_ref, gr_ref, *, d, rope):
    h = _modulated(x_ref[...], g_ref[...], mod_ref[...], d).astype(BF16)
    nw, mw, hd = NA_WIDTH, ML_WIDTH, ML_HEAD_DIM
    qa_ref[...] = (_dot(h, w_ref[:, 0:nw]) * (NA_HEAD_DIM ** -0.5)).astype(BF16)
    ka_ref[...] = _dot(h, w_ref[:, nw:2 * nw]).astype(BF16)
    va_ref[...] = _dot(h, w_ref[:, 2 * nw:3 * nw]).astype(BF16)
    base = 3 * nw
    qb = _dot(h, w_ref[:, base:base + mw])
    kb = _dot(h, w_ref[:, base + mw:base + 2 * mw])
    if rope:
        cos = cos_ref[...]
        sin = sin_ref[...]
        for hh in range(ML_HEADS):
            sl = slice(hh * hd, (hh + 1) * hd)
            qh = qb[:, sl]
            kh = kb[:, sl]
            qb_ref[:, sl] = (qh * cos + pltpu.roll(qh, hd // 2, 1) * sin).astype(BF16)
            kb_ref[:, sl] = ((kh * cos + pltpu.roll(kh, hd // 2, 1) * sin) * (hd ** -0.5)).astype(BF16)
    else:
        qb_ref[...] = qb.astype(BF16)
        kb_ref[...] = (kb * (hd ** -0.5)).astype(BF16)
    vb_ref[...] = _dot(h, w_ref[:, base + 2 * mw:base + 3 * mw]).astype(BF16)
    ob_ref[...] = _dot(h, w_ref[:, base + 3 * mw:base + 4 * mw])
    zc = _dot(h, wg_ref[...]) + gbc_ref[...]
    lane = lax.broadcasted_iota(jnp.int32, zc.shape, 1)
    gc_ref[...] = _log_gates(zc, (lane % 2) == 1)
    zr = _dot_nt(wgt_ref[...], h) + gbr_ref[...]
    row = lax.broadcasted_iota(jnp.int32, zr.shape, 0)
    gr_ref[...] = _log_gates(zr, (row % 2) == 1)


def _even_projection(x, mod3, mod_row, g, w_main, w_gc, w_gr, gb_col, gb_row, cos, sin, rope, tm=512):
    bsz, n, d = x.shape
    tm = min(tm, n)
    wm = w_main.shape[1]
    tok = lambda width, dt: jax.ShapeDtypeStruct((bsz, n, width), dt)
    tok_spec = lambda width: pl.BlockSpec((None, tm, width), lambda b, i: (b, i, 0))
    const = lambda shape: pl.BlockSpec(shape, lambda b, i: (0,) * len(shape))
    out_shape = [tok(NA_WIDTH, BF16)] * 3 + [tok(ML_WIDTH, BF16)] * 3 + [tok(ML_WIDTH, F32),
                 tok(V7X_LANES, F32), jax.ShapeDtypeStruct((bsz, N_GATES, n), F32)]
    out_specs = [tok_spec(NA_WIDTH)] * 3 + [tok_spec(ML_WIDTH)] * 4 + [
        tok_spec(V7X_LANES), pl.BlockSpec((None, N_GATES, tm), lambda b, i: (b, 0, i))]
    return pl.pallas_call(
        functools.partial(_evenproj_body, d=d, rope=rope),
        out_shape=out_shape,
        grid=(bsz, n // tm),
        in_specs=[
            tok_spec(d),
            pl.BlockSpec((None, 1, 3 * d), lambda b, i: (mod_row(b), 0, 0)),
            const((1, d)),
            const((d, wm)),
            const((d, V7X_LANES)),
            const((N_GATES, d)),
            const((1, V7X_LANES)),
            const((N_GATES, 1)),
            pl.BlockSpec((tm, ML_HEAD_DIM), lambda b, i: (i, 0)),
            pl.BlockSpec((tm, ML_HEAD_DIM), lambda b, i: (i, 0)),
        ],
        out_specs=out_specs,
        compiler_params=_params(2),
        name="even_projection",
    )(x, mod3, g.reshape(1, d), w_main, w_gc, w_gr, gb_col, gb_row, cos, sin)


def _na_body(q_ref, k_ref, v_ref, kx_ref, vx_ref, bias_ref, o_ref, *, rows):
    rb = pl.program_id(2)
    r, kr = NA_ROWS_PER_BLOCK, NA_BAND_ROWS
    band0 = jnp.clip(rb * r - NA_KH // 2, 0, rows - kr)
    start = pl.multiple_of(band0 * GRID_W, GRID_W)
    kband = k_ref[pl.ds(start, kr * GRID_W), :]
    vband = v_ref[pl.ds(start, kr * GRID_W), :]
    kx = kx_ref[...]
    vx = vx_ref[...]
    q = q_ref[...]
    lane = lax.broadcasted_iota(jnp.int32, q.shape, 1)
    outs = []
    for hh in range(2):
        in_head = (lane // NA_HEAD_DIM) == hh
        qm = jnp.where(in_head, q, jnp.zeros_like(q))
        s_loc = _dot_nt(qm, kband) + bias_ref[hh]
        s_ctx = _dot_nt(qm, kx)
        m = jnp.maximum(jnp.max(s_loc, axis=-1, keepdims=True), jnp.max(s_ctx, axis=-1, keepdims=True))
        p_loc = jnp.exp(s_loc - m)
        p_ctx = jnp.exp(s_ctx - m)
        denom = jnp.sum(p_loc, axis=-1, keepdims=True) + jnp.sum(p_ctx, axis=-1, keepdims=True)
        o = _dot(p_loc.astype(BF16), vband) + _dot(p_ctx.astype(BF16), vx)
        outs.append(o * (1.0 / denom))
    lane_o = lax.broadcasted_iota(jnp.int32, outs[0].shape, 1)
    o_ref[...] = jnp.where(lane_o < NA_HEAD_DIM, outs[0], outs[1]).astype(BF16)


def _na_bias_tables(rpb, rows):
    r, kr, w = NA_ROWS_PER_BLOCK, NA_BAND_ROWS, GRID_W
    kh, kw = NA_KH, NA_KW
    cols = np.arange(w)
    c0 = np.clip(cols - kw // 2, 0, w - kw)
    cc = np.arange(w)[None, :]
    col_ok = (cc >= c0[:, None]) & (cc < c0[:, None] + kw)
    col_rel = np.clip(cc - cols[:, None] + (NA_KW - 1), 0, 2 * NA_KW - 2)
    t1 = jnp.where(col_ok[None, None], rpb[:, :, col_rel], NEG_BIG)
    masked = jnp.full((rpb.shape[0], 1, w, w), NEG_BIG, F32)
    t1 = jnp.concatenate([t1, masked], axis=1)
    variants = []
    for i0, b0 in ((0, 0), (r, r - kh // 2), (rows - r, rows - kr)):
        qi = i0 + np.arange(r)
        rr = b0 + np.arange(kr)
        r0 = np.clip(qi - kh // 2, 0, rows - kh)
        ok = (rr[None, :] >= r0[:, None]) & (rr[None, :] < r0[:, None] + kh)
        rel = rr[None, :] - qi[:, None] + (NA_KH - 1)
        a_idx = np.where(ok, rel, 2 * NA_KH - 1).astype(np.int32)
        blk = t1[:, a_idx]
        variants.append(blk.transpose(0, 1, 3, 2, 4).reshape(rpb.shape[0], r * w, kr * w))
    return jnp.stack(variants)


def _neighbourhood_attention(q, k, v, kx, vx, bias):
    bsz, n, _ = q.shape
    lc = kx.shape[1]
    rows = n // GRID_W
    r, kr = NA_ROWS_PER_BLOCK, NA_BAND_ROWS
    nrb = rows // r
    tq = r * GRID_W
    pair = 2 * NA_HEAD_DIM
    variant = lambda rb: jnp.where(rb == 0, 0, jnp.where(rb == nrb - 1, 2, 1))
    return pl.pallas_call(
        functools.partial(_na_body, rows=rows),
        out_shape=jax.ShapeDtypeStruct((bsz, n, NA_WIDTH), BF16),
        grid=(bsz, NA_HEADS // 2, nrb),
        in_specs=[
            pl.BlockSpec((None, tq, pair), lambda b, hp, rb: (b, rb, hp)),
            pl.BlockSpec((None, n, pair), lambda b, hp, rb: (b, 0, hp)),
            pl.BlockSpec((None, n, pair), lambda b, hp, rb: (b, 0, hp)),
            pl.BlockSpec((None, lc, pair), lambda b, hp, rb: (b, 0, hp)),
            pl.BlockSpec((None, lc, pair), lambda b, hp, rb: (b, 0, hp)),
            pl.BlockSpec((None, 2, tq, kr * GRID_W), lambda b, hp, rb: (variant(rb), hp, 0, 0)),
        ],
        out_specs=pl.BlockSpec((None, tq, pair), lambda b, hp, rb: (b, rb, hp)),
        compiler_params=_params(3),
        name="neighbourhood_attention",
    )(q, k, v, kx, vx, bias)


def _mlstm_body(*refs, with_output):
    if with_output:
        (qf_ref, kf_ref, vf_ref, gcf_ref, grf_ref, qr_ref, kr_ref, vr_ref, gcr_ref, grr_ref,
         c0_ref, m0_ref, hf_ref, hr_ref, c_scr, m_scr) = refs
    else:
        (kf_ref, vf_ref, gcf_ref, grf_ref, kr_ref, vr_ref, gcr_ref, grr_ref,
         c0_ref, m0_ref, c_out_ref, m_out_ref, c_scr, m_scr) = refs
    step = pl.program_id(1)
    n_steps = pl.num_programs(1)
    L, hd = ML_CHUNK, ML_HEAD_DIM

    @pl.when(step == 0)
    def _():
        c_scr[...] = c0_ref[...]
        m_scr[...] = m0_ref[...]

    t_idx = lax.broadcasted_iota(jnp.int32, (L, L), 0)
    s_idx = lax.broadcasted_iota(jnp.int32, (L, L), 1)
    ones_col = (lax.broadcasted_iota(jnp.int32, (L, hd), 1) == 0).astype(F32)
    hi = lax.Precision.HIGHEST
    for d in range(2):
        if d == 0:
            causal = s_idx <= t_idx
            k_ref, v_ref, gc_ref, gr_ref = kf_ref, vf_ref, gcf_ref, grf_ref
            q_ref = qf_ref if with_output else None
            h_ref = hf_ref if with_output else None
        else:
            causal = s_idx >= t_idx
            k_ref, v_ref, gc_ref, gr_ref = kr_ref, vr_ref, gcr_ref, grr_ref
            q_ref = qr_ref if with_output else None
            h_ref = hr_ref if with_output else None
        tri_c = causal.astype(F32)
        tri_r = jnp.logical_not(causal).astype(F32) + (s_idx == t_idx).astype(F32)
        gcol = gc_ref[...]
        grow = gr_ref[...]
        bcol = jnp.dot(tri_c, gcol, precision=hi, preferred_element_type=F32)
        brow = jnp.dot(grow, tri_r, precision=hi, preferred_element_type=F32)
        end = L - 1 if d == 0 else 0
        for hh in range(ML_HEADS):
            gi = hh * 4 + d * 2
            sl = slice(hh * hd, (hh + 1) * hd)
            i_row = grow[gi:gi + 1, :]
            b_row = brow[gi + 1:gi + 2, :]
            i_col = gcol[:, gi:gi + 1]
            b_col = bcol[:, gi + 1:gi + 2]
            b_end = b_row[:, end:end + 1]
            m_old = m_scr[d, hh][0:1, 0:1]
            w_end_row = b_end - b_row + i_row
            m_new = jnp.maximum(b_end + m_old, jnp.max(w_end_row, axis=-1, keepdims=True))
            a_prev = jnp.exp(b_end + m_old - m_new)
            a_tok = jnp.exp(b_end - b_col + i_col - m_new)
            k = k_ref[:, sl]
            v = v_ref[:, sl].astype(F32)
            c_old = c_scr[d, hh]
            if with_output:
                q = q_ref[:, sl]
                log_w = jnp.where(causal, b_col - b_row + i_row, NEG_BIG)
                log_inter = b_col + m_old
                m_t = jnp.maximum(log_inter, jnp.max(log_w, axis=-1, keepdims=True))
                w_intra = jnp.exp(log_w - m_t)
                w_inter = jnp.exp(log_inter - m_t)
                s = (_dot_nt(q, k) * w_intra).astype(BF16)
                v_aug = jnp.concatenate([v, ones_col], axis=1).astype(BF16)
                both = w_inter * _dot(q, c_old.astype(BF16)) + _dot(s, v_aug)
                den = both[:, hd:hd + 1]
                scale = 1.0 / jnp.maximum(jnp.abs(den), jnp.exp(-m_t))
                h_ref[:, sl] = both[:, :hd] * scale
            av_aug = jnp.concatenate([a_tok * v, a_tok * ones_col], axis=1).astype(BF16)
            c_scr[d, hh] = a_prev * c_old + _dot_tn(k, av_aug)
            m_scr[d, hh] = jnp.broadcast_to(m_new, (V7X_SUBLANES, V7X_LANES))

    if not with_output:
        @pl.when(step == n_steps - 1)
        def _():
            c_out_ref[...] = c_scr[...]
            m_out_ref[...] = m_scr[...]


def _mlstm_scan(q, k, v, gcol, grow, c0, m0, with_output):
    bsz, n, _ = k.shape
    L, hd = ML_CHUNK, ML_HEAD_DIM
    nch = n // L
    fwd = lambda width: pl.BlockSpec((None, L, width), lambda b, c: (b, c, 0))
    rev = lambda width: pl.BlockSpec((None, L, width), lambda b, c: (b, nch - 1 - c, 0))
    grow_f = pl.BlockSpec((None, N_GATES, L), lambda b, c: (b, 0, c))
    grow_r = pl.BlockSpec((None, N_GATES, L), lambda b, c: (b, 0, nch - 1 - c))
    c_spec = pl.BlockSpec((None, 2, ML_HEADS, hd, 2 * hd), lambda b, c: (b, 0, 0, 0, 0))
    m_spec = pl.BlockSpec((None, 2, ML_HEADS, V7X_SUBLANES, V7X_LANES), lambda b, c: (b, 0, 0, 0, 0))
    if with_output:
        args = [q, k, v, gcol, grow, q, k, v, gcol, grow, c0, m0]
        in_specs = [fwd(ML_WIDTH)] * 3 + [fwd(V7X_LANES), grow_f] + [rev(ML_WIDTH)] * 3 + [rev(V7X_LANES), grow_r]
        out_shape = [jax.ShapeDtypeStruct((bsz, n, ML_WIDTH), F32)] * 2
        out_specs = [fwd(ML_WIDTH), rev(ML_WIDTH)]
    else:
        args = [k, v, gcol, grow, k, v, gcol, grow, c0, m0]
        in_specs = [fwd(ML_WIDTH)] * 2 + [fwd(V7X_LANES), grow_f] + [rev(ML_WIDTH)] * 2 + [rev(V7X_LANES), grow_r]
        out_shape = [jax.ShapeDtypeStruct(c0.shape, F32), jax.ShapeDtypeStruct(m0.shape, F32)]
        out_specs = [c_spec, m_spec]
    return pl.pallas_call(
        functools.partial(_mlstm_body, with_output=with_output),
        out_shape=out_shape,
        grid=(bsz, nch),
        in_specs=in_specs + [c_spec, m_spec],
        out_specs=out_specs,
        scratch_shapes=[pltpu.VMEM((2, ML_HEADS, hd, 2 * hd), F32),
                        pltpu.VMEM((2, ML_HEADS, V7X_SUBLANES, V7X_LANES), F32)],
        compiler_params=_params(2),
        name="mlstm_scan_latent" if with_output else "mlstm_scan_context",
    )(*args)


def _evenout_body(x_ref, mod_ref, ya_ref, hf_ref, hr_ref, ob_ref, hg_ref, w_ref, o_ref, *, d):
    hd = ML_HEAD_DIM
    hb = hf_ref[...] + hr_ref[...]
    ob = ob_ref[...]
    hg = hg_ref[...]
    y = _dot(ya_ref[...], w_ref[0:NA_WIDTH, :])
    for hh in range(ML_HEADS):
        sl = slice(hh * hd, (hh + 1) * hd)
        hs = hb[:, sl]
        ms = jnp.mean(hs * hs, axis=-1, keepdims=True)
        yb = hs * lax.rsqrt(ms + EPS) * hg[:, sl] * jax.nn.sigmoid(ob[:, sl])
        y = y + _dot(yb.astype(BF16), w_ref[NA_WIDTH + hh * hd:NA_WIDTH + (hh + 1) * hd, :])
    o_ref[...] = x_ref[...] + mod_ref[...][:, 2 * d:] * y


def _even_output(x, mod3, ya, hf, hr, ob, head_g, w_out, tm=512):
    bsz, n, d = x.shape
    tok_spec = lambda width: pl.BlockSpec((None, tm, width), lambda b, i: (b, i, 0))
    return pl.pallas_call(
        functools.partial(_evenout_body, d=d),
        out_shape=jax.ShapeDtypeStruct(x.shape, F32),
        grid=(bsz, n // tm),
        in_specs=[
            tok_spec(d),
            pl.BlockSpec((None, 1, 3 * d), lambda b, i: (b, 0, 0)),
            tok_spec(NA_WIDTH), tok_spec(ML_WIDTH), tok_spec(ML_WIDTH), tok_spec(ML_WIDTH),
            pl.BlockSpec((1, ML_WIDTH), lambda b, i: (0, 0)),
            pl.BlockSpec(w_out.shape, lambda b, i: (0, 0)),
        ],
        out_specs=tok_spec(d),
        compiler_params=_params(2),
        name="even_output",
    )(x, mod3, ya, hf, hr, ob, head_g.reshape(1, ML_WIDTH), w_out)


def _sg_body(x_ref, mod_ref, g_ref, win_ref, lng_ref, lnb_ref, ws_ref, bs_ref, wout_ref, o_ref, *, d, width, tm):
    x = x_ref[...]
    mod = mod_ref[...]
    h = _modulated(x, g_ref[...], mod, d).astype(BF16)
    u = jax.nn.gelu(_dot(h, win_ref[:, 0:width]))
    v = jax.nn.gelu(_dot(h, win_ref[:, width:2 * width]))
    mu = jnp.mean(v, axis=-1, keepdims=True)
    vc = v - mu
    var = jnp.mean(vc * vc, axis=-1, keepdims=True)
    vn = (vc * lax.rsqrt(var + EPS) * lng_ref[...] + lnb_ref[...]).astype(BF16)
    gw = width // SG_GROUPS
    bs = bs_ref[...]
    y = None
    for g in range(SG_GROUPS):
        cs = slice(g * gw, (g + 1) * gw)
        parts = []
        for c in range(tm // SG_CHUNK):
            rs = slice(c * SG_CHUNK, (c + 1) * SG_CHUNK)
            mixed = _dot(ws_ref[g], vn[rs, cs]) + bs[:, g:g + 1]
            parts.append((u[rs, cs] * mixed).astype(BF16))
        t = parts[0] if len(parts) == 1 else jnp.concatenate(parts, axis=0)
        yg = _dot(t, wout_ref[cs, :])
        y = yg if y is None else y + yg
    o_ref[...] = x + mod[:, 2 * d:] * y


def _spatial_gating(x, mod3, g, w_in, ln_g, ln_b, w_s, b_s, w_out, tm=256):
    bsz, n, d = x.shape
    width = w_out.shape[0]
    const = lambda shape: pl.BlockSpec(shape, lambda b, i: (0,) * len(shape))
    return pl.pallas_call(
        functools.partial(_sg_body, d=d, width=width, tm=tm),
        out_shape=jax.ShapeDtypeStruct(x.shape, F32),
        grid=(bsz, n // tm),
        in_specs=[
            pl.BlockSpec((None, tm, d), lambda b, i: (b, i, 0)),
            pl.BlockSpec((None, 1, 3 * d), lambda b, i: (b, 0, 0)),
            const((1, d)), const(w_in.shape), const((1, width)), const((1, width)),
            const(w_s.shape), const((SG_CHUNK, SG_GROUPS)), const(w_out.shape),
        ],
        out_specs=pl.BlockSpec((None, tm, d), lambda b, i: (b, i, 0)),
        compiler_params=_params(2),
        name="spatial_gating",
    )(x, mod3, g.reshape(1, d), w_in, ln_g.reshape(1, width), ln_b.reshape(1, width), w_s, b_s.T, w_out)


def _rope_tables(n):
    hd = ML_HEAD_DIM
    n_pairs = hd // 4
    pos = jnp.arange(n)
    row = (pos // GRID_W).astype(F32)
    col = (pos % GRID_W).astype(F32)
    inv_freq = ROPE_THETA ** (-jnp.arange(n_pairs, dtype=F32) / n_pairs)
    ang = jnp.concatenate([row[:, None] * inv_freq, col[:, None] * inv_freq], axis=-1)
    cos, sin = jnp.cos(ang), jnp.sin(ang)
    return jnp.concatenate([cos, cos], axis=-1), jnp.concatenate([-sin, sin], axis=-1)


def _even_weights(w_in, gate_b):
    d = w_in.shape[0]
    hd = ML_HEAD_DIM
    deint = np.concatenate([np.arange(0, hd, 2), np.arange(1, hd, 2)])
    perm = np.concatenate([h * hd + deint for h in range(ML_HEADS)])
    base = 3 * NA_WIDTH
    qb = w_in[:, base:base + ML_WIDTH][:, perm]
    kb = w_in[:, base + ML_WIDTH:base + 2 * ML_WIDTH][:, perm]
    w_main = jnp.concatenate([w_in[:, :base], qb, kb, w_in[:, base + 2 * ML_WIDTH:base + 4 * ML_WIDTH]],
                             axis=1).astype(BF16)
    wg = w_in[:, base + 4 * ML_WIDTH:]
    w_gc = jnp.pad(wg, ((0, 0), (0, V7X_LANES - N_GATES))).astype(BF16)
    w_gr = wg.T.astype(BF16)
    gb = gate_b.reshape(N_GATES).astype(F32)
    gb_col = jnp.pad(gb, (0, V7X_LANES - N_GATES)).reshape(1, V7X_LANES)
    gb_row = gb.reshape(N_GATES, 1)
    return w_main, w_gc, w_gr, gb_col, gb_row


def kernel(x, c, ctx, c_ctx, w_mod, b_mod, norm_g, ffn_w_in, ffn_w_out, mix_w_in, na_rpb, ml_gate_b, ml_head_g,
           mix_w_out, sg_w_in, sg_ln_g, sg_ln_b, sg_w_s, sg_b_s, sg_w_out, final_g):
    bsz, n, d = x.shape
    depth = w_mod.shape[0]
    ctx_row = bsz
    cvec = jnp.zeros((V7X_SUBLANES, d), F32).at[:bsz].set(c).at[ctx_row].set(c_ctx)
    mod = _mod_vectors(cvec, w_mod, b_mod).reshape(depth, V7X_SUBLANES, 3, 1, 3 * d)
    batch_row = lambda b: b
    context_row = lambda b: ctx_row
    last_ctx_layer = ((depth - 1) // 2) * 2
    xc = ctx
    for l in range(depth):
        ctx_in = l <= last_ctx_layer
        ctx_out = l < last_ctx_layer
        w_in_a, w_out_a = ffn_w_in[l, 0].astype(BF16), ffn_w_out[l, 0].astype(BF16)
        w_in_b, w_out_b = ffn_w_in[l, 1].astype(BF16), ffn_w_out[l, 1].astype(BF16)
        x = _ffn(x, mod[l, :, 0], batch_row, norm_g[l, 0], w_in_a, w_out_a)
        if ctx_in:
            xc = _ffn(xc, mod[l, :, 0], context_row, norm_g[l, 0], w_in_a, w_out_a)
        if l % 2 == 0:
            e = l // 2
            w_main, w_gc, w_gr, gb_col, gb_row = _even_weights(mix_w_in[e], ml_gate_b[e])
            cos, sin = _rope_tables(n)
            qa, ka, va, qb, kb, vb, ob, gcol, grow = _even_projection(
                x, mod[l, :, 1], batch_row, norm_g[l, 1], w_main, w_gc, w_gr, gb_col, gb_row, cos, sin, True)
            lc = xc.shape[1]
            ones = jnp.ones((lc, ML_HEAD_DIM), F32)
            _, kax, vax, _, kbx, vbx, _, gcolx, growx = _even_projection(
                xc, mod[l, :, 1], context_row, norm_g[l, 1], w_main, w_gc, w_gr, gb_col, gb_row, ones, ones, False)
            bias = _na_bias_tables(na_rpb[e], n // GRID_W)
            ya = _neighbourhood_attention(qa, ka, va, kax, vax, bias)
            c0 = jnp.zeros((bsz, 2, ML_HEADS, ML_HEAD_DIM, 2 * ML_HEAD_DIM), F32)
            m0 = jnp.zeros((bsz, 2, ML_HEADS, V7X_SUBLANES, V7X_LANES), F32)
            c1, m1 = _mlstm_scan(None, kbx, vbx, gcolx, growx, c0, m0, False)
            hf, hr = _mlstm_scan(qb, kb, vb, gcol, grow, c1, m1, True)
            x = _even_output(x, mod[l, :, 1], ya, hf, hr, ob, ml_head_g[e], mix_w_out[e].astype(BF16))
            assert not ctx_out, "context output path is not needed for this depth"
        else:
            o = l // 2
            x = _spatial_gating(x, mod[l, :, 1], norm_g[l, 1], sg_w_in[o].astype(BF16), sg_ln_g[o], sg_ln_b[o],
                                sg_w_s[o].astype(BF16), sg_b_s[o], sg_w_out[o].astype(BF16))
        fg = final_g if l == depth - 1 else None
        x = _ffn(x, mod[l, :, 2], batch_row, norm_g[l, 2], w_in_b, w_out_b, final_g=fg)
    return x
```

```python
import functools

import numpy as np
import jax
import jax.numpy as jnp
from jax import lax
from jax.experimental import pallas as pl
from jax.experimental.pallas import tpu as pltpu

GRID_W = 64
NA_HEADS = 8
NA_HEAD_DIM = 64
NA_KH = 8
NA_KW = 16
ML_HEADS = 4
ML_HEAD_DIM = 128
ML_CHUNK = 128
ROPE_THETA = 10000.0
SG_CHUNK = 128
SG_GROUPS = 8
EPS = 1e-6
NA_WIDTH = NA_HEADS * NA_HEAD_DIM
ML_WIDTH = ML_HEADS * ML_HEAD_DIM
N_GATES = 4 * ML_HEADS

V7X_LANES = 128
V7X_SUBLANES = 8
V7X_MXU_COLUMNS = 256
V7X_VMEM_LIMIT_BYTES = 56 * 1024 * 1024

NEG_BIG = -1e30
NA_ROWS_PER_STEP = 8
NA_SUB_ROWS = 4
NA_BAND_ROWS = 12

BF16 = jnp.bfloat16
F32 = jnp.float32


def _dot(a, b):
    return jnp.dot(a, b, preferred_element_type=F32)


def _dot_nt(a, b):
    return lax.dot_general(a, b, (((1,), (1,)), ((), ())), preferred_element_type=F32)


def _dot_tn(a, b):
    return lax.dot_general(a, b, (((0,), (0,)), ((), ())), preferred_element_type=F32)


def _params(n_axes):
    return pltpu.CompilerParams(
        dimension_semantics=("arbitrary",) * n_axes,
        vmem_limit_bytes=V7X_VMEM_LIMIT_BYTES,
    )


def _modulated(x, g, mod, d):
    shift = mod[:, :d]
    scale = mod[:, d:2 * d]
    gs = g * (1.0 + scale)
    ms = jnp.mean(x * x, axis=-1, keepdims=True)
    return x * lax.rsqrt(ms + EPS) * gs + shift


def _mod_body(c_ref, w_ref, b_ref, o_ref):
    c = c_ref[...]
    s = (c * jax.nn.sigmoid(c)).astype(BF16)
    o_ref[...] = _dot(s, w_ref[...].astype(BF16)) + b_ref[...]


def _mod_vectors(cvec, w_mod, b_mod):
    depth, d, width = w_mod.shape
    tn = width // 8
    return pl.pallas_call(
        _mod_body,
        out_shape=jax.ShapeDtypeStruct((depth, V7X_SUBLANES, width), F32),
        grid=(depth, width // tn),
        in_specs=[
            pl.BlockSpec((V7X_SUBLANES, d), lambda l, j: (0, 0)),
            pl.BlockSpec((None, d, tn), lambda l, j: (l, 0, j)),
            pl.BlockSpec((None, 1, tn), lambda l, j: (l, 0, j)),
        ],
        out_specs=pl.BlockSpec((None, V7X_SUBLANES, tn), lambda l, j: (l, 0, j)),
        compiler_params=_params(2),
        name="mod_vectors",
    )(cvec, w_mod, b_mod.reshape(depth, 1, width))


def _ffn_body(x_ref, mod_ref, g_ref, win_ref, wout_ref, *rest, d, d_ff, chunk, final):
    if final:
        fg_ref, o_ref = rest
    else:
        (o_ref,) = rest
    x = x_ref[...]
    mod = mod_ref[...]
    h = _modulated(x, g_ref[...], mod, d).astype(BF16)
    acc = None
    for c in range(d_ff // chunk):
        a = _dot(h, win_ref[:, c * chunk:(c + 1) * chunk])
        b = _dot(h, win_ref[:, d_ff + c * chunk:d_ff + (c + 1) * chunk])
        t = (a * jax.nn.sigmoid(a) * b).astype(BF16)
        y = _dot(t, wout_ref[c * chunk:(c + 1) * chunk, :])
        acc = y if acc is None else acc + y
    out = x + (0.5 * mod[:, 2 * d:]) * acc
    if final:
        ms = jnp.mean(out * out, axis=-1, keepdims=True)
        out = out * lax.rsqrt(ms + EPS) * fg_ref[...]
    o_ref[...] = out


def _ffn(x, mod3, mod_row, g, w_in_all, w_out_all, layer, half, final_g=None, tm=512):
    bsz, n, d = x.shape
    d_ff = w_out_all.shape[2]
    tm = min(tm, n)
    final = final_g is not None
    in_specs = [
        pl.BlockSpec((None, tm, d), lambda b, i: (b, i, 0)),
        pl.BlockSpec((None, 1, 3 * d), lambda b, i: (mod_row(b), 0, 0)),
        pl.BlockSpec((1, d), lambda b, i: (0, 0)),
        pl.BlockSpec((None, None, d, 2 * d_ff), lambda b, i: (layer, half, 0, 0)),
        pl.BlockSpec((None, None, d_ff, d), lambda b, i: (layer, half, 0, 0)),
    ]
    args = [x, mod3, g.reshape(1, d), w_in_all, w_out_all]
    if final:
        in_specs.append(pl.BlockSpec((1, d), lambda b, i: (0, 0)))
        args.append(final_g.reshape(1, d))
    return pl.pallas_call(
        functools.partial(_ffn_body, d=d, d_ff=d_ff, chunk=V7X_MXU_COLUMNS, final=final),
        out_shape=jax.ShapeDtypeStruct(x.shape, F32),
        grid=(bsz, n // tm),
        in_specs=in_specs,
        out_specs=pl.BlockSpec((None, tm, d), lambda b, i: (b, i, 0)),
        compiler_params=_params(2),
        name="macaron_ffn",
    )(*args)


def _log_gates(z, is_forget):
    ls = jnp.minimum(z, 0.0) - jnp.log1p(jnp.exp(-jnp.abs(z)))
    return jnp.where(is_forget, ls, z)


def _evenproj_body(x_ref, mod_ref, g_ref, w_ref, wg_ref, wgt_ref, gbc_ref, gbr_ref, rowt_ref, colt_ref,
                   qa_ref, ka_ref, va_ref, qb_ref, kb_ref, vb_ref, ob_ref, gc_ref, gr_ref, *, d, rope):
    h = _modulated(x_ref[...], g_ref[...], mod_ref[...], d).astype(BF16)
    nw, mw, hd = NA_WIDTH, ML_WIDTH, ML_HEAD_DIM
    qa_ref[...] = (_dot(h, w_ref[:, 0:nw]) * (NA_HEAD_DIM ** -0.5)).astype(BF16)
    ka_ref[...] = _dot(h, w_ref[:, nw:2 * nw]).astype(BF16)
    va_ref[...] = _dot(h, w_ref[:, 2 * nw:3 * nw]).astype(BF16)
    base = 3 * nw
    qb = _dot(h, w_ref[:, base:base + mw])
    kb = _dot(h, w_ref[:, base + mw:base + 2 * mw])
    if rope:
        tm = qb.shape[0]
        expand = lambda t: jnp.broadcast_to(t[:, None, :], (tm // GRID_W, GRID_W, hd)).reshape(tm, hd)
        cos = expand(rowt_ref[0]) + colt_ref[0]
        sin = expand(rowt_ref[1]) + colt_ref[1]
        for hh in range(ML_HEADS):
            sl = slice(hh * hd, (hh + 1) * hd)
            qh = qb[:, sl]
            kh = kb[:, sl]
            qb_ref[:, sl] = (qh * cos + pltpu.roll(qh, hd // 2, 1) * sin).astype(BF16)
            kb_ref[:, sl] = ((kh * cos + pltpu.roll(kh, hd // 2, 1) * sin) * (hd ** -0.5)).astype(BF16)
    else:
        qb_ref[...] = qb.astype(BF16)
        kb_ref[...] = (kb * (hd ** -0.5)).astype(BF16)
    vb_ref[...] = _dot(h, w_ref[:, base + 2 * mw:base + 3 * mw]).astype(BF16)
    ob_ref[...] = _dot(h, w_ref[:, base + 3 * mw:base + 4 * mw])
    zc = _dot(h, wg_ref[...]) + gbc_ref[...]
    lane = lax.broadcasted_iota(jnp.int32, zc.shape, 1)
    gc_ref[...] = _log_gates(zc, (lane % 2) == 1)
    zr = _dot_nt(wgt_ref[...], h) + gbr_ref[...]
    row = lax.broadcasted_iota(jnp.int32, zr.shape, 0)
    gr_ref[...] = _log_gates(zr, (row % 2) == 1)


def _even_projection(x, mod3, mod_row, g, w_main, w_gc, w_gr, gb_col, gb_row, rope_tables, tm=512):
    bsz, n, d = x.shape
    tm = min(tm, n)
    rope = rope_tables is not None
    if rope:
        row_t, col_t = rope_tables
        col_t = jnp.tile(col_t, (1, tm // GRID_W, 1))
    else:
        row_t = jnp.zeros((2, tm // GRID_W, ML_HEAD_DIM), F32)
        col_t = jnp.zeros((2, tm, ML_HEAD_DIM), F32)
    wm = w_main.shape[1]
    tok = lambda width, dt: jax.ShapeDtypeStruct((bsz, n, width), dt)
    tok_spec = lambda width: pl.BlockSpec((None, tm, width), lambda b, i: (b, i, 0))
    const = lambda shape: pl.BlockSpec(shape, lambda b, i: (0,) * len(shape))
    out_shape = [tok(NA_WIDTH, BF16)] * 3 + [tok(ML_WIDTH, BF16)] * 3 + [tok(ML_WIDTH, F32),
                 tok(V7X_LANES, F32), jax.ShapeDtypeStruct((bsz, N_GATES, n), F32)]
    out_specs = [tok_spec(NA_WIDTH)] * 3 + [tok_spec(ML_WIDTH)] * 4 + [
        tok_spec(V7X_LANES), pl.BlockSpec((None, N_GATES, tm), lambda b, i: (b, 0, i))]
    return pl.pallas_call(
        functools.partial(_evenproj_body, d=d, rope=rope),
        out_shape=out_shape,
        grid=(bsz, n // tm),
        in_specs=[
            tok_spec(d),
            pl.BlockSpec((None, 1, 3 * d), lambda b, i: (mod_row(b), 0, 0)),
            const((1, d)),
            const((d, wm)),
            const((d, V7X_LANES)),
            const((N_GATES, d)),
            const((1, V7X_LANES)),
            const((N_GATES, 1)),
            pl.BlockSpec((2, tm // GRID_W, ML_HEAD_DIM), lambda b, i: (0, i if rope else 0, 0)),
            pl.BlockSpec((2, tm, ML_HEAD_DIM), lambda b, i: (0, 0, 0)),
        ],
        out_specs=out_specs,
        compiler_params=_params(2),
        name="even_projection",
    )(x, mod3, g.reshape(1, d), w_main, w_gc, w_gr, gb_col, gb_row, row_t, col_t)


def _na_body(q_ref, k_ref, v_ref, kx_ref, vx_ref, bias0_ref, bias1_ref, o_ref, *, rows):
    rb = pl.program_id(2)
    sub, band = NA_SUB_ROWS, NA_BAND_ROWS
    n_sub = NA_ROWS_PER_STEP // sub
    tq = sub * GRID_W
    kx = kx_ref[...]
    vx = vx_ref[...]
    q = q_ref[...]
    lane = lax.broadcasted_iota(jnp.int32, q.shape, 1)
    bias_refs = (bias0_ref, bias1_ref)
    bands = []
    for u in range(n_sub):
        band0 = jnp.clip((rb * n_sub + u) * sub - NA_KH // 2, 0, rows - band)
        start = pl.multiple_of(band0 * GRID_W, GRID_W)
        bands.append((k_ref[pl.ds(start, band * GRID_W), :], v_ref[pl.ds(start, band * GRID_W), :]))
    outs = []
    for hh in range(2):
        qm = jnp.where((lane // NA_HEAD_DIM) == hh, q, jnp.zeros_like(q))
        s_ctx = _dot_nt(qm, kx)
        m_ctx = jnp.max(s_ctx, axis=-1, keepdims=True)
        s_locs, ms = [], []
        for u in range(n_sub):
            s_loc = _dot_nt(qm[u * tq:(u + 1) * tq], bands[u][0]) + bias_refs[u][hh]
            s_locs.append(s_loc)
            ms.append(jnp.maximum(jnp.max(s_loc, axis=-1, keepdims=True), m_ctx[u * tq:(u + 1) * tq]))
        m = jnp.concatenate(ms, axis=0)
        p_ctx = jnp.exp(s_ctx - m)
        o_ctx = _dot(p_ctx.astype(BF16), vx)
        l_ctx = jnp.sum(p_ctx, axis=-1, keepdims=True)
        o_parts = []
        for u in range(n_sub):
            p_loc = jnp.exp(s_locs[u] - ms[u])
            denom = jnp.sum(p_loc, axis=-1, keepdims=True) + l_ctx[u * tq:(u + 1) * tq]
            o = _dot(p_loc.astype(BF16), bands[u][1]) + o_ctx[u * tq:(u + 1) * tq]
            o_parts.append(o * (1.0 / denom))
        outs.append(jnp.concatenate(o_parts, axis=0))
    lane_o = lax.broadcasted_iota(jnp.int32, outs[0].shape, 1)
    o_ref[...] = jnp.where(lane_o < NA_HEAD_DIM, outs[0], outs[1]).astype(BF16)


def _na_bias_tables(rpb, rows):
    sub, band, w = NA_SUB_ROWS, NA_BAND_ROWS, GRID_W
    kh, kw = NA_KH, NA_KW
    n_heads = rpb.shape[0]
    cols = np.arange(w)
    c0 = np.clip(cols - kw // 2, 0, w - kw)
    cc = np.arange(w)[None, :]
    col_ok = (cc >= c0[:, None]) & (cc < c0[:, None] + kw)
    padded = jnp.pad(rpb, ((0, 0), (0, 0), (w, w)))
    shifted = jnp.stack([padded[:, :, w + kw - 1 - j:2 * w + kw - 1 - j] for j in range(w)], axis=2)
    t1 = jnp.where(col_ok[None, None], shifted, NEG_BIG)
    masked = jnp.full((n_heads, w, w), NEG_BIG, F32)
    variants = []
    for i0, b0 in ((0, 0), (sub, sub - kh // 2), (rows - sub, rows - band)):
        blocks = []
        for qi in range(i0, i0 + sub):
            r0 = min(max(qi - kh // 2, 0), rows - kh)
            row = [t1[:, r - qi + kh - 1] if r0 <= r < r0 + kh else masked for r in range(b0, b0 + band)]
            blocks.append(jnp.concatenate(row, axis=-1))
        variants.append(jnp.concatenate(blocks, axis=1))
    return jnp.stack(variants)


def _neighbourhood_attention(q, k, v, kx, vx, bias):
    bsz, n, _ = q.shape
    lc = kx.shape[1]
    rows = n // GRID_W
    sub, band = NA_SUB_ROWS, NA_BAND_ROWS
    n_sub = NA_ROWS_PER_STEP // sub
    nrb = rows // NA_ROWS_PER_STEP
    tq = NA_ROWS_PER_STEP * GRID_W
    pair = 2 * NA_HEAD_DIM
    last = rows // sub - 1

    def bias_spec(u):
        def index(b, hp, rb):
            sb = rb * n_sub + u
            return (jnp.where(sb == 0, 0, jnp.where(sb == last, 2, 1)), hp, 0, 0)
        return pl.BlockSpec((None, 2, sub * GRID_W, band * GRID_W), index)

    return pl.pallas_call(
        functools.partial(_na_body, rows=rows),
        out_shape=jax.ShapeDtypeStruct((bsz, n, NA_WIDTH), BF16),
        grid=(bsz, NA_HEADS // 2, nrb),
        in_specs=[
            pl.BlockSpec((None, tq, pair), lambda b, hp, rb: (b, rb, hp)),
            pl.BlockSpec((None, n, pair), lambda b, hp, rb: (b, 0, hp)),
            pl.BlockSpec((None, n, pair), lambda b, hp, rb: (b, 0, hp)),
            pl.BlockSpec((None, lc, pair), lambda b, hp, rb: (b, 0, hp)),
            pl.BlockSpec((None, lc, pair), lambda b, hp, rb: (b, 0, hp)),
            bias_spec(0), bias_spec(1),
        ],
        out_specs=pl.BlockSpec((None, tq, pair), lambda b, hp, rb: (b, rb, hp)),
        compiler_params=_params(3),
        name="neighbourhood_attention",
    )(q, k, v, kx, vx, bias, bias)


def _mlstm_body(*refs, with_output):
    if with_output:
        (qf_ref, kf_ref, vf_ref, gcf_ref, grf_ref, qr_ref, kr_ref, vr_ref, gcr_ref, grr_ref,
         c0_ref, m0_ref, hf_ref, hr_ref, c_scr, m_scr) = refs
    else:
        (kf_ref, vf_ref, gcf_ref, grf_ref, kr_ref, vr_ref, gcr_ref, grr_ref,
         c0_ref, m0_ref, c_out_ref, m_out_ref, c_scr, m_scr) = refs
    step = pl.program_id(1)
    n_steps = pl.num_programs(1)
    L, hd = ML_CHUNK, ML_HEAD_DIM

    @pl.when(step == 0)
    def _():
        c_scr[...] = c0_ref[...]
        m_scr[...] = m0_ref[...]

    t_idx = lax.broadcasted_iota(jnp.int32, (L, L), 0)
    s_idx = lax.broadcasted_iota(jnp.int32, (L, L), 1)
    ones_col = (lax.broadcasted_iota(jnp.int32, (L, hd), 1) == 0).astype(F32)
    hi = lax.Precision.HIGHEST
    for d in range(2):
        if d == 0:
            causal = s_idx <= t_idx
            k_ref, v_ref, gc_ref, gr_ref = kf_ref, vf_ref, gcf_ref, grf_ref
            q_ref = qf_ref if with_output else None
            h_ref = hf_ref if with_output else None
        else:
            causal = s_idx >= t_idx
            k_ref, v_ref, gc_ref, gr_ref = kr_ref, vr_ref, gcr_ref, grr_ref
            q_ref = qr_ref if with_output else None
            h_ref = hr_ref if with_output else None
        tri_c = causal.astype(F32)
        tri_r = jnp.logical_not(causal).astype(F32) + (s_idx == t_idx).astype(F32)
        gcol = gc_ref[...]
        grow = gr_ref[...]
        bcol = jnp.dot(tri_c, gcol, precision=hi, preferred_element_type=F32)
        brow = jnp.dot(grow, tri_r, precision=hi, preferred_element_type=F32)
        end = L - 1 if d == 0 else 0
        for hh in range(ML_HEADS):
            gi = hh * 4 + d * 2
            sl = slice(hh * hd, (hh + 1) * hd)
            i_row = grow[gi:gi + 1, :]
            b_row = brow[gi + 1:gi + 2, :]
            i_col = gcol[:, gi:gi + 1]
            b_col = bcol[:, gi + 1:gi + 2]
            b_end = b_row[:, end:end + 1]
            m_old = m_scr[d, hh][0:1, 0:1]
            w_end_row = b_end - b_row + i_row
            m_new = jnp.maximum(b_end + m_old, jnp.max(w_end_row, axis=-1, keepdims=True))
            a_prev = jnp.exp(b_end + m_old - m_new)
            a_tok = jnp.exp(b_end - b_col + i_col - m_new)
            k = k_ref[:, sl]
            v = v_ref[:, sl].astype(F32)
            c_old = c_scr[d, hh]
            if with_output:
                q = q_ref[:, sl]
                log_w = jnp.where(causal, b_col - b_row + i_row, NEG_BIG)
                log_inter = b_col + m_old
                m_t = jnp.maximum(log_inter, jnp.max(log_w, axis=-1, keepdims=True))
                w_intra = jnp.exp(log_w - m_t)
                w_inter = jnp.exp(log_inter - m_t)
                s = (_dot_nt(q, k) * w_intra).astype(BF16)
                v_aug = jnp.concatenate([v, ones_col], axis=1).astype(BF16)
                both = w_inter * _dot(q, c_old.astype(BF16)) + _dot(s, v_aug)
                den = both[:, hd:hd + 1]
                scale = 1.0 / jnp.maximum(jnp.abs(den), jnp.exp(-m_t))
                h_ref[:, sl] = both[:, :hd] * scale
            av_aug = jnp.concatenate([a_tok * v, a_tok * ones_col], axis=1).astype(BF16)
            c_scr[d, hh] = a_prev * c_old + _dot_tn(k, av_aug)
            m_scr[d, hh] = jnp.broadcast_to(m_new, (V7X_SUBLANES, V7X_LANES))

    if not with_output:
        @pl.when(step == n_steps - 1)
        def _():
            c_out_ref[...] = c_scr[...]
            m_out_ref[...] = m_scr[...]


def _mlstm_scan(q, k, v, gcol, grow, c0, m0, with_output):
    bsz, n, _ = k.shape
    L, hd = ML_CHUNK, ML_HEAD_DIM
    nch = n // L
    fwd = lambda width: pl.BlockSpec((None, L, width), lambda b, c: (b, c, 0))
    rev = lambda width: pl.BlockSpec((None, L, width), lambda b, c: (b, nch - 1 - c, 0))
    grow_f = pl.BlockSpec((None, N_GATES, L), lambda b, c: (b, 0, c))
    grow_r = pl.BlockSpec((None, N_GATES, L), lambda b, c: (b, 0, nch - 1 - c))
    c_spec = pl.BlockSpec((None, 2, ML_HEADS, hd, 2 * hd), lambda b, c: (b, 0, 0, 0, 0))
    m_spec = pl.BlockSpec((None, 2, ML_HEADS, V7X_SUBLANES, V7X_LANES), lambda b, c: (b, 0, 0, 0, 0))
    if with_output:
        args = [q, k, v, gcol, grow, q, k, v, gcol, grow, c0, m0]
        in_specs = [fwd(ML_WIDTH)] * 3 + [fwd(V7X_LANES), grow_f] + [rev(ML_WIDTH)] * 3 + [rev(V7X_LANES), grow_r]
        out_shape = [jax.ShapeDtypeStruct((bsz, n, ML_WIDTH), F32)] * 2
        out_specs = [fwd(ML_WIDTH), rev(ML_WIDTH)]
    else:
        args = [k, v, gcol, grow, k, v, gcol, grow, c0, m0]
        in_specs = [fwd(ML_WIDTH)] * 2 + [fwd(V7X_LANES), grow_f] + [rev(ML_WIDTH)] * 2 + [rev(V7X_LANES), grow_r]
        out_shape = [jax.ShapeDtypeStruct(c0.shape, F32), jax.ShapeDtypeStruct(m0.shape, F32)]
        out_specs = [c_spec, m_spec]
    return pl.pallas_call(
        functools.partial(_mlstm_body, with_output=with_output),
        out_shape=out_shape,
        grid=(bsz, nch),
        in_specs=in_specs + [c_spec, m_spec],
        out_specs=out_specs,
        scratch_shapes=[pltpu.VMEM((2, ML_HEADS, hd, 2 * hd), F32),
                        pltpu.VMEM((2, ML_HEADS, V7X_SUBLANES, V7X_LANES), F32)],
        compiler_params=_params(2),
        name="mlstm_scan_latent" if with_output else "mlstm_scan_context",
    )(*args)


def _evenout_body(x_ref, mod_ref, ya_ref, hf_ref, hr_ref, ob_ref, hg_ref, w_ref, o_ref, *, d):
    hd = ML_HEAD_DIM
    hb = hf_ref[...] + hr_ref[...]
    ob = ob_ref[...]
    hg = hg_ref[...]
    y = _dot(ya_ref[...], w_ref[0:NA_WIDTH, :])
    for hh in range(ML_HEADS):
        sl = slice(hh * hd, (hh + 1) * hd)
        hs = hb[:, sl]
        ms = jnp.mean(hs * hs, axis=-1, keepdims=True)
        yb = hs * lax.rsqrt(ms + EPS) * hg[:, sl] * jax.nn.sigmoid(ob[:, sl])
        y = y + _dot(yb.astype(BF16), w_ref[NA_WIDTH + hh * hd:NA_WIDTH + (hh + 1) * hd, :])
    o_ref[...] = x_ref[...] + mod_ref[...][:, 2 * d:] * y


def _even_output(x, mod3, ya, hf, hr, ob, head_g, w_out, tm=512):
    bsz, n, d = x.shape
    tok_spec = lambda width: pl.BlockSpec((None, tm, width), lambda b, i: (b, i, 0))
    return pl.pallas_call(
        functools.partial(_evenout_body, d=d),
        out_shape=jax.ShapeDtypeStruct(x.shape, F32),
        grid=(bsz, n // tm),
        in_specs=[
            tok_spec(d),
            pl.BlockSpec((None, 1, 3 * d), lambda b, i: (b, 0, 0)),
            tok_spec(NA_WIDTH), tok_spec(ML_WIDTH), tok_spec(ML_WIDTH), tok_spec(ML_WIDTH),
            pl.BlockSpec((1, ML_WIDTH), lambda b, i: (0, 0)),
            pl.BlockSpec(w_out.shape, lambda b, i: (0, 0)),
        ],
        out_specs=tok_spec(d),
        compiler_params=_params(2),
        name="even_output",
    )(x, mod3, ya, hf, hr, ob, head_g.reshape(1, ML_WIDTH), w_out)


def _sg_body(x_ref, mod_ref, g_ref, win_ref, lng_ref, lnb_ref, ws_ref, bs_ref, wout_ref, o_ref, v_scr,
             *, d, width, tm):
    x = x_ref[...]
    mod = mod_ref[...]
    h = _modulated(x, g_ref[...], mod, d).astype(BF16)
    gw = width // SG_GROUPS
    groups = [slice(g * gw, (g + 1) * gw) for g in range(SG_GROUPS)]
    total = None
    for cs in groups:
        vg = jax.nn.gelu(_dot(h, win_ref[:, width + cs.start:width + cs.stop]))
        v_scr[:, cs] = vg
        part = jnp.sum(vg, axis=-1, keepdims=True)
        total = part if total is None else total + part
    mu = total * (1.0 / width)
    sq = None
    for cs in groups:
        vc = v_scr[:, cs] - mu
        part = jnp.sum(vc * vc, axis=-1, keepdims=True)
        sq = part if sq is None else sq + part
    rstd = lax.rsqrt(sq * (1.0 / width) + EPS)
    bs = bs_ref[...]
    y = None
    for g, cs in enumerate(groups):
        vn = ((v_scr[:, cs] - mu) * rstd * lng_ref[:, cs] + lnb_ref[:, cs]).astype(BF16)
        u = jax.nn.gelu(_dot(h, win_ref[:, cs]))
        parts = []
        for c in range(tm // SG_CHUNK):
            rs = slice(c * SG_CHUNK, (c + 1) * SG_CHUNK)
            mixed = _dot(ws_ref[g], vn[rs]) + bs[:, g:g + 1]
            parts.append((u[rs] * mixed).astype(BF16))
        yg = _dot(jnp.concatenate(parts, axis=0), wout_ref[cs, :])
        y = yg if y is None else y + yg
    o_ref[...] = x + mod[:, 2 * d:] * y


def _spatial_gating(x, mod3, g, w_in, ln_g, ln_b, w_s, b_s, w_out, tm=512):
    bsz, n, d = x.shape
    width = w_out.shape[0]
    const = lambda shape: pl.BlockSpec(shape, lambda b, i: (0,) * len(shape))
    return pl.pallas_call(
        functools.partial(_sg_body, d=d, width=width, tm=tm),
        out_shape=jax.ShapeDtypeStruct(x.shape, F32),
        grid=(bsz, n // tm),
        in_specs=[
            pl.BlockSpec((None, tm, d), lambda b, i: (b, i, 0)),
            pl.BlockSpec((None, 1, 3 * d), lambda b, i: (b, 0, 0)),
            const((1, d)), const(w_in.shape), const((1, width)), const((1, width)),
            const(w_s.shape), const((SG_CHUNK, SG_GROUPS)), const(w_out.shape),
        ],
        out_specs=pl.BlockSpec((None, tm, d), lambda b, i: (b, i, 0)),
        scratch_shapes=[pltpu.VMEM((tm, width), F32)],
        compiler_params=_params(2),
        name="spatial_gating",
    )(x, mod3, g.reshape(1, d), w_in, ln_g.reshape(1, width), ln_b.reshape(1, width), w_s, b_s.T, w_out)


def _rope_tables(n):
    hd = ML_HEAD_DIM
    n_pairs = hd // 4
    inv_freq = ROPE_THETA ** (-jnp.arange(n_pairs, dtype=F32) / n_pairs)
    row_ang = jnp.arange(n // GRID_W, dtype=F32)[:, None] * inv_freq
    col_ang = jnp.arange(GRID_W, dtype=F32)[:, None] * inv_freq

    def table(ang, is_row):
        zero = jnp.zeros_like(ang)
        cos, sin = jnp.cos(ang), jnp.sin(ang)
        half = lambda t: jnp.concatenate([t, zero] if is_row else [zero, t], axis=-1)
        return jnp.stack([jnp.concatenate([half(cos), half(cos)], axis=-1),
                          jnp.concatenate([half(-sin), half(sin)], axis=-1)])

    return table(row_ang, True), table(col_ang, False)


def _even_weights(w_in, gate_b):
    d = w_in.shape[0]
    hd = ML_HEAD_DIM
    base = 3 * NA_WIDTH
    deint = lambda w: w.reshape(d, ML_HEADS, hd // 2, 2).transpose(0, 1, 3, 2).reshape(d, ML_WIDTH)
    qb = deint(w_in[:, base:base + ML_WIDTH])
    kb = deint(w_in[:, base + ML_WIDTH:base + 2 * ML_WIDTH])
    w_main = jnp.concatenate([w_in[:, :base], qb, kb, w_in[:, base + 2 * ML_WIDTH:base + 4 * ML_WIDTH]],
                             axis=1).astype(BF16)
    wg = w_in[:, base + 4 * ML_WIDTH:]
    w_gc = jnp.pad(wg, ((0, 0), (0, V7X_LANES - N_GATES))).astype(BF16)
    w_gr = wg.T.astype(BF16)
    gb = gate_b.reshape(N_GATES).astype(F32)
    gb_col = jnp.pad(gb, (0, V7X_LANES - N_GATES)).reshape(1, V7X_LANES)
    gb_row = gb.reshape(N_GATES, 1)
    return w_main, w_gc, w_gr, gb_col, gb_row


def kernel(x, c, ctx, c_ctx, w_mod, b_mod, norm_g, ffn_w_in, ffn_w_out, mix_w_in, na_rpb, ml_gate_b, ml_head_g,
           mix_w_out, sg_w_in, sg_ln_g, sg_ln_b, sg_w_s, sg_b_s, sg_w_out, final_g):
    bsz, n, d = x.shape
    depth = w_mod.shape[0]
    ctx_row = bsz
    cvec = jnp.zeros((V7X_SUBLANES, d), F32).at[:bsz].set(c).at[ctx_row].set(c_ctx)
    mod = _mod_vectors(cvec, w_mod, b_mod).reshape(depth, V7X_SUBLANES, 3, 1, 3 * d)
    batch_row = lambda b: b
    context_row = lambda b: ctx_row
    last_ctx_layer = ((depth - 1) // 2) * 2
    ffn_w_in = ffn_w_in.astype(BF16)
    ffn_w_out = ffn_w_out.astype(BF16)
    xc = ctx
    for l in range(depth):
        ctx_in = l <= last_ctx_layer
        ctx_out = l < last_ctx_layer
        x = _ffn(x, mod[l, :, 0], batch_row, norm_g[l, 0], ffn_w_in, ffn_w_out, l, 0)
        if ctx_in:
            xc = _ffn(xc, mod[l, :, 0], context_row, norm_g[l, 0], ffn_w_in, ffn_w_out, l, 0)
        if l % 2 == 0:
            e = l // 2
            w_main, w_gc, w_gr, gb_col, gb_row = _even_weights(mix_w_in[e], ml_gate_b[e])
            qa, ka, va, qb, kb, vb, ob, gcol, grow = _even_projection(
                x, mod[l, :, 1], batch_row, norm_g[l, 1], w_main, w_gc, w_gr, gb_col, gb_row, _rope_tables(n))
            _, kax, vax, _, kbx, vbx, _, gcolx, growx = _even_projection(
                xc, mod[l, :, 1], context_row, norm_g[l, 1], w_main, w_gc, w_gr, gb_col, gb_row, None)
            bias = _na_bias_tables(na_rpb[e], n // GRID_W)
            ya = _neighbourhood_attention(qa, ka, va, kax, vax, bias)
            c0 = jnp.zeros((bsz, 2, ML_HEADS, ML_HEAD_DIM, 2 * ML_HEAD_DIM), F32)
            m0 = jnp.zeros((bsz, 2, ML_HEADS, V7X_SUBLANES, V7X_LANES), F32)
            c1, m1 = _mlstm_scan(None, kbx, vbx, gcolx, growx, c0, m0, False)
            hf, hr = _mlstm_scan(qb, kb, vb, gcol, grow, c1, m1, True)
            x = _even_output(x, mod[l, :, 1], ya, hf, hr, ob, ml_head_g[e], mix_w_out[e].astype(BF16))
            assert not ctx_out, "context output path is not needed for this depth"
        else:
            o = l // 2
            x = _spatial_gating(x, mod[l, :, 1], norm_g[l, 1], sg_w_in[o].astype(BF16), sg_ln_g[o], sg_ln_b[o],
                                sg_w_s[o].astype(BF16), sg_b_s[o], sg_w_out[o].astype(BF16))
        fg = final_g if l == depth - 1 else None
        x = _ffn(x, mod[l, :, 2], batch_row, norm_g[l, 2], ffn_w_in, ffn_w_out, l, 1, final_g=fg)
    return x
```

```python
import functools

import numpy as np
import jax
import jax.numpy as jnp
from jax import lax
from jax.experimental import pallas as pl
from jax.experimental.pallas import tpu as pltpu

GRID_W = 64
NA_HEADS = 8
NA_HEAD_DIM = 64
NA_KH = 8
NA_KW = 16
ML_HEADS = 4
ML_HEAD_DIM = 128
ML_CHUNK = 128
ROPE_THETA = 10000.0
SG_CHUNK = 128
SG_GROUPS = 8
EPS = 1e-6
NA_WIDTH = NA_HEADS * NA_HEAD_DIM
ML_WIDTH = ML_HEADS * ML_HEAD_DIM
N_GATES = 4 * ML_HEADS
ML_AUG_ROWS = 16

V7X_LANES = 128
V7X_SUBLANES = 8
V7X_MXU_COLUMNS = 256
V7X_VMEM_LIMIT_BYTES = 56 * 1024 * 1024

NEG_BIG = -1e30
NA_ROWS_PER_STEP = 8
NA_SUB_ROWS = 4
NA_BAND_ROWS = 12

BF16 = jnp.bfloat16
F32 = jnp.float32


def _dot(a, b):
    return jnp.dot(a, b, preferred_element_type=F32)


def _dot_nt(a, b):
    return lax.dot_general(a, b, (((1,), (1,)), ((), ())), preferred_element_type=F32)


def _params(n_axes):
    return pltpu.CompilerParams(
        dimension_semantics=("arbitrary",) * n_axes,
        vmem_limit_bytes=V7X_VMEM_LIMIT_BYTES,
    )


def _modulated(x, g, mod, d):
    shift = mod[:, :d]
    scale = mod[:, d:2 * d]
    gs = g * (1.0 + scale)
    ms = jnp.mean(x * x, axis=-1, keepdims=True)
    return x * lax.rsqrt(ms + EPS) * gs + shift


def _mod_body(c_ref, w_ref, b_ref, o_ref):
    c = c_ref[...]
    s = (c * jax.nn.sigmoid(c)).astype(BF16)
    o_ref[...] = _dot(s, w_ref[...].astype(BF16)) + b_ref[...]


def _mod_vectors(cvec, w_mod, b_mod):
    depth, d, width = w_mod.shape
    tn = width // 8
    return pl.pallas_call(
        _mod_body,
        out_shape=jax.ShapeDtypeStruct((depth, V7X_SUBLANES, width), F32),
        grid=(depth, width // tn),
        in_specs=[
            pl.BlockSpec((V7X_SUBLANES, d), lambda l, j: (0, 0)),
            pl.BlockSpec((None, d, tn), lambda l, j: (l, 0, j)),
            pl.BlockSpec((None, 1, tn), lambda l, j: (l, 0, j)),
        ],
        out_specs=pl.BlockSpec((None, V7X_SUBLANES, tn), lambda l, j: (l, 0, j)),
        compiler_params=_params(2),
        name="mod_vectors",
    )(cvec, w_mod, b_mod.reshape(depth, 1, width))


def _ffn_body(x_ref, mod_ref, g_ref, win_ref, wout_ref, *rest, d, d_ff, chunk, final):
    if final:
        fg_ref, o_ref = rest
    else:
        (o_ref,) = rest
    x = x_ref[...]
    mod = mod_ref[...]
    h = _modulated(x, g_ref[...], mod, d).astype(BF16)
    acc = None
    for c in range(d_ff // chunk):
        a = _dot(h, win_ref[:, c * chunk:(c + 1) * chunk])
        b = _dot(h, win_ref[:, d_ff + c * chunk:d_ff + (c + 1) * chunk])
        t = (a * jax.nn.sigmoid(a) * b).astype(BF16)
        y = _dot(t, wout_ref[c * chunk:(c + 1) * chunk, :])
        acc = y if acc is None else acc + y
    out = x + (0.5 * mod[:, 2 * d:]) * acc
    if final:
        ms = jnp.mean(out * out, axis=-1, keepdims=True)
        out = out * lax.rsqrt(ms + EPS) * fg_ref[...]
    o_ref[...] = out


def _ffn(x, mod3, mod_row, g, w_in_all, w_out_all, layer, half, final_g=None, tm=512):
    bsz, n, d = x.shape
    d_ff = w_out_all.shape[2]
    tm = min(tm, n)
    final = final_g is not None
    in_specs = [
        pl.BlockSpec((None, tm, d), lambda b, i: (b, i, 0)),
        pl.BlockSpec((None, 1, 3 * d), lambda b, i: (mod_row(b), 0, 0)),
        pl.BlockSpec((1, d), lambda b, i: (0, 0)),
        pl.BlockSpec((None, None, d, 2 * d_ff), lambda b, i: (layer, half, 0, 0)),
        pl.BlockSpec((None, None, d_ff, d), lambda b, i: (layer, half, 0, 0)),
    ]
    args = [x, mod3, g.reshape(1, d), w_in_all, w_out_all]
    if final:
        in_specs.append(pl.BlockSpec((1, d), lambda b, i: (0, 0)))
        args.append(final_g.reshape(1, d))
    return pl.pallas_call(
        functools.partial(_ffn_body, d=d, d_ff=d_ff, chunk=V7X_MXU_COLUMNS, final=final),
        out_shape=jax.ShapeDtypeStruct(x.shape, F32),
        grid=(bsz, n // tm),
        in_specs=in_specs,
        out_specs=pl.BlockSpec((None, tm, d), lambda b, i: (b, i, 0)),
        compiler_params=_params(2),
        name="macaron_ffn",
    )(*args)


def _log_gates(z, is_forget):
    ls = jnp.minimum(z, 0.0) - jnp.log1p(jnp.exp(-jnp.abs(z)))
    return jnp.where(is_forget, ls, z)


def _split3(x):
    hi = x.astype(BF16)
    r1 = x - hi.astype(F32)
    mid = r1.astype(BF16)
    lo = (r1 - mid.astype(F32)).astype(BF16)
    return [hi, mid, lo]


def _evenproj_body(x_ref, mod_ref, g_ref, w_ref, wvt_ref, wg_ref, wgt_ref, gbc_ref, gbr_ref, rowt_ref, colt_ref,
                   qa_ref, ka_ref, va_ref, qb_ref, kb_ref, vt_ref, ob_ref, gc_ref, gr_ref, *, d, rope):
    h = _modulated(x_ref[...], g_ref[...], mod_ref[...], d).astype(BF16)
    nw, mw, hd = NA_WIDTH, ML_WIDTH, ML_HEAD_DIM
    qa_ref[...] = (_dot(h, w_ref[:, 0:nw]) * (NA_HEAD_DIM ** -0.5)).astype(BF16)
    ka_ref[...] = _dot(h, w_ref[:, nw:2 * nw]).astype(BF16)
    va_ref[...] = _dot(h, w_ref[:, 2 * nw:3 * nw]).astype(BF16)
    base = 3 * nw
    qb = _dot(h, w_ref[:, base:base + mw])
    kb = _dot(h, w_ref[:, base + mw:base + 2 * mw])
    if rope:
        tm = qb.shape[0]
        expand = lambda t: jnp.broadcast_to(t[:, None, :], (tm // GRID_W, GRID_W, hd)).reshape(tm, hd)
        cos = expand(rowt_ref[0]) + colt_ref[0]
        sin = expand(rowt_ref[1]) + colt_ref[1]
        for hh in range(ML_HEADS):
            sl = slice(hh * hd, (hh + 1) * hd)
            qh = qb[:, sl]
            kh = kb[:, sl]
            qb_ref[:, sl] = (qh * cos + pltpu.roll(qh, hd // 2, 1) * sin).astype(BF16)
            kb_ref[:, sl] = ((kh * cos + pltpu.roll(kh, hd // 2, 1) * sin) * (hd ** -0.5)).astype(BF16)
    else:
        qb_ref[...] = qb.astype(BF16)
        kb_ref[...] = (kb * (hd ** -0.5)).astype(BF16)
    vt_ref[...] = _dot_nt(wvt_ref[...], h).astype(BF16)
    ob_ref[...] = _dot(h, w_ref[:, base + 2 * mw:base + 3 * mw])
    L = ML_CHUNK
    tm = h.shape[0]
    r_idx = lax.broadcasted_iota(jnp.int32, (L, L), 0)
    c_idx = lax.broadcasted_iota(jnp.int32, (L, L), 1)
    lower = (c_idx <= r_idx).astype(BF16)
    upper = (r_idx <= c_idx).astype(BF16)

    zc = _dot(h, wg_ref[...]) + gbc_ref[...]
    lane = lax.broadcasted_iota(jnp.int32, zc.shape, 1)
    gates_c = _log_gates(zc, (lane % 2) == 1)
    split_c = jnp.concatenate(_split3(gates_c), axis=1)
    bwd_lane = ((lax.broadcasted_iota(jnp.int32, (L, V7X_LANES), 1) // 2) % 2) == 1
    for c in range(tm // L):
        rs = slice(c * L, (c + 1) * L)
        r = _dot(lower, split_c[rs])
        prefix = r[:, 0:V7X_LANES] + r[:, V7X_LANES:2 * V7X_LANES] + r[:, 2 * V7X_LANES:]
        g = gates_c[rs]
        b = jnp.where(bwd_lane, prefix[L - 1:L, :] - prefix + g, prefix)
        gc_ref[rs, :] = g - pltpu.roll(b, V7X_LANES - 1, 1)

    zr = _dot_nt(wgt_ref[...], h) + gbr_ref[...]
    row = lax.broadcasted_iota(jnp.int32, zr.shape, 0)
    gates_r = _log_gates(zr, (row % 2) == 1)
    split_r = jnp.concatenate(_split3(gates_r), axis=0)
    row_l = lax.broadcasted_iota(jnp.int32, (N_GATES, L), 0)
    bwd_row = ((row_l // 2) % 2) == 1
    f_row = (row_l % 2) == 1
    for c in range(tm // L):
        ls = slice(c * L, (c + 1) * L)
        r = _dot(split_r[:, ls], upper)
        prefix = r[0:N_GATES] + r[N_GATES:2 * N_GATES] + r[2 * N_GATES:]
        g = gates_r[:, ls]
        b = jnp.where(bwd_row, prefix[:, L - 1:L] - prefix + g, prefix)
        gr_ref[:, ls] = jnp.where(f_row, b, g - pltpu.roll(b, N_GATES - 1, 0))


def _even_projection(x, mod3, mod_row, g, weights, rope_tables, tm=512):
    w_main, w_vt, w_gc, w_gr, gb_col, gb_row = weights
    bsz, n, d = x.shape
    tm = min(tm, n)
    rope = rope_tables is not None
    if rope:
        row_t, col_t = rope_tables
        col_t = jnp.tile(col_t, (1, tm // GRID_W, 1))
    else:
        row_t = jnp.zeros((2, tm // GRID_W, ML_HEAD_DIM), F32)
        col_t = jnp.zeros((2, tm, ML_HEAD_DIM), F32)
    wm = w_main.shape[1]
    tok = lambda width, dt: jax.ShapeDtypeStruct((bsz, n, width), dt)
    tok_spec = lambda width: pl.BlockSpec((None, tm, width), lambda b, i: (b, i, 0))
    const = lambda shape: pl.BlockSpec(shape, lambda b, i: (0,) * len(shape))
    out_shape = [tok(NA_WIDTH, BF16)] * 3 + [tok(ML_WIDTH, BF16)] * 2 + [
        jax.ShapeDtypeStruct((bsz, ML_WIDTH, n), BF16), tok(ML_WIDTH, F32),
        tok(V7X_LANES, F32), jax.ShapeDtypeStruct((bsz, N_GATES, n), F32)]
    out_specs = [tok_spec(NA_WIDTH)] * 3 + [tok_spec(ML_WIDTH)] * 2 + [
        pl.BlockSpec((None, ML_WIDTH, tm), lambda b, i: (b, 0, i)), tok_spec(ML_WIDTH),
        tok_spec(V7X_LANES), pl.BlockSpec((None, N_GATES, tm), lambda b, i: (b, 0, i))]
    return pl.pallas_call(
        functools.partial(_evenproj_body, d=d, rope=rope),
        out_shape=out_shape,
        grid=(bsz, n // tm),
        in_specs=[
            tok_spec(d),
            pl.BlockSpec((None, 1, 3 * d), lambda b, i: (mod_row(b), 0, 0)),
            const((1, d)),
            const((d, wm)),
            const((ML_WIDTH, d)),
            const((d, V7X_LANES)),
            const((N_GATES, d)),
            const((1, V7X_LANES)),
            const((N_GATES, 1)),
            pl.BlockSpec((2, tm // GRID_W, ML_HEAD_DIM), lambda b, i: (0, i if rope else 0, 0)),
            pl.BlockSpec((2, tm, ML_HEAD_DIM), lambda b, i: (0, 0, 0)),
        ],
        out_specs=out_specs,
        compiler_params=_params(2),
        name="even_projection",
    )(x, mod3, g.reshape(1, d), w_main, w_vt, w_gc, w_gr, gb_col, gb_row, row_t, col_t)


def _na_body(q_ref, k_ref, v_ref, kx_ref, vx_ref, bias0_ref, bias1_ref, o_ref, *, rows):
    rb = pl.program_id(2)
    sub, band = NA_SUB_ROWS, NA_BAND_ROWS
    n_sub = NA_ROWS_PER_STEP // sub
    tq = sub * GRID_W
    kx = kx_ref[...]
    vx = vx_ref[...]
    q = q_ref[...]
    lane = lax.broadcasted_iota(jnp.int32, q.shape, 1)
    bias_refs = (bias0_ref, bias1_ref)
    bands = []
    for u in range(n_sub):
        band0 = jnp.clip((rb * n_sub + u) * sub - NA_KH // 2, 0, rows - band)
        start = pl.multiple_of(band0 * GRID_W, GRID_W)
        bands.append((k_ref[pl.ds(start, band * GRID_W), :], v_ref[pl.ds(start, band * GRID_W), :]))
    outs = []
    for hh in range(2):
        qm = jnp.where((lane // NA_HEAD_DIM) == hh, q, jnp.zeros_like(q))
        s_ctx = _dot_nt(qm, kx)
        m_ctx = jnp.max(s_ctx, axis=-1, keepdims=True)
        s_locs, ms = [], []
        for u in range(n_sub):
            s_loc = _dot_nt(qm[u * tq:(u + 1) * tq], bands[u][0]) + bias_refs[u][hh]
            s_locs.append(s_loc)
            ms.append(jnp.maximum(jnp.max(s_loc, axis=-1, keepdims=True), m_ctx[u * tq:(u + 1) * tq]))
        m = jnp.concatenate(ms, axis=0)
        p_ctx = jnp.exp(s_ctx - m)
        o_ctx = _dot(p_ctx.astype(BF16), vx)
        l_ctx = jnp.sum(p_ctx, axis=-1, keepdims=True)
        o_parts = []
        for u in range(n_sub):
            p_loc = jnp.exp(s_locs[u] - ms[u])
            denom = jnp.sum(p_loc, axis=-1, keepdims=True) + l_ctx[u * tq:(u + 1) * tq]
            o = _dot(p_loc.astype(BF16), bands[u][1]) + o_ctx[u * tq:(u + 1) * tq]
            o_parts.append(o * (1.0 / denom))
        outs.append(jnp.concatenate(o_parts, axis=0))
    lane_o = lax.broadcasted_iota(jnp.int32, outs[0].shape, 1)
    o_ref[...] = jnp.where(lane_o < NA_HEAD_DIM, outs[0], outs[1]).astype(BF16)


def _na_bias_tables(rpb, rows):
    sub, band, w = NA_SUB_ROWS, NA_BAND_ROWS, GRID_W
    kh, kw = NA_KH, NA_KW
    n_heads = rpb.shape[0]
    cols = np.arange(w)
    c0 = np.clip(cols - kw // 2, 0, w - kw)
    cc = np.arange(w)[None, :]
    col_ok = (cc >= c0[:, None]) & (cc < c0[:, None] + kw)
    padded = jnp.pad(rpb, ((0, 0), (0, 0), (w, w)))
    shifted = jnp.stack([padded[:, :, w + kw - 1 - j:2 * w + kw - 1 - j] for j in range(w)], axis=2)
    t1 = jnp.where(col_ok[None, None], shifted, NEG_BIG)
    masked = jnp.full((n_heads, w, w), NEG_BIG, F32)
    variants = []
    for i0, b0 in ((0, 0), (sub, sub - kh // 2), (rows - sub, rows - band)):
        blocks = []
        for qi in range(i0, i0 + sub):
            r0 = min(max(qi - kh // 2, 0), rows - kh)
            row = [t1[:, r - qi + kh - 1] if r0 <= r < r0 + kh else masked for r in range(b0, b0 + band)]
            blocks.append(jnp.concatenate(row, axis=-1))
        variants.append(jnp.concatenate(blocks, axis=1))
    return jnp.stack(variants)


def _neighbourhood_attention(q, k, v, kx, vx, bias):
    bsz, n, _ = q.shape
    lc = kx.shape[1]
    rows = n // GRID_W
    sub, band = NA_SUB_ROWS, NA_BAND_ROWS
    n_sub = NA_ROWS_PER_STEP // sub
    nrb = rows // NA_ROWS_PER_STEP
    tq = NA_ROWS_PER_STEP * GRID_W
    pair = 2 * NA_HEAD_DIM
    last = rows // sub - 1

    def bias_spec(u):
        def index(b, hp, rb):
            sb = rb * n_sub + u
            return (jnp.where(sb == 0, 0, jnp.where(sb == last, 2, 1)), hp, 0, 0)
        return pl.BlockSpec((None, 2, sub * GRID_W, band * GRID_W), index)

    return pl.pallas_call(
        functools.partial(_na_body, rows=rows),
        out_shape=jax.ShapeDtypeStruct((bsz, n, NA_WIDTH), BF16),
        grid=(bsz, NA_HEADS // 2, nrb),
        in_specs=[
            pl.BlockSpec((None, tq, pair), lambda b, hp, rb: (b, rb, hp)),
            pl.BlockSpec((None, n, pair), lambda b, hp, rb: (b, 0, hp)),
            pl.BlockSpec((None, n, pair), lambda b, hp, rb: (b, 0, hp)),
            pl.BlockSpec((None, lc, pair), lambda b, hp, rb: (b, 0, hp)),
            pl.BlockSpec((None, lc, pair), lambda b, hp, rb: (b, 0, hp)),
            bias_spec(0), bias_spec(1),
        ],
        out_specs=pl.BlockSpec((None, tq, pair), lambda b, hp, rb: (b, rb, hp)),
        compiler_params=_params(3),
        name="neighbourhood_attention",
    )(q, k, v, kx, vx, bias, bias)


def _mlstm_body(*refs, with_output):
    if with_output:
        (qf_ref, kf_ref, vtf_ref, gcf_ref, grf_ref, qr_ref, kr_ref, vtr_ref, gcr_ref, grr_ref,
         c0_ref, m0_ref, hf_ref, hr_ref, c_scr, m_scr) = refs
    else:
        (kf_ref, vtf_ref, grf_ref, kr_ref, vtr_ref, grr_ref,
         c0_ref, m0_ref, c_out_ref, m_out_ref, c_scr, m_scr) = refs
    step = pl.program_id(0)
    n_steps = pl.num_programs(0)
    L, hd = ML_CHUNK, ML_HEAD_DIM
    bsz = kf_ref.shape[0]

    @pl.when(step == 0)
    def _():
        c_scr[...] = c0_ref[...]
        m_scr[...] = m0_ref[...]

    s_idx = lax.broadcasted_iota(jnp.int32, (L, L), 0)
    t_idx = lax.broadcasted_iota(jnp.int32, (L, L), 1)
    ones_rows = (lax.broadcasted_iota(jnp.int32, (ML_AUG_ROWS, L), 0) == 0).astype(BF16)
    for bi, d in [(bi, d) for bi in range(bsz) for d in range(2)]:
        if d == 0:
            visible = s_idx <= t_idx
            k_ref, vt_ref, gr_ref = kf_ref.at[bi], vtf_ref.at[bi], grf_ref.at[bi]
            q_ref, gc_ref, h_ref = ((qf_ref.at[bi], gcf_ref.at[bi], hf_ref.at[bi]) if with_output
                                    else (None, None, None))
        else:
            visible = s_idx >= t_idx
            k_ref, vt_ref, gr_ref = kr_ref.at[bi], vtr_ref.at[bi], grr_ref.at[bi]
            q_ref, gc_ref, h_ref = ((qr_ref.at[bi], gcr_ref.at[bi], hr_ref.at[bi]) if with_output
                                    else (None, None, None))
        grow = gr_ref[...]
        end = L - 1 if d == 0 else 0
        for hh in range(ML_HEADS):
            gi = hh * 4 + d * 2
            sl = slice(hh * hd, (hh + 1) * hd)
            c_row = grow[gi:gi + 1, :]
            b_row = grow[gi + 1:gi + 2, :]
            b_end = b_row[:, end:end + 1]
            m_old = m_scr[bi, d, hh][0:1, 0:1]
            m_new = jnp.maximum(b_end + m_old, b_end + jnp.max(c_row, axis=-1, keepdims=True))
            a_prev = jnp.exp(b_end + m_old - m_new)
            a_tok = jnp.exp(b_end + c_row - m_new)
            k = k_ref[:, sl]
            vt_aug = jnp.concatenate([vt_ref[sl, :], ones_rows], axis=0)
            c_old = c_scr[bi, d, hh]
            if with_output:
                q = q_ref[:, sl]
                log_w = jnp.where(visible, b_row + gc_ref[:, gi:gi + 1], NEG_BIG)
                m_intra = jnp.max(log_w, axis=0, keepdims=True)
                p = (_dot_nt(k, q) * jnp.exp(log_w - m_intra)).astype(BF16)
                intra = _dot(vt_aug, p)
                inter = _dot_nt(c_old.astype(BF16), q)
                log_inter = b_row + m_old
                m_t = jnp.maximum(log_inter, m_intra)
                both = jnp.exp(log_inter - m_t) * inter + jnp.exp(m_intra - m_t) * intra
                den = both[hd:hd + 1, :]
                scale = 1.0 / jnp.maximum(jnp.abs(den), jnp.exp(-m_t))
                h_ref[:, sl] = (both[:hd, :] * scale).T
            av = (vt_aug.astype(F32) * a_tok).astype(BF16)
            c_scr[bi, d, hh] = a_prev * c_old + _dot(av, k)
            m_scr[bi, d, hh] = jnp.broadcast_to(m_new, (V7X_SUBLANES, V7X_LANES))

    if not with_output:
        @pl.when(step == n_steps - 1)
        def _():
            c_out_ref[...] = c_scr[...]
            m_out_ref[...] = m_scr[...]


def _mlstm_scan(q, k, vt, gcol, grow, c0, m0, with_output):
    bsz, n, _ = k.shape
    L, hd = ML_CHUNK, ML_HEAD_DIM
    nch = n // L
    fwd = lambda width: pl.BlockSpec((bsz, L, width), lambda c: (0, c, 0))
    rev = lambda width: pl.BlockSpec((bsz, L, width), lambda c: (0, nch - 1 - c, 0))
    fwd_t = lambda height: pl.BlockSpec((bsz, height, L), lambda c: (0, 0, c))
    rev_t = lambda height: pl.BlockSpec((bsz, height, L), lambda c: (0, 0, nch - 1 - c))
    c_spec = pl.BlockSpec((bsz, 2, ML_HEADS, hd + ML_AUG_ROWS, hd), lambda c: (0, 0, 0, 0, 0))
    m_spec = pl.BlockSpec((bsz, 2, ML_HEADS, V7X_SUBLANES, V7X_LANES), lambda c: (0, 0, 0, 0, 0))
    if with_output:
        args = [q, k, vt, gcol, grow, q, k, vt, gcol, grow, c0, m0]
        in_specs = ([fwd(ML_WIDTH)] * 2 + [fwd_t(ML_WIDTH), fwd(V7X_LANES), fwd_t(N_GATES)]
                    + [rev(ML_WIDTH)] * 2 + [rev_t(ML_WIDTH), rev(V7X_LANES), rev_t(N_GATES)])
        out_shape = [jax.ShapeDtypeStruct((bsz, n, ML_WIDTH), F32)] * 2
        out_specs = [fwd(ML_WIDTH), rev(ML_WIDTH)]
    else:
        args = [k, vt, grow, k, vt, grow, c0, m0]
        in_specs = ([fwd(ML_WIDTH), fwd_t(ML_WIDTH), fwd_t(N_GATES)]
                    + [rev(ML_WIDTH), rev_t(ML_WIDTH), rev_t(N_GATES)])
        out_shape = [jax.ShapeDtypeStruct(c0.shape, F32), jax.ShapeDtypeStruct(m0.shape, F32)]
        out_specs = [c_spec, m_spec]
    return pl.pallas_call(
        functools.partial(_mlstm_body, with_output=with_output),
        out_shape=out_shape,
        grid=(nch,),
        in_specs=in_specs + [c_spec, m_spec],
        out_specs=out_specs,
        scratch_shapes=[pltpu.VMEM((bsz, 2, ML_HEADS, hd + ML_AUG_ROWS, hd), F32),
                        pltpu.VMEM((bsz, 2, ML_HEADS, V7X_SUBLANES, V7X_LANES), F32)],
        compiler_params=_params(1),
        name="mlstm_scan_latent" if with_output else "mlstm_scan_context",
    )(*args)


def _evenout_body(x_ref, mod_ref, ya_ref, hf_ref, hr_ref, ob_ref, hg_ref, w_ref, o_ref, *, d):
    hd = ML_HEAD_DIM
    hb = hf_ref[...] + hr_ref[...]
    ob = ob_ref[...]
    hg = hg_ref[...]
    y = _dot(ya_ref[...], w_ref[0:NA_WIDTH, :])
    for hh in range(ML_HEADS):
        sl = slice(hh * hd, (hh + 1) * hd)
        hs = hb[:, sl]
        ms = jnp.mean(hs * hs, axis=-1, keepdims=True)
        yb = hs * lax.rsqrt(ms + EPS) * hg[:, sl] * jax.nn.sigmoid(ob[:, sl])
        y = y + _dot(yb.astype(BF16), w_ref[NA_WIDTH + hh * hd:NA_WIDTH + (hh + 1) * hd, :])
    o_ref[...] = x_ref[...] + mod_ref[...][:, 2 * d:] * y


def _even_output(x, mod3, ya, hf, hr, ob, head_g, w_out, tm=512):
    bsz, n, d = x.shape
    tok_spec = lambda width: pl.BlockSpec((None, tm, width), lambda b, i: (b, i, 0))
    return pl.pallas_call(
        functools.partial(_evenout_body, d=d),
        out_shape=jax.ShapeDtypeStruct(x.shape, F32),
        grid=(bsz, n // tm),
        in_specs=[
            tok_spec(d),
            pl.BlockSpec((None, 1, 3 * d), lambda b, i: (b, 0, 0)),
            tok_spec(NA_WIDTH), tok_spec(ML_WIDTH), tok_spec(ML_WIDTH), tok_spec(ML_WIDTH),
            pl.BlockSpec((1, ML_WIDTH), lambda b, i: (0, 0)),
            pl.BlockSpec(w_out.shape, lambda b, i: (0, 0)),
        ],
        out_specs=tok_spec(d),
        compiler_params=_params(2),
        name="even_output",
    )(x, mod3, ya, hf, hr, ob, head_g.reshape(1, ML_WIDTH), w_out)


def _sg_body(x_ref, mod_ref, g_ref, win_ref, lng_ref, lnb_ref, ws_ref, bs_ref, wout_ref, o_ref, v_scr,
             *, d, width, tm):
    x = x_ref[...]
    mod = mod_ref[...]
    h = _modulated(x, g_ref[...], mod, d).astype(BF16)
    gw = width // SG_GROUPS
    groups = [slice(g * gw, (g + 1) * gw) for g in range(SG_GROUPS)]
    total = None
    for cs in groups:
        vg = jax.nn.gelu(_dot(h, win_ref[:, width + cs.start:width + cs.stop]))
        v_scr[:, cs] = vg
        part = jnp.sum(vg, axis=-1, keepdims=True)
        total = part if total is None else total + part
    mu = total * (1.0 / width)
    sq = None
    for cs in groups:
        vc = v_scr[:, cs] - mu
        part = jnp.sum(vc * vc, axis=-1, keepdims=True)
        sq = part if sq is None else sq + part
    rstd = lax.rsqrt(sq * (1.0 / width) + EPS)
    bs = bs_ref[...]
    y = None
    for g, cs in enumerate(groups):
        vn = ((v_scr[:, cs] - mu) * rstd * lng_ref[:, cs] + lnb_ref[:, cs]).astype(BF16)
        u = jax.nn.gelu(_dot(h, win_ref[:, cs]))
        parts = []
        for c in range(tm // SG_CHUNK):
            rs = slice(c * SG_CHUNK, (c + 1) * SG_CHUNK)
            mixed = _dot(ws_ref[g], vn[rs]) + bs[:, g:g + 1]
            parts.append((u[rs] * mixed).astype(BF16))
        yg = _dot(jnp.concatenate(parts, axis=0), wout_ref[cs, :])
        y = yg if y is None else y + yg
    o_ref[...] = x + mod[:, 2 * d:] * y


def _spatial_gating(x, mod3, g, w_in, ln_g, ln_b, w_s, b_s, w_out, tm=512):
    bsz, n, d = x.shape
    width = w_out.shape[0]
    const = lambda shape: pl.BlockSpec(shape, lambda b, i: (0,) * len(shape))
    return pl.pallas_call(
        functools.partial(_sg_body, d=d, width=width, tm=tm),
        out_shape=jax.ShapeDtypeStruct(x.shape, F32),
        grid=(bsz, n // tm),
        in_specs=[
            pl.BlockSpec((None, tm, d), lambda b, i: (b, i, 0)),
            pl.BlockSpec((None, 1, 3 * d), lambda b, i: (b, 0, 0)),
            const((1, d)), const(w_in.shape), const((1, width)), const((1, width)),
            const(w_s.shape), const((SG_CHUNK, SG_GROUPS)), const(w_out.shape),
        ],
        out_specs=pl.BlockSpec((None, tm, d), lambda b, i: (b, i, 0)),
        scratch_shapes=[pltpu.VMEM((tm, width), F32)],
        compiler_params=_params(2),
        name="spatial_gating",
    )(x, mod3, g.reshape(1, d), w_in, ln_g.reshape(1, width), ln_b.reshape(1, width), w_s, b_s.T, w_out)


def _rope_tables(n):
    hd = ML_HEAD_DIM
    n_pairs = hd // 4
    inv_freq = ROPE_THETA ** (-jnp.arange(n_pairs, dtype=F32) / n_pairs)
    row_ang = jnp.arange(n // GRID_W, dtype=F32)[:, None] * inv_freq
    col_ang = jnp.arange(GRID_W, dtype=F32)[:, None] * inv_freq

    def table(ang, is_row):
        zero = jnp.zeros_like(ang)
        cos, sin = jnp.cos(ang), jnp.sin(ang)
        half = lambda t: jnp.concatenate([t, zero] if is_row else [zero, t], axis=-1)
        return jnp.stack([jnp.concatenate([half(cos), half(cos)], axis=-1),
                          jnp.concatenate([half(-sin), half(sin)], axis=-1)])

    return table(row_ang, True), table(col_ang, False)


def _even_weights(w_in, gate_b):
    d = w_in.shape[0]
    hd = ML_HEAD_DIM
    base = 3 * NA_WIDTH
    deint = lambda w: w.reshape(d, ML_HEADS, hd // 2, 2).transpose(0, 1, 3, 2).reshape(d, ML_WIDTH)
    qb = deint(w_in[:, base:base + ML_WIDTH])
    kb = deint(w_in[:, base + ML_WIDTH:base + 2 * ML_WIDTH])
    w_main = jnp.concatenate([w_in[:, :base], qb, kb, w_in[:, base + 3 * ML_WIDTH:base + 4 * ML_WIDTH]],
                             axis=1).astype(BF16)
    w_vt = w_in[:, base + 2 * ML_WIDTH:base + 3 * ML_WIDTH].T.astype(BF16)
    wg = w_in[:, base + 4 * ML_WIDTH:]
    w_gc = jnp.pad(wg, ((0, 0), (0, V7X_LANES - N_GATES))).astype(BF16)
    w_gr = wg.T.astype(BF16)
    gb = gate_b.reshape(N_GATES).astype(F32)
    gb_col = jnp.pad(gb, (0, V7X_LANES - N_GATES)).reshape(1, V7X_LANES)
    gb_row = gb.reshape(N_GATES, 1)
    return w_main, w_vt, w_gc, w_gr, gb_col, gb_row


def kernel(x, c, ctx, c_ctx, w_mod, b_mod, norm_g, ffn_w_in, ffn_w_out, mix_w_in, na_rpb, ml_gate_b, ml_head_g,
           mix_w_out, sg_w_in, sg_ln_g, sg_ln_b, sg_w_s, sg_b_s, sg_w_out, final_g):
    bsz, n, d = x.shape
    depth = w_mod.shape[0]
    ctx_row = bsz
    cvec = jnp.zeros((V7X_SUBLANES, d), F32).at[:bsz].set(c).at[ctx_row].set(c_ctx)
    mod = _mod_vectors(cvec, w_mod, b_mod).reshape(depth, V7X_SUBLANES, 3, 1, 3 * d)
    batch_row = lambda b: b
    context_row = lambda b: ctx_row
    last_ctx_layer = ((depth - 1) // 2) * 2
    ffn_w_in = ffn_w_in.astype(BF16)
    ffn_w_out = ffn_w_out.astype(BF16)
    xc = ctx
    for l in range(depth):
        ctx_in = l <= last_ctx_layer
        ctx_out = l < last_ctx_layer
        x = _ffn(x, mod[l, :, 0], batch_row, norm_g[l, 0], ffn_w_in, ffn_w_out, l, 0)
        if ctx_in:
            xc = _ffn(xc, mod[l, :, 0], context_row, norm_g[l, 0], ffn_w_in, ffn_w_out, l, 0)
        if l % 2 == 0:
            e = l // 2
            weights = _even_weights(mix_w_in[e], ml_gate_b[e])
            qa, ka, va, qb, kb, vt, ob, gcol, grow = _even_projection(
                x, mod[l, :, 1], batch_row, norm_g[l, 1], weights, _rope_tables(n))
            _, kax, vax, _, kbx, vtx, _, _, growx = _even_projection(
                xc, mod[l, :, 1], context_row, norm_g[l, 1], weights, None)
            bias = _na_bias_tables(na_rpb[e], n // GRID_W)
            ya = _neighbourhood_attention(qa, ka, va, kax, vax, bias)
            c0 = jnp.zeros((bsz, 2, ML_HEADS, ML_HEAD_DIM + ML_AUG_ROWS, ML_HEAD_DIM), F32)
            m0 = jnp.zeros((bsz, 2, ML_HEADS, V7X_SUBLANES, V7X_LANES), F32)
            c1, m1 = _mlstm_scan(None, kbx, vtx, None, growx, c0, m0, False)
            hf, hr = _mlstm_scan(qb, kb, vt, gcol, grow, c1, m1, True)
            x = _even_output(x, mod[l, :, 1], ya, hf, hr, ob, ml_head_g[e], mix_w_out[e].astype(BF16))
            assert not ctx_out, "context output path is not needed for this depth"
        else:
            o = l // 2
            x = _spatial_gating(x, mod[l, :, 1], norm_g[l, 1], sg_w_in[o].astype(BF16), sg_ln_g[o], sg_ln_b[o],
                                sg_w_s[o].astype(BF16), sg_b_s[o], sg_w_out[o].astype(BF16))
        fg = final_g if l == depth - 1 else None
        x = _ffn(x, mod[l, :, 2], batch_row, norm_g[l, 2], ffn_w_in, ffn_w_out, l, 1, final_g=fg)
    return x
```

```python
import functools

import numpy as np
import jax
import jax.numpy as jnp
from jax import lax
from jax.experimental import pallas as pl
from jax.experimental.pallas import tpu as pltpu

GRID_W = 64
NA_HEADS = 8
NA_HEAD_DIM = 64
NA_KH = 8
NA_KW = 16
ML_HEADS = 4
ML_HEAD_DIM = 128
ML_CHUNK = 128
ROPE_THETA = 10000.0
SG_CHUNK = 128
SG_GROUPS = 8
EPS = 1e-6
NA_WIDTH = NA_HEADS * NA_HEAD_DIM
ML_WIDTH = ML_HEADS * ML_HEAD_DIM
N_GATES = 4 * ML_HEADS
ML_AUG_ROWS = 16

V7X_LANES = 128
V7X_SUBLANES = 8
V7X_MXU_COLUMNS = 256
V7X_VMEM_LIMIT_BYTES = 56 * 1024 * 1024

NEG_BIG = -1e30
NA_ROWS_PER_STEP = 8
NA_SUB_ROWS = 4
NA_BAND_ROWS = 12

BF16 = jnp.bfloat16
F32 = jnp.float32


def _dot(a, b):
    return jnp.dot(a, b, preferred_element_type=F32)


def _dot_nt(a, b):
    return lax.dot_general(a, b, (((1,), (1,)), ((), ())), preferred_element_type=F32)


def _params(n_axes):
    return pltpu.CompilerParams(
        dimension_semantics=("arbitrary",) * n_axes,
        vmem_limit_bytes=V7X_VMEM_LIMIT_BYTES,
    )


def _modulated(x, g, mod, d):
    shift = mod[:, :d]
    scale = mod[:, d:2 * d]
    gs = g * (1.0 + scale)
    ms = jnp.mean(x * x, axis=-1, keepdims=True)
    return x * lax.rsqrt(ms + EPS) * gs + shift


def _mod_body(c_ref, w_ref, b_ref, o_ref):
    c = c_ref[...]
    s = (c * jax.nn.sigmoid(c)).astype(BF16)
    o_ref[...] = _dot(s, w_ref[...].astype(BF16)) + b_ref[...]


def _mod_vectors(cvec, w_mod, b_mod):
    depth, d, width = w_mod.shape
    tn = width // 8
    return pl.pallas_call(
        _mod_body,
        out_shape=jax.ShapeDtypeStruct((depth, V7X_SUBLANES, width), F32),
        grid=(depth, width // tn),
        in_specs=[
            pl.BlockSpec((V7X_SUBLANES, d), lambda l, j: (0, 0)),
            pl.BlockSpec((None, d, tn), lambda l, j: (l, 0, j)),
            pl.BlockSpec((None, 1, tn), lambda l, j: (l, 0, j)),
        ],
        out_specs=pl.BlockSpec((None, V7X_SUBLANES, tn), lambda l, j: (l, 0, j)),
        compiler_params=_params(2),
        name="mod_vectors",
    )(cvec, w_mod, b_mod.reshape(depth, 1, width))


def _mixer_output(ya_ref, hf_ref, hr_ref, ob_ref, hg_ref, w_ref):
    hd = ML_HEAD_DIM
    hb = hf_ref[...] + hr_ref[...]
    ob = ob_ref[...]
    hg = hg_ref[...]
    y = _dot(ya_ref[...], w_ref[0:NA_WIDTH, :])
    for hh in range(ML_HEADS):
        sl = slice(hh * hd, (hh + 1) * hd)
        hs = hb[:, sl]
        ms = jnp.mean(hs * hs, axis=-1, keepdims=True)
        yb = hs * lax.rsqrt(ms + EPS) * hg[:, sl] * jax.nn.sigmoid(ob[:, sl])
        y = y + _dot(yb.astype(BF16), w_ref[NA_WIDTH + hh * hd:NA_WIDTH + (hh + 1) * hd, :])
    return y


def _ffn_body(*refs, d, d_ff, chunk, final, mixer):
    refs = list(refs)
    o_ref = refs.pop()
    x_ref, mod_ref, g_ref, win_ref, wout_ref = refs[:5]
    rest = refs[5:]
    x = x_ref[...]
    if mixer:
        mixmod_ref = rest[0]
        x = x + mixmod_ref[...][:, 2 * d:] * _mixer_output(*rest[1:7])
        rest = rest[7:]
    if final:
        (fg_ref,) = rest
    mod = mod_ref[...]
    h = _modulated(x, g_ref[...], mod, d).astype(BF16)
    acc = None
    for c in range(d_ff // chunk):
        a = _dot(h, win_ref[:, c * chunk:(c + 1) * chunk])
        b = _dot(h, win_ref[:, d_ff + c * chunk:d_ff + (c + 1) * chunk])
        t = (a * jax.nn.sigmoid(a) * b).astype(BF16)
        y = _dot(t, wout_ref[c * chunk:(c + 1) * chunk, :])
        acc = y if acc is None else acc + y
    out = x + (0.5 * mod[:, 2 * d:]) * acc
    if final:
        ms = jnp.mean(out * out, axis=-1, keepdims=True)
        out = out * lax.rsqrt(ms + EPS) * fg_ref[...]
    o_ref[...] = out


def _ffn(x, mod3, mod_row, g, w_in_all, w_out_all, layer, half, final_g=None, mixer=None, tm=512):
    bsz, n, d = x.shape
    d_ff = w_out_all.shape[2]
    tm = min(tm, n)
    final = final_g is not None
    tok_spec = lambda width: pl.BlockSpec((None, tm, width), lambda b, i: (b, i, 0))
    resident = dict(pipeline_mode=pl.Buffered(1))
    in_specs = [
        tok_spec(d),
        pl.BlockSpec((None, 1, 3 * d), lambda b, i: (mod_row(b), 0, 0)),
        pl.BlockSpec((1, d), lambda b, i: (0, 0)),
        pl.BlockSpec((None, None, d, 2 * d_ff), lambda b, i: (layer, half, 0, 0), **resident),
        pl.BlockSpec((None, None, d_ff, d), lambda b, i: (layer, half, 0, 0), **resident),
    ]
    args = [x, mod3, g.reshape(1, d), w_in_all, w_out_all]
    if mixer is not None:
        mix_mod3, ya, hf, hr, ob, head_g, w_mix = mixer
        in_specs += [
            pl.BlockSpec((None, 1, 3 * d), lambda b, i: (mod_row(b), 0, 0)),
            tok_spec(NA_WIDTH), tok_spec(ML_WIDTH), tok_spec(ML_WIDTH), tok_spec(ML_WIDTH),
            pl.BlockSpec((1, ML_WIDTH), lambda b, i: (0, 0)),
            pl.BlockSpec(w_mix.shape, lambda b, i: (0, 0), **resident),
        ]
        args += [mix_mod3, ya, hf, hr, ob, head_g.reshape(1, ML_WIDTH), w_mix]
    if final:
        in_specs.append(pl.BlockSpec((1, d), lambda b, i: (0, 0)))
        args.append(final_g.reshape(1, d))
    return pl.pallas_call(
        functools.partial(_ffn_body, d=d, d_ff=d_ff, chunk=V7X_MXU_COLUMNS, final=final,
                          mixer=mixer is not None),
        out_shape=jax.ShapeDtypeStruct(x.shape, F32),
        grid=(bsz, n // tm),
        in_specs=in_specs,
        out_specs=pl.BlockSpec((None, tm, d), lambda b, i: (b, i, 0)),
        compiler_params=_params(2),
        name="macaron_ffn",
    )(*args)


def _log_gates(z, is_forget):
    ls = jnp.minimum(z, 0.0) - jnp.log1p(jnp.exp(-jnp.abs(z)))
    return jnp.where(is_forget, ls, z)


def _split3(x):
    hi = x.astype(BF16)
    r1 = x - hi.astype(F32)
    mid = r1.astype(BF16)
    lo = (r1 - mid.astype(F32)).astype(BF16)
    return [hi, mid, lo]


def _evenproj_body(x_ref, mod_ref, g_ref, w_ref, wvt_ref, wg_ref, wgt_ref, gbc_ref, gbr_ref, rowt_ref, colt_ref,
                   qa_ref, ka_ref, va_ref, qb_ref, kb_ref, vt_ref, ob_ref, gc_ref, gr_ref, *, d, rope):
    h = _modulated(x_ref[...], g_ref[...], mod_ref[...], d).astype(BF16)
    gates = _gate_stages(h, wg_ref, wgt_ref, gbc_ref, gbr_ref, gc_ref, gr_ref)
    nw, mw, hd = NA_WIDTH, ML_WIDTH, ML_HEAD_DIM
    next(gates, None)
    qa_ref[...] = (_dot(h, w_ref[:, 0:nw]) * (NA_HEAD_DIM ** -0.5)).astype(BF16)
    next(gates, None)
    ka_ref[...] = _dot(h, w_ref[:, nw:2 * nw]).astype(BF16)
    next(gates, None)
    va_ref[...] = _dot(h, w_ref[:, 2 * nw:3 * nw]).astype(BF16)
    next(gates, None)
    base = 3 * nw
    qb = _dot(h, w_ref[:, base:base + mw])
    next(gates, None)
    kb = _dot(h, w_ref[:, base + mw:base + 2 * mw])
    next(gates, None)
    if rope:
        tm = qb.shape[0]
        expand = lambda t: jnp.broadcast_to(t[:, None, :], (tm // GRID_W, GRID_W, hd)).reshape(tm, hd)
        cos = expand(rowt_ref[0]) + colt_ref[0]
        sin = expand(rowt_ref[1]) + colt_ref[1]
        for hh in range(ML_HEADS):
            sl = slice(hh * hd, (hh + 1) * hd)
            qh = qb[:, sl]
            kh = kb[:, sl]
            qb_ref[:, sl] = (qh * cos + pltpu.roll(qh, hd // 2, 1) * sin).astype(BF16)
            kb_ref[:, sl] = ((kh * cos + pltpu.roll(kh, hd // 2, 1) * sin) * (hd ** -0.5)).astype(BF16)
    else:
        qb_ref[...] = qb.astype(BF16)
        kb_ref[...] = (kb * (hd ** -0.5)).astype(BF16)
    vt_ref[...] = _dot_nt(wvt_ref[...], h).astype(BF16)
    next(gates, None)
    ob_ref[...] = _dot(h, w_ref[:, base + 2 * mw:base + 3 * mw])
    for _ in gates:
        pass


def _gate_stages(h, wg_ref, wgt_ref, gbc_ref, gbr_ref, gc_ref, gr_ref):
    L = ML_CHUNK
    tm = h.shape[0]
    r_idx = lax.broadcasted_iota(jnp.int32, (L, L), 0)
    c_idx = lax.broadcasted_iota(jnp.int32, (L, L), 1)
    lower = (c_idx <= r_idx).astype(BF16)
    upper = (r_idx <= c_idx).astype(BF16)
    zc = _dot(h, wg_ref[...]) + gbc_ref[...]
    zr = _dot_nt(wgt_ref[...], h) + gbr_ref[...]
    yield
    lane = lax.broadcasted_iota(jnp.int32, zc.shape, 1)
    gates_c = _log_gates(zc, (lane % 2) == 1)
    split_c = jnp.concatenate(_split3(gates_c), axis=1)
    row = lax.broadcasted_iota(jnp.int32, zr.shape, 0)
    gates_r = _log_gates(zr, (row % 2) == 1)
    split_r = jnp.concatenate(_split3(gates_r), axis=0)
    yield
    bwd_lane = ((lax.broadcasted_iota(jnp.int32, (L, V7X_LANES), 1) // 2) % 2) == 1
    for c in range(tm // L):
        rs = slice(c * L, (c + 1) * L)
        r = _dot(lower, split_c[rs])
        prefix = r[:, 0:V7X_LANES] + r[:, V7X_LANES:2 * V7X_LANES] + r[:, 2 * V7X_LANES:]
        g = gates_c[rs]
        b = jnp.where(bwd_lane, prefix[L - 1:L, :] - prefix + g, prefix)
        gc_ref[rs, :] = g - pltpu.roll(b, V7X_LANES - 1, 1)
        if c % 2 == 1:
            yield
    row_l = lax.broadcasted_iota(jnp.int32, (N_GATES, L), 0)
    bwd_row = ((row_l // 2) % 2) == 1
    f_row = (row_l % 2) == 1
    for c in range(tm // L):
        ls = slice(c * L, (c + 1) * L)
        r = _dot(split_r[:, ls], upper)
        prefix = r[0:N_GATES] + r[N_GATES:2 * N_GATES] + r[2 * N_GATES:]
        g = gates_r[:, ls]
        b = jnp.where(bwd_row, prefix[:, L - 1:L] - prefix + g, prefix)
        gr_ref[:, ls] = jnp.where(f_row, b, g - pltpu.roll(b, N_GATES - 1, 0))
        if c % 2 == 1:
            yield


def _even_projection(x, mod3, mod_row, g, weights, rope_tables, tm=512):
    w_main, w_vt, w_gc, w_gr, gb_col, gb_row = weights
    bsz, n, d = x.shape
    tm = min(tm, n)
    rope = rope_tables is not None
    if rope:
        row_t, col_t = rope_tables
        col_t = jnp.tile(col_t, (1, tm // GRID_W, 1))
    else:
        row_t = jnp.zeros((2, tm // GRID_W, ML_HEAD_DIM), F32)
        col_t = jnp.zeros((2, tm, ML_HEAD_DIM), F32)
    wm = w_main.shape[1]
    tok = lambda width, dt: jax.ShapeDtypeStruct((bsz, n, width), dt)
    tok_spec = lambda width: pl.BlockSpec((None, tm, width), lambda b, i: (b, i, 0))
    const = lambda shape: pl.BlockSpec(shape, lambda b, i: (0,) * len(shape))
    out_shape = [tok(NA_WIDTH, BF16)] * 3 + [tok(ML_WIDTH, BF16)] * 2 + [
        jax.ShapeDtypeStruct((bsz, ML_WIDTH, n), BF16), tok(ML_WIDTH, F32),
        tok(V7X_LANES, F32), jax.ShapeDtypeStruct((bsz, N_GATES, n), F32)]
    out_specs = [tok_spec(NA_WIDTH)] * 3 + [tok_spec(ML_WIDTH)] * 2 + [
        pl.BlockSpec((None, ML_WIDTH, tm), lambda b, i: (b, 0, i)), tok_spec(ML_WIDTH),
        tok_spec(V7X_LANES), pl.BlockSpec((None, N_GATES, tm), lambda b, i: (b, 0, i))]
    return pl.pallas_call(
        functools.partial(_evenproj_body, d=d, rope=rope),
        out_shape=out_shape,
        grid=(bsz, n // tm),
        in_specs=[
            tok_spec(d),
            pl.BlockSpec((None, 1, 3 * d), lambda b, i: (mod_row(b), 0, 0)),
            const((1, d)),
            const((d, wm)),
            const((ML_WIDTH, d)),
            const((d, V7X_LANES)),
            const((N_GATES, d)),
            const((1, V7X_LANES)),
            const((N_GATES, 1)),
            pl.BlockSpec((2, tm // GRID_W, ML_HEAD_DIM), lambda b, i: (0, i if rope else 0, 0)),
            pl.BlockSpec((2, tm, ML_HEAD_DIM), lambda b, i: (0, 0, 0)),
        ],
        out_specs=out_specs,
        compiler_params=_params(2),
        name="even_projection",
    )(x, mod3, g.reshape(1, d), w_main, w_vt, w_gc, w_gr, gb_col, gb_row, row_t, col_t)


def _na_body(q_ref, k_ref, v_ref, kx_ref, vx_ref, bias0_ref, bias1_ref, o_ref, *, rows):
    rb = pl.program_id(2)
    sub, band = NA_SUB_ROWS, NA_BAND_ROWS
    n_sub = NA_ROWS_PER_STEP // sub
    tq = sub * GRID_W
    kx = kx_ref[...]
    vx = vx_ref[...]
    q = q_ref[...]
    lane = lax.broadcasted_iota(jnp.int32, q.shape, 1)
    bias_refs = (bias0_ref, bias1_ref)
    bands = []
    for u in range(n_sub):
        band0 = jnp.clip((rb * n_sub + u) * sub - NA_KH // 2, 0, rows - band)
        start = pl.multiple_of(band0 * GRID_W, GRID_W)
        bands.append((k_ref[pl.ds(start, band * GRID_W), :], v_ref[pl.ds(start, band * GRID_W), :]))
    qms = [jnp.where((lane // NA_HEAD_DIM) == hh, q, jnp.zeros_like(q)) for hh in range(2)]
    items = [(hh, u) for hh in range(2) for u in range(n_sub)]

    def scores(item):
        hh, u = item
        qm = qms[hh][u * tq:(u + 1) * tq]
        return _dot_nt(qm, bands[u][0]) + bias_refs[u][hh], _dot_nt(qm, kx)

    def attend(item, s_loc, s_ctx):
        u = item[1]
        m = jnp.maximum(jnp.max(s_loc, axis=-1, keepdims=True), jnp.max(s_ctx, axis=-1, keepdims=True))
        p_loc = jnp.exp(s_loc - m)
        p_ctx = jnp.exp(s_ctx - m)
        denom = jnp.sum(p_loc, axis=-1, keepdims=True) + jnp.sum(p_ctx, axis=-1, keepdims=True)
        o = _dot(p_loc.astype(BF16), bands[u][1]) + _dot(p_ctx.astype(BF16), vx)
        return o * (1.0 / denom)

    outs = {}
    ready = scores(items[0])
    for k, item in enumerate(items):
        nxt = scores(items[k + 1]) if k + 1 < len(items) else None
        outs[item] = attend(item, *ready)
        ready = nxt
    per_head = [jnp.concatenate([outs[(hh, u)] for u in range(n_sub)], axis=0) for hh in range(2)]
    lane_o = lax.broadcasted_iota(jnp.int32, per_head[0].shape, 1)
    o_ref[...] = jnp.where(lane_o < NA_HEAD_DIM, per_head[0], per_head[1]).astype(BF16)


def _na_bias_tables(rpb, rows):
    sub, band, w = NA_SUB_ROWS, NA_BAND_ROWS, GRID_W
    kh, kw = NA_KH, NA_KW
    n_heads = rpb.shape[0]
    cols = np.arange(w)
    c0 = np.clip(cols - kw // 2, 0, w - kw)
    cc = np.arange(w)[None, :]
    col_ok = (cc >= c0[:, None]) & (cc < c0[:, None] + kw)
    padded = jnp.pad(rpb, ((0, 0), (0, 0), (w, w)))
    shifted = jnp.stack([padded[:, :, w + kw - 1 - j:2 * w + kw - 1 - j] for j in range(w)], axis=2)
    t1 = jnp.where(col_ok[None, None], shifted, NEG_BIG)
    masked = jnp.full((n_heads, w, w), NEG_BIG, F32)
    variants = []
    for i0, b0 in ((0, 0), (sub, sub - kh // 2), (rows - sub, rows - band)):
        blocks = []
        for qi in range(i0, i0 + sub):
            r0 = min(max(qi - kh // 2, 0), rows - kh)
            row = [t1[:, r - qi + kh - 1] if r0 <= r < r0 + kh else masked for r in range(b0, b0 + band)]
            blocks.append(jnp.concatenate(row, axis=-1))
        variants.append(jnp.concatenate(blocks, axis=1))
    return jnp.stack(variants)


def _neighbourhood_attention(q, k, v, kx, vx, bias):
    bsz, n, _ = q.shape
    lc = kx.shape[1]
    rows = n // GRID_W
    sub, band = NA_SUB_ROWS, NA_BAND_ROWS
    n_sub = NA_ROWS_PER_STEP // sub
    nrb = rows // NA_ROWS_PER_STEP
    tq = NA_ROWS_PER_STEP * GRID_W
    pair = 2 * NA_HEAD_DIM
    last = rows // sub - 1

    def bias_spec(u):
        def index(b, hp, rb):
            sb = rb * n_sub + u
            return (jnp.where(sb == 0, 0, jnp.where(sb == last, 2, 1)), hp, 0, 0)
        return pl.BlockSpec((None, 2, sub * GRID_W, band * GRID_W), index)

    return pl.pallas_call(
        functools.partial(_na_body, rows=rows),
        out_shape=jax.ShapeDtypeStruct((bsz, n, NA_WIDTH), BF16),
        grid=(bsz, NA_HEADS // 2, nrb),
        in_specs=[
            pl.BlockSpec((None, tq, pair), lambda b, hp, rb: (b, rb, hp)),
            pl.BlockSpec((None, n, pair), lambda b, hp, rb: (b, 0, hp)),
            pl.BlockSpec((None, n, pair), lambda b, hp, rb: (b, 0, hp)),
            pl.BlockSpec((None, lc, pair), lambda b, hp, rb: (b, 0, hp)),
            pl.BlockSpec((None, lc, pair), lambda b, hp, rb: (b, 0, hp)),
            bias_spec(0), bias_spec(1),
        ],
        out_specs=pl.BlockSpec((None, tq, pair), lambda b, hp, rb: (b, rb, hp)),
        compiler_params=_params(3),
        name="neighbourhood_attention",
    )(q, k, v, kx, vx, bias, bias)


def _mlstm_body(*refs, with_output):
    if with_output:
        (qf_ref, kf_ref, vtf_ref, gcf_ref, grf_ref, qr_ref, kr_ref, vtr_ref, gcr_ref, grr_ref,
         c0_ref, m0_ref, hf_ref, hr_ref, c_scr, m_scr) = refs
    else:
        (kf_ref, vtf_ref, grf_ref, kr_ref, vtr_ref, grr_ref,
         c0_ref, m0_ref, c_out_ref, m_out_ref, c_scr, m_scr) = refs
    step = pl.program_id(0)
    n_steps = pl.num_programs(0)
    L, hd = ML_CHUNK, ML_HEAD_DIM
    bsz = kf_ref.shape[0]

    @pl.when(step == 0)
    def _():
        c_scr[...] = c0_ref[...]
        m_scr[...] = m0_ref[...]

    s_idx = lax.broadcasted_iota(jnp.int32, (L, L), 0)
    t_idx = lax.broadcasted_iota(jnp.int32, (L, L), 1)
    ones_rows = (lax.broadcasted_iota(jnp.int32, (ML_AUG_ROWS, L), 0) == 0).astype(BF16)
    fwd_refs = (kf_ref, vtf_ref, grf_ref) + ((qf_ref, gcf_ref, hf_ref) if with_output else (None,) * 3)
    bwd_refs = (kr_ref, vtr_ref, grr_ref) + ((qr_ref, gcr_ref, hr_ref) if with_output else (None,) * 3)
    streams = [(bi, d, hh) for bi in range(bsz) for d in range(2) for hh in range(ML_HEADS)]

    def refs_of(stream):
        bi, d, _ = stream
        return tuple(r if r is None else r.at[bi] for r in (fwd_refs if d == 0 else bwd_refs))

    def front(stream):
        bi, d, hh = stream
        k_ref, _, _, q_ref, _, _ = refs_of(stream)
        sl = slice(hh * hd, (hh + 1) * hd)
        q = q_ref[:, sl]
        return _dot_nt(k_ref[:, sl], q), _dot_nt(c_scr[bi, d, hh].astype(BF16), q)

    def finish(stream, fronts):
        bi, d, hh = stream
        k_ref, vt_ref, gr_ref, _, gc_ref, h_ref = refs_of(stream)
        visible = (s_idx <= t_idx) if d == 0 else (s_idx >= t_idx)
        end = L - 1 if d == 0 else 0
        gi = hh * 4 + d * 2
        sl = slice(hh * hd, (hh + 1) * hd)
        c_row = gr_ref[gi:gi + 1, :]
        b_row = gr_ref[gi + 1:gi + 2, :]
        b_end = b_row[:, end:end + 1]
        m_old = m_scr[bi, d, hh][0:1, 0:1]
        m_new = jnp.maximum(b_end + m_old, b_end + jnp.max(c_row, axis=-1, keepdims=True))
        a_prev = jnp.exp(b_end + m_old - m_new)
        a_tok = jnp.exp(b_end + c_row - m_new)
        k = k_ref[:, sl]
        vt_aug = jnp.concatenate([vt_ref[sl, :], ones_rows], axis=0)
        c_old = c_scr[bi, d, hh]
        if with_output:
            st, inter = fronts
            log_w = jnp.where(visible, b_row + gc_ref[:, gi:gi + 1], NEG_BIG)
            m_intra = jnp.max(log_w, axis=0, keepdims=True)
            p = (st * jnp.exp(log_w - m_intra)).astype(BF16)
            intra = _dot(vt_aug, p)
            log_inter = b_row + m_old
            m_t = jnp.maximum(log_inter, m_intra)
            both = jnp.exp(log_inter - m_t) * inter + jnp.exp(m_intra - m_t) * intra
            den = both[hd:hd + 1, :]
            scale = 1.0 / jnp.maximum(jnp.abs(den), jnp.exp(-m_t))
            h_ref[:, sl] = (both[:hd, :] * scale).T
        av = (vt_aug.astype(F32) * a_tok).astype(BF16)
        c_scr[bi, d, hh] = a_prev * c_old + _dot(av, k)
        m_scr[bi, d, hh] = jnp.broadcast_to(m_new, (V7X_SUBLANES, V7X_LANES))

    ready = front(streams[0]) if with_output else None
    for i, stream in enumerate(streams):
        nxt = front(streams[i + 1]) if with_output and i + 1 < len(streams) else None
        finish(stream, ready)
        ready = nxt

    if not with_output:
        @pl.when(step == n_steps - 1)
        def _():
            c_out_ref[...] = c_scr[...]
            m_out_ref[...] = m_scr[...]


def _mlstm_scan(q, k, vt, gcol, grow, c0, m0, with_output):
    bsz, n, _ = k.shape
    L, hd = ML_CHUNK, ML_HEAD_DIM
    nch = n // L
    fwd = lambda width: pl.BlockSpec((bsz, L, width), lambda c: (0, c, 0))
    rev = lambda width: pl.BlockSpec((bsz, L, width), lambda c: (0, nch - 1 - c, 0))
    fwd_t = lambda height: pl.BlockSpec((bsz, height, L), lambda c: (0, 0, c))
    rev_t = lambda height: pl.BlockSpec((bsz, height, L), lambda c: (0, 0, nch - 1 - c))
    c_spec = pl.BlockSpec((bsz, 2, ML_HEADS, hd + ML_AUG_ROWS, hd), lambda c: (0, 0, 0, 0, 0))
    m_spec = pl.BlockSpec((bsz, 2, ML_HEADS, V7X_SUBLANES, V7X_LANES), lambda c: (0, 0, 0, 0, 0))
    if with_output:
        args = [q, k, vt, gcol, grow, q, k, vt, gcol, grow, c0, m0]
        in_specs = ([fwd(ML_WIDTH)] * 2 + [fwd_t(ML_WIDTH), fwd(V7X_LANES), fwd_t(N_GATES)]
                    + [rev(ML_WIDTH)] * 2 + [rev_t(ML_WIDTH), rev(V7X_LANES), rev_t(N_GATES)])
        out_shape = [jax.ShapeDtypeStruct((bsz, n, ML_WIDTH), F32)] * 2
        out_specs = [fwd(ML_WIDTH), rev(ML_WIDTH)]
    else:
        args = [k, vt, grow, k, vt, grow, c0, m0]
        in_specs = ([fwd(ML_WIDTH), fwd_t(ML_WIDTH), fwd_t(N_GATES)]
                    + [rev(ML_WIDTH), rev_t(ML_WIDTH), rev_t(N_GATES)])
        out_shape = [jax.ShapeDtypeStruct(c0.shape, F32), jax.ShapeDtypeStruct(m0.shape, F32)]
        out_specs = [c_spec, m_spec]
    return pl.pallas_call(
        functools.partial(_mlstm_body, with_output=with_output),
        out_shape=out_shape,
        grid=(nch,),
        in_specs=in_specs + [c_spec, m_spec],
        out_specs=out_specs,
        scratch_shapes=[pltpu.VMEM((bsz, 2, ML_HEADS, hd + ML_AUG_ROWS, hd), F32),
                        pltpu.VMEM((bsz, 2, ML_HEADS, V7X_SUBLANES, V7X_LANES), F32)],
        compiler_params=_params(1),
        name="mlstm_scan_latent" if with_output else "mlstm_scan_context",
    )(*args)


def _gelu(x):
    k2 = 2.0 * (2.0 / np.pi) ** 0.5
    z = x * (k2 + (k2 * 0.044715) * (x * x))
    return x * jax.nn.sigmoid(z)


def _sg_body(x_ref, mod_ref, g_ref, win_ref, lng_ref, lnb_ref, ws_ref, bs_ref, wout_ref, o_ref, v_scr,
             *, d, width, tm):
    x = x_ref[...]
    mod = mod_ref[...]
    h = _modulated(x, g_ref[...], mod, d).astype(BF16)
    gw = width // SG_GROUPS
    groups = [slice(g * gw, (g + 1) * gw) for g in range(SG_GROUPS)]
    order = [width + cs.start for cs in groups] + [cs.start for cs in groups]
    project = lambda k: _dot(h, win_ref[:, order[k]:order[k] + gw])
    pre = project(0)
    total = None
    for k, cs in enumerate(groups):
        nxt = project(k + 1)
        vg = _gelu(pre)
        v_scr[:, cs] = vg
        part = jnp.sum(vg, axis=-1, keepdims=True)
        total = part if total is None else total + part
        pre = nxt
    mu = total * (1.0 / width)
    sq = None
    for cs in groups:
        vc = v_scr[:, cs] - mu
        part = jnp.sum(vc * vc, axis=-1, keepdims=True)
        sq = part if sq is None else sq + part
    rstd = lax.rsqrt(sq * (1.0 / width) + EPS)
    bs = bs_ref[...]
    y = None
    for g, cs in enumerate(groups):
        nxt = project(SG_GROUPS + g + 1) if g + 1 < SG_GROUPS else None
        vn = ((v_scr[:, cs] - mu) * rstd * lng_ref[:, cs] + lnb_ref[:, cs]).astype(BF16)
        u = _gelu(pre)
        pre = nxt
        parts = []
        for c in range(tm // SG_CHUNK):
            rs = slice(c * SG_CHUNK, (c + 1) * SG_CHUNK)
            mixed = _dot(ws_ref[g], vn[rs]) + bs[:, g:g + 1]
            parts.append((u[rs] * mixed).astype(BF16))
        yg = _dot(jnp.concatenate(parts, axis=0), wout_ref[cs, :])
        y = yg if y is None else y + yg
    o_ref[...] = x + mod[:, 2 * d:] * y


def _spatial_gating(x, mod3, g, w_in, ln_g, ln_b, w_s, b_s, w_out, tm=512):
    bsz, n, d = x.shape
    width = w_out.shape[0]
    w_in, w_s, w_out = w_in.astype(BF16), w_s.astype(BF16), w_out.astype(BF16)
    const = lambda shape: pl.BlockSpec(shape, lambda b, i: (0,) * len(shape))
    return pl.pallas_call(
        functools.partial(_sg_body, d=d, width=width, tm=tm),
        out_shape=jax.ShapeDtypeStruct(x.shape, F32),
        grid=(bsz, n // tm),
        in_specs=[
            pl.BlockSpec((None, tm, d), lambda b, i: (b, i, 0)),
            pl.BlockSpec((None, 1, 3 * d), lambda b, i: (b, 0, 0)),
            const((1, d)), const(w_in.shape), const((1, width)), const((1, width)),
            const(w_s.shape), const((SG_CHUNK, SG_GROUPS)), const(w_out.shape),
        ],
        out_specs=pl.BlockSpec((None, tm, d), lambda b, i: (b, i, 0)),
        scratch_shapes=[pltpu.VMEM((tm, width), F32)],
        compiler_params=_params(2),
        name="spatial_gating",
    )(x, mod3, g.reshape(1, d), w_in, ln_g.reshape(1, width), ln_b.reshape(1, width), w_s, b_s.T, w_out)


def _rope_tables(n):
    hd = ML_HEAD_DIM
    n_pairs = hd // 4
    inv_freq = ROPE_THETA ** (-jnp.arange(n_pairs, dtype=F32) / n_pairs)
    row_ang = jnp.arange(n // GRID_W, dtype=F32)[:, None] * inv_freq
    col_ang = jnp.arange(GRID_W, dtype=F32)[:, None] * inv_freq

    def table(ang, is_row):
        zero = jnp.zeros_like(ang)
        cos, sin = jnp.cos(ang), jnp.sin(ang)
        half = lambda t: jnp.concatenate([t, zero] if is_row else [zero, t], axis=-1)
        return jnp.stack([jnp.concatenate([half(cos), half(cos)], axis=-1),
                          jnp.concatenate([half(-sin), half(sin)], axis=-1)])

    return table(row_ang, True), table(col_ang, False)


def _even_weights(w_in, gate_b):
    d = w_in.shape[0]
    hd = ML_HEAD_DIM
    base = 3 * NA_WIDTH
    deint = lambda w: w.reshape(d, ML_HEADS, hd // 2, 2).transpose(0, 1, 3, 2).reshape(d, ML_WIDTH)
    qb = deint(w_in[:, base:base + ML_WIDTH])
    kb = deint(w_in[:, base + ML_WIDTH:base + 2 * ML_WIDTH])
    w_main = jnp.concatenate([w_in[:, :base], qb, kb, w_in[:, base + 3 * ML_WIDTH:base + 4 * ML_WIDTH]],
                             axis=1).astype(BF16)
    w_vt = w_in[:, base + 2 * ML_WIDTH:base + 3 * ML_WIDTH].T.astype(BF16)
    wg = w_in[:, base + 4 * ML_WIDTH:]
    w_gc = jnp.pad(wg, ((0, 0), (0, V7X_LANES - N_GATES))).astype(BF16)
    w_gr = wg.T.astype(BF16)
    gb = gate_b.reshape(N_GATES).astype(F32)
    gb_col = jnp.pad(gb, (0, V7X_LANES - N_GATES)).reshape(1, V7X_LANES)
    gb_row = gb.reshape(N_GATES, 1)
    return w_main, w_vt, w_gc, w_gr, gb_col, gb_row


def kernel(x, c, ctx, c_ctx, w_mod, b_mod, norm_g, ffn_w_in, ffn_w_out, mix_w_in, na_rpb, ml_gate_b, ml_head_g,
           mix_w_out, sg_w_in, sg_ln_g, sg_ln_b, sg_w_s, sg_b_s, sg_w_out, final_g):
    bsz, n, d = x.shape
    depth = w_mod.shape[0]
    ctx_row = bsz
    cvec = jnp.zeros((V7X_SUBLANES, d), F32).at[:bsz].set(c).at[ctx_row].set(c_ctx)
    mod = _mod_vectors(cvec, w_mod, b_mod).reshape(depth, V7X_SUBLANES, 3, 1, 3 * d)
    batch_row = lambda b: b
    context_row = lambda b: ctx_row
    last_ctx_layer = ((depth - 1) // 2) * 2
    ffn_w_in = ffn_w_in.astype(BF16)
    ffn_w_out = ffn_w_out.astype(BF16)
    xc = ctx
    for l in range(depth):
        ctx_in = l <= last_ctx_layer
        ctx_out = l < last_ctx_layer
        x = _ffn(x, mod[l, :, 0], batch_row, norm_g[l, 0], ffn_w_in, ffn_w_out, l, 0)
        if ctx_in:
            xc = _ffn(xc, mod[l, :, 0], context_row, norm_g[l, 0], ffn_w_in, ffn_w_out, l, 0)
        if l % 2 == 0:
            e = l // 2
            weights = _even_weights(mix_w_in[e], ml_gate_b[e])
            qa, ka, va, qb, kb, vt, ob, gcol, grow = _even_projection(
                x, mod[l, :, 1], batch_row, norm_g[l, 1], weights, _rope_tables(n))
            _, kax, vax, _, kbx, vtx, _, _, growx = _even_projection(
                xc, mod[l, :, 1], context_row, norm_g[l, 1], weights, None)
            bias = _na_bias_tables(na_rpb[e], n // GRID_W)
            ya = _neighbourhood_attention(qa, ka, va, kax, vax, bias)
            c0 = jnp.zeros((bsz, 2, ML_HEADS, ML_HEAD_DIM + ML_AUG_ROWS, ML_HEAD_DIM), F32)
            m0 = jnp.zeros((bsz, 2, ML_HEADS, V7X_SUBLANES, V7X_LANES), F32)
            c1, m1 = _mlstm_scan(None, kbx, vtx, None, growx, c0, m0, False)
            hf, hr = _mlstm_scan(qb, kb, vt, gcol, grow, c1, m1, True)
            mixer = (mod[l, :, 1], ya, hf, hr, ob, ml_head_g[e], mix_w_out[e].astype(BF16))
            assert not ctx_out, "context output path is not needed for this depth"
        else:
            o = l // 2
            mixer = None
            x = _spatial_gating(x, mod[l, :, 1], norm_g[l, 1], sg_w_in[o], sg_ln_g[o], sg_ln_b[o],
                                sg_w_s[o], sg_b_s[o], sg_w_out[o])
        fg = final_g if l == depth - 1 else None
        x = _ffn(x, mod[l, :, 2], batch_row, norm_g[l, 2], ffn_w_in, ffn_w_out, l, 1, final_g=fg, mixer=mixer)
    return x
```

```python
import functools

import numpy as np
import jax
import jax.numpy as jnp
from jax import lax
from jax.experimental import pallas as pl
from jax.experimental.pallas import tpu as pltpu

GRID_W = 64
NA_HEADS = 8
NA_HEAD_DIM = 64
NA_KH = 8
NA_KW = 16
ML_HEADS = 4
ML_HEAD_DIM = 128
ML_CHUNK = 128
ROPE_THETA = 10000.0
SG_CHUNK = 128
SG_GROUPS = 8
EPS = 1e-6
NA_WIDTH = NA_HEADS * NA_HEAD_DIM
ML_WIDTH = ML_HEADS * ML_HEAD_DIM
N_GATES = 4 * ML_HEADS
ML_AUG_ROWS = 16

V7X_LANES = 128
V7X_SUBLANES = 8
V7X_MXU_COLUMNS = 256
V7X_VMEM_LIMIT_BYTES = 56 * 1024 * 1024

NEG_BIG = -1e30
FFN_TOKENS_PER_STEP = 1024
NA_ROWS_PER_STEP = 16
NA_SUB_ROWS = 4
NA_BAND_ROWS = 12

BF16 = jnp.bfloat16
F32 = jnp.float32


def _dot(a, b):
    return jnp.dot(a, b, preferred_element_type=F32)


def _dot_nt(a, b):
    return lax.dot_general(a, b, (((1,), (1,)), ((), ())), preferred_element_type=F32)


def _params(n_axes):
    return pltpu.CompilerParams(
        dimension_semantics=("arbitrary",) * n_axes,
        vmem_limit_bytes=V7X_VMEM_LIMIT_BYTES,
    )


def _modulated(x, g, mod, d):
    shift = mod[:, :d]
    scale = mod[:, d:2 * d]
    gs = g * (1.0 + scale)
    ms = jnp.mean(x * x, axis=-1, keepdims=True)
    return x * lax.rsqrt(ms + EPS) * gs + shift


def _mod_body(c_ref, w_ref, b_ref, o_ref):
    c = c_ref[...]
    s = (c * jax.nn.sigmoid(c)).astype(BF16)
    o_ref[...] = _dot(s, w_ref[...].astype(BF16)) + b_ref[...]


def _mod_vectors(cvec, w_mod, b_mod):
    depth, d, width = w_mod.shape
    tn = width // 8
    return pl.pallas_call(
        _mod_body,
        out_shape=jax.ShapeDtypeStruct((depth, V7X_SUBLANES, width), F32),
        grid=(depth, width // tn),
        in_specs=[
            pl.BlockSpec((V7X_SUBLANES, d), lambda l, j: (0, 0)),
            pl.BlockSpec((None, d, tn), lambda l, j: (l, 0, j)),
            pl.BlockSpec((None, 1, tn), lambda l, j: (l, 0, j)),
        ],
        out_specs=pl.BlockSpec((None, V7X_SUBLANES, tn), lambda l, j: (l, 0, j)),
        compiler_params=_params(2),
        name="mod_vectors",
    )(cvec, w_mod, b_mod.reshape(depth, 1, width))


def _mixer_output(ya_ref, hf_ref, hr_ref, ob_ref, hg_ref, w_ref):
    hd = ML_HEAD_DIM
    hb = hf_ref[...] + hr_ref[...]
    ob = ob_ref[...]
    hg = hg_ref[...]
    y = _dot(ya_ref[...], w_ref[0:NA_WIDTH, :])
    ybs = []
    for hh in range(ML_HEADS):
        sl = slice(hh * hd, (hh + 1) * hd)
        hs = hb[:, sl]
        ms = jnp.mean(hs * hs, axis=-1, keepdims=True)
        ybs.append((hs * lax.rsqrt(ms + EPS) * hg[:, sl] * jax.nn.sigmoid(ob[:, sl])).astype(BF16))
    return y + _dot(jnp.concatenate(ybs, axis=1), w_ref[NA_WIDTH:, :])


def _ffn_body(*refs, d, d_ff, chunk, final, mixer):
    refs = list(refs)
    o_ref = refs.pop()
    x_ref, mod_ref, g_ref, win_ref, wout_ref = refs[:5]
    rest = refs[5:]
    x = x_ref[...]
    if mixer:
        mixmod_ref = rest[0]
        x = x + mixmod_ref[...][:, 2 * d:] * _mixer_output(*rest[1:7])
        rest = rest[7:]
    if final:
        (fg_ref,) = rest
    mod = mod_ref[...]
    h = _modulated(x, g_ref[...], mod, d).astype(BF16)
    acc = None
    for c in range(d_ff // chunk):
        a = _dot(h, win_ref[:, c * chunk:(c + 1) * chunk])
        b = _dot(h, win_ref[:, d_ff + c * chunk:d_ff + (c + 1) * chunk])
        t = (a * jax.nn.sigmoid(a) * b).astype(BF16)
        y = _dot(t, wout_ref[c * chunk:(c + 1) * chunk, :])
        acc = y if acc is None else acc + y
    out = x + (0.5 * mod[:, 2 * d:]) * acc
    if final:
        ms = jnp.mean(out * out, axis=-1, keepdims=True)
        out = out * lax.rsqrt(ms + EPS) * fg_ref[...]
    o_ref[...] = out


def _ffn(x, mod3, mod_row, g, w_in_all, w_out_all, layer, half, final_g=None, mixer=None, tm=None):
    bsz, n, d = x.shape
    d_ff = w_out_all.shape[2]
    if tm is None:
        tm = FFN_TOKENS_PER_STEP // 2 if mixer is not None else FFN_TOKENS_PER_STEP
    tm = min(tm, n)
    final = final_g is not None
    tok_spec = lambda width: pl.BlockSpec((None, tm, width), lambda b, i: (b, i, 0))
    resident = dict(pipeline_mode=pl.Buffered(1))
    in_specs = [
        tok_spec(d),
        pl.BlockSpec((None, 1, 3 * d), lambda b, i: (mod_row(b), 0, 0)),
        pl.BlockSpec((1, d), lambda b, i: (0, 0)),
        pl.BlockSpec((None, None, d, 2 * d_ff), lambda b, i: (layer, half, 0, 0), **resident),
        pl.BlockSpec((None, None, d_ff, d), lambda b, i: (layer, half, 0, 0), **resident),
    ]
    args = [x, mod3, g.reshape(1, d), w_in_all, w_out_all]
    if mixer is not None:
        mix_mod3, ya, hf, hr, ob, head_g, w_mix = mixer
        in_specs += [
            pl.BlockSpec((None, 1, 3 * d), lambda b, i: (mod_row(b), 0, 0)),
            tok_spec(NA_WIDTH), tok_spec(ML_WIDTH), tok_spec(ML_WIDTH), tok_spec(ML_WIDTH),
            pl.BlockSpec((1, ML_WIDTH), lambda b, i: (0, 0)),
            pl.BlockSpec(w_mix.shape, lambda b, i: (0, 0), **resident),
        ]
        args += [mix_mod3, ya, hf, hr, ob, head_g.reshape(1, ML_WIDTH), w_mix]
    if final:
        in_specs.append(pl.BlockSpec((1, d), lambda b, i: (0, 0)))
        args.append(final_g.reshape(1, d))
    return pl.pallas_call(
        functools.partial(_ffn_body, d=d, d_ff=d_ff, chunk=V7X_MXU_COLUMNS, final=final,
                          mixer=mixer is not None),
        out_shape=jax.ShapeDtypeStruct(x.shape, F32),
        grid=(bsz, n // tm),
        in_specs=in_specs,
        out_specs=pl.BlockSpec((None, tm, d), lambda b, i: (b, i, 0)),
        compiler_params=_params(2),
        name="macaron_ffn",
    )(*args)


def _log_gates(z, is_forget):
    ls = jnp.minimum(z, 0.0) - jnp.log1p(jnp.exp(-jnp.abs(z)))
    return jnp.where(is_forget, ls, z)


def _split3(x):
    hi = x.astype(BF16)
    r1 = x - hi.astype(F32)
    mid = r1.astype(BF16)
    lo = (r1 - mid.astype(F32)).astype(BF16)
    return [hi, mid, lo]


def _evenproj_body(x_ref, mod_ref, g_ref, w_ref, wvt_ref, wg_ref, wgt_ref, gbc_ref, gbr_ref, rowt_ref, colt_ref,
                   qa_ref, ka_ref, va_ref, qb_ref, kb_ref, vt_ref, ob_ref, gc_ref, gr_ref, *, d, rope):
    h = _modulated(x_ref[...], g_ref[...], mod_ref[...], d).astype(BF16)
    gates = _gate_stages(h, wg_ref, wgt_ref, gbc_ref, gbr_ref, gc_ref, gr_ref)
    nw, mw, hd = NA_WIDTH, ML_WIDTH, ML_HEAD_DIM
    next(gates, None)
    qa_ref[...] = (_dot(h, w_ref[:, 0:nw]) * (NA_HEAD_DIM ** -0.5)).astype(BF16)
    next(gates, None)
    ka_ref[...] = _dot(h, w_ref[:, nw:2 * nw]).astype(BF16)
    next(gates, None)
    va_ref[...] = _dot(h, w_ref[:, 2 * nw:3 * nw]).astype(BF16)
    next(gates, None)
    base = 3 * nw
    qb = _dot(h, w_ref[:, base:base + mw])
    next(gates, None)
    kb = _dot(h, w_ref[:, base + mw:base + 2 * mw])
    next(gates, None)
    if rope:
        tm = qb.shape[0]
        expand = lambda t: jnp.broadcast_to(t[:, None, :], (tm // GRID_W, GRID_W, hd)).reshape(tm, hd)
        cos = expand(rowt_ref[0]) + colt_ref[0]
        sin = expand(rowt_ref[1]) + colt_ref[1]
        for hh in range(ML_HEADS):
            sl = slice(hh * hd, (hh + 1) * hd)
            qh = qb[:, sl]
            kh = kb[:, sl]
            qb_ref[:, sl] = (qh * cos + pltpu.roll(qh, hd // 2, 1) * sin).astype(BF16)
            kb_ref[:, sl] = ((kh * cos + pltpu.roll(kh, hd // 2, 1) * sin) * (hd ** -0.5)).astype(BF16)
    else:
        qb_ref[...] = qb.astype(BF16)
        kb_ref[...] = (kb * (hd ** -0.5)).astype(BF16)
    vt_ref[...] = _dot_nt(wvt_ref[...], h).astype(BF16)
    next(gates, None)
    ob_ref[...] = _dot(h, w_ref[:, base + 2 * mw:base + 3 * mw])
    for _ in gates:
        pass


def _gate_stages(h, wg_ref, wgt_ref, gbc_ref, gbr_ref, gc_ref, gr_ref):
    L = ML_CHUNK
    tm = h.shape[0]
    r_idx = lax.broadcasted_iota(jnp.int32, (L, L), 0)
    c_idx = lax.broadcasted_iota(jnp.int32, (L, L), 1)
    lower = (c_idx <= r_idx).astype(BF16)
    upper = (r_idx <= c_idx).astype(BF16)
    zc = _dot(h, wg_ref[...]) + gbc_ref[...]
    zr = _dot_nt(wgt_ref[...], h) + gbr_ref[...]
    yield
    lane = lax.broadcasted_iota(jnp.int32, zc.shape, 1)
    gates_c = _log_gates(zc, (lane % 2) == 1)
    split_c = jnp.concatenate(_split3(gates_c), axis=1)
    row = lax.broadcasted_iota(jnp.int32, zr.shape, 0)
    gates_r = _log_gates(zr, (row % 2) == 1)
    split_r = jnp.concatenate(_split3(gates_r), axis=0)
    yield
    bwd_lane = ((lax.broadcasted_iota(jnp.int32, (L, V7X_LANES), 1) // 2) % 2) == 1
    for c in range(tm // L):
        rs = slice(c * L, (c + 1) * L)
        r = _dot(lower, split_c[rs])
        prefix = r[:, 0:V7X_LANES] + r[:, V7X_LANES:2 * V7X_LANES] + r[:, 2 * V7X_LANES:]
        g = gates_c[rs]
        b = jnp.where(bwd_lane, prefix[L - 1:L, :] - prefix + g, prefix)
        gc_ref[rs, :] = g - pltpu.roll(b, V7X_LANES - 1, 1)
        if c % 2 == 1:
            yield
    row_l = lax.broadcasted_iota(jnp.int32, (N_GATES, L), 0)
    bwd_row = ((row_l // 2) % 2) == 1
    f_row = (row_l % 2) == 1
    for c in range(tm // L):
        ls = slice(c * L, (c + 1) * L)
        r = _dot(split_r[:, ls], upper)
        prefix = r[0:N_GATES] + r[N_GATES:2 * N_GATES] + r[2 * N_GATES:]
        g = gates_r[:, ls]
        b = jnp.where(bwd_row, prefix[:, L - 1:L] - prefix + g, prefix)
        gr_ref[:, ls] = jnp.where(f_row, b, g - pltpu.roll(b, N_GATES - 1, 0))
        if c % 2 == 1:
            yield


def _even_projection(x, mod3, mod_row, g, weights, rope_tables, tm=1024):
    w_main, w_vt, w_gc, w_gr, gb_col, gb_row = weights
    bsz, n, d = x.shape
    tm = min(tm, n)
    rope = rope_tables is not None
    if rope:
        row_t, col_t = rope_tables
        col_t = jnp.tile(col_t, (1, tm // GRID_W, 1))
    else:
        row_t = jnp.zeros((2, tm // GRID_W, ML_HEAD_DIM), F32)
        col_t = jnp.zeros((2, tm, ML_HEAD_DIM), F32)
    wm = w_main.shape[1]
    tok = lambda width, dt: jax.ShapeDtypeStruct((bsz, n, width), dt)
    tok_spec = lambda width: pl.BlockSpec((None, tm, width), lambda b, i: (b, i, 0))
    const = lambda shape: pl.BlockSpec(shape, lambda b, i: (0,) * len(shape))
    out_shape = [tok(NA_WIDTH, BF16)] * 3 + [tok(ML_WIDTH, BF16)] * 2 + [
        jax.ShapeDtypeStruct((bsz, ML_WIDTH, n), BF16), tok(ML_WIDTH, F32),
        tok(V7X_LANES, F32), jax.ShapeDtypeStruct((bsz, N_GATES, n), F32)]
    out_specs = [tok_spec(NA_WIDTH)] * 3 + [tok_spec(ML_WIDTH)] * 2 + [
        pl.BlockSpec((None, ML_WIDTH, tm), lambda b, i: (b, 0, i)), tok_spec(ML_WIDTH),
        tok_spec(V7X_LANES), pl.BlockSpec((None, N_GATES, tm), lambda b, i: (b, 0, i))]
    return pl.pallas_call(
        functools.partial(_evenproj_body, d=d, rope=rope),
        out_shape=out_shape,
        grid=(bsz, n // tm),
        in_specs=[
            tok_spec(d),
            pl.BlockSpec((None, 1, 3 * d), lambda b, i: (mod_row(b), 0, 0)),
            const((1, d)),
            const((d, wm)),
            const((ML_WIDTH, d)),
            const((d, V7X_LANES)),
            const((N_GATES, d)),
            const((1, V7X_LANES)),
            const((N_GATES, 1)),
            pl.BlockSpec((2, tm // GRID_W, ML_HEAD_DIM), lambda b, i: (0, i if rope else 0, 0)),
            pl.BlockSpec((2, tm, ML_HEAD_DIM), lambda b, i: (0, 0, 0)),
        ],
        out_specs=out_specs,
        compiler_params=_params(2),
        name="even_projection",
    )(x, mod3, g.reshape(1, d), w_main, w_vt, w_gc, w_gr, gb_col, gb_row, row_t, col_t)


def _na_body(q_ref, k_ref, v_ref, kx_ref, vx_ref, *rest, rows):
    rb = pl.program_id(2)
    sub, band = NA_SUB_ROWS, NA_BAND_ROWS
    n_sub = NA_ROWS_PER_STEP // sub
    tq = sub * GRID_W
    bias_refs, o_ref = rest[:n_sub], rest[n_sub]
    kx = kx_ref[...]
    vx = vx_ref[...]
    q = q_ref[...]
    lane = lax.broadcasted_iota(jnp.int32, q.shape, 1)
    bands = []
    for u in range(n_sub):
        band0 = jnp.clip((rb * n_sub + u) * sub - NA_KH // 2, 0, rows - band)
        start = pl.multiple_of(band0 * GRID_W, GRID_W)
        bands.append((k_ref[pl.ds(start, band * GRID_W), :], v_ref[pl.ds(start, band * GRID_W), :]))
    qms = [jnp.where((lane // NA_HEAD_DIM) == hh, q, jnp.zeros_like(q)) for hh in range(2)]
    items = [(hh, u) for hh in range(2) for u in range(n_sub)]

    def scores(item):
        hh, u = item
        qm = qms[hh][u * tq:(u + 1) * tq]
        return _dot_nt(qm, bands[u][0]) + bias_refs[u][hh], _dot_nt(qm, kx)

    def attend(item, s_loc, s_ctx):
        u = item[1]
        m = jnp.maximum(jnp.max(s_loc, axis=-1, keepdims=True), jnp.max(s_ctx, axis=-1, keepdims=True))
        p_loc = jnp.exp(s_loc - m)
        p_ctx = jnp.exp(s_ctx - m)
        denom = jnp.sum(p_loc, axis=-1, keepdims=True) + jnp.sum(p_ctx, axis=-1, keepdims=True)
        o = _dot(p_loc.astype(BF16), bands[u][1]) + _dot(p_ctx.astype(BF16), vx)
        return o * (1.0 / denom)

    outs = {}
    ready = scores(items[0])
    for k, item in enumerate(items):
        nxt = scores(items[k + 1]) if k + 1 < len(items) else None
        outs[item] = attend(item, *ready)
        ready = nxt
    per_head = [jnp.concatenate([outs[(hh, u)] for u in range(n_sub)], axis=0) for hh in range(2)]
    lane_o = lax.broadcasted_iota(jnp.int32, per_head[0].shape, 1)
    o_ref[...] = jnp.where(lane_o < NA_HEAD_DIM, per_head[0], per_head[1]).astype(BF16)


def _na_bias_tables(rpb, rows):
    sub, band, w = NA_SUB_ROWS, NA_BAND_ROWS, GRID_W
    kh, kw = NA_KH, NA_KW
    n_heads = rpb.shape[0]
    cols = np.arange(w)
    c0 = np.clip(cols - kw // 2, 0, w - kw)
    cc = np.arange(w)[None, :]
    col_ok = (cc >= c0[:, None]) & (cc < c0[:, None] + kw)
    padded = jnp.pad(rpb, ((0, 0), (0, 0), (w, w)))
    shifted = jnp.stack([padded[:, :, w + kw - 1 - j:2 * w + kw - 1 - j] for j in range(w)], axis=2)
    t1 = jnp.where(col_ok[None, None], shifted, NEG_BIG)
    masked = jnp.full((n_heads, w, w), NEG_BIG, F32)
    variants = []
    for i0, b0 in ((0, 0), (sub, sub - kh // 2), (rows - sub, rows - band)):
        blocks = []
        for qi in range(i0, i0 + sub):
            r0 = min(max(qi - kh // 2, 0), rows - kh)
            row = [t1[:, r - qi + kh - 1] if r0 <= r < r0 + kh else masked for r in range(b0, b0 + band)]
            blocks.append(jnp.concatenate(row, axis=-1))
        variants.append(jnp.concatenate(blocks, axis=1))
    return jnp.stack(variants)


def _neighbourhood_attention(q, k, v, kx, vx, bias):
    bsz, n, _ = q.shape
    lc = kx.shape[1]
    rows = n // GRID_W
    sub, band = NA_SUB_ROWS, NA_BAND_ROWS
    n_sub = NA_ROWS_PER_STEP // sub
    nrb = rows // NA_ROWS_PER_STEP
    tq = NA_ROWS_PER_STEP * GRID_W
    pair = 2 * NA_HEAD_DIM
    last = rows // sub - 1

    def bias_spec(u):
        def index(b, hp, rb):
            sb = rb * n_sub + u
            return (jnp.where(sb == 0, 0, jnp.where(sb == last, 2, 1)), hp, 0, 0)
        return pl.BlockSpec((None, 2, sub * GRID_W, band * GRID_W), index)

    return pl.pallas_call(
        functools.partial(_na_body, rows=rows),
        out_shape=jax.ShapeDtypeStruct((bsz, n, NA_WIDTH), BF16),
        grid=(bsz, NA_HEADS // 2, nrb),
        in_specs=[
            pl.BlockSpec((None, tq, pair), lambda b, hp, rb: (b, rb, hp)),
            pl.BlockSpec((None, n, pair), lambda b, hp, rb: (b, 0, hp)),
            pl.BlockSpec((None, n, pair), lambda b, hp, rb: (b, 0, hp)),
            pl.BlockSpec((None, lc, pair), lambda b, hp, rb: (b, 0, hp)),
            pl.BlockSpec((None, lc, pair), lambda b, hp, rb: (b, 0, hp)),
        ] + [bias_spec(u) for u in range(n_sub)],
        out_specs=pl.BlockSpec((None, tq, pair), lambda b, hp, rb: (b, rb, hp)),
        compiler_params=_params(3),
        name="neighbourhood_attention",
    )(q, k, v, kx, vx, *([bias] * n_sub))


def _mlstm_body(*refs, with_output):
    if with_output:
        (qf_ref, kf_ref, vtf_ref, gcf_ref, grf_ref, qr_ref, kr_ref, vtr_ref, gcr_ref, grr_ref,
         c0_ref, m0_ref, hf_ref, hr_ref, c_scr, m_scr) = refs
    else:
        (kf_ref, vtf_ref, grf_ref, kr_ref, vtr_ref, grr_ref,
         c0_ref, m0_ref, c_out_ref, m_out_ref, c_scr, m_scr) = refs
    step = pl.program_id(0)
    n_steps = pl.num_programs(0)
    L, hd = ML_CHUNK, ML_HEAD_DIM
    bsz = kf_ref.shape[0]

    @pl.when(step == 0)
    def _():
        c_scr[...] = c0_ref[...]
        m_scr[...] = m0_ref[...]

    s_idx = lax.broadcasted_iota(jnp.int32, (L, L), 0)
    t_idx = lax.broadcasted_iota(jnp.int32, (L, L), 1)
    ones_rows = (lax.broadcasted_iota(jnp.int32, (ML_AUG_ROWS, L), 0) == 0).astype(BF16)
    fwd_refs = (kf_ref, vtf_ref, grf_ref) + ((qf_ref, gcf_ref, hf_ref) if with_output else (None,) * 3)
    bwd_refs = (kr_ref, vtr_ref, grr_ref) + ((qr_ref, gcr_ref, hr_ref) if with_output else (None,) * 3)
    items = [(bi, d, p) for bi in range(bsz) for d in range(2) for p in range(ML_HEADS // 2)]
    lane2 = lax.broadcasted_iota(jnp.int32, (L, 2 * hd), 1)
    zero_ll = jnp.zeros((L, L), BF16)

    def refs_of(item):
        bi, d, _ = item
        return tuple(r if r is None else r.at[bi] for r in (fwd_refs if d == 0 else bwd_refs))

    def block_diag(x2):
        zero = jnp.zeros_like(x2)
        return jnp.concatenate([jnp.where(lane2 < hd, x2, zero), jnp.where(lane2 >= hd, x2, zero)], axis=0)

    def front(item):
        bi, d, p = item
        k_ref, _, _, q_ref, _, _ = refs_of(item)
        sl2 = slice(2 * p * hd, (2 * p + 2) * hd)
        qbd = block_diag(q_ref[:, sl2])
        return _dot_nt(k_ref[:, sl2], qbd), _dot_nt(c_scr[bi, d, p].astype(BF16), qbd)

    def finish(item, fronts):
        bi, d, p = item
        k_ref, vt_ref, gr_ref, _, gc_ref, h_ref = refs_of(item)
        visible = (s_idx <= t_idx) if d == 0 else (s_idx >= t_idx)
        end = L - 1 if d == 0 else 0
        sl2 = slice(2 * p * hd, (2 * p + 2) * hd)
        a_prevs, a_toks, vt_augs, ps, w_inters, w_corrs, inv_floors = [], [], [], [], [], [], []
        for j in range(2):
            hh = 2 * p + j
            gi = hh * 4 + d * 2
            c_row = gr_ref[gi:gi + 1, :]
            b_row = gr_ref[gi + 1:gi + 2, :]
            b_end = b_row[:, end:end + 1]
            m_old = m_scr[bi, d, hh][0:1, 0:1]
            m_new = jnp.maximum(b_end + m_old, b_end + jnp.max(c_row, axis=-1, keepdims=True))
            a_prevs.append(jnp.broadcast_to(jnp.exp(b_end + m_old - m_new), (1, hd)))
            a_toks.append(jnp.exp(b_end + c_row - m_new))
            vt_augs.append(jnp.concatenate([vt_ref[hh * hd:(hh + 1) * hd, :], ones_rows], axis=0))
            m_scr[bi, d, hh] = jnp.broadcast_to(m_new, (V7X_SUBLANES, V7X_LANES))
            if with_output:
                st = fronts[0][:, j * L:(j + 1) * L]
                log_w = jnp.where(visible, b_row + gc_ref[:, gi:gi + 1], NEG_BIG)
                m_intra = jnp.max(log_w, axis=0, keepdims=True)
                ps.append((st * jnp.exp(log_w - m_intra)).astype(BF16))
                log_inter = b_row + m_old
                m_t = jnp.maximum(log_inter, m_intra)
                w_inters.append(jnp.exp(log_inter - m_t))
                w_corrs.append(jnp.exp(m_intra - m_t))
                inv_floors.append(jnp.exp(-m_t))
        cat = lambda parts: jnp.concatenate(parts, axis=1)
        vt2 = cat(vt_augs)
        c_old = c_scr[bi, d, p]
        if with_output:
            pbd = jnp.concatenate([cat([ps[0], zero_ll]), cat([zero_ll, ps[1]])], axis=0)
            intra = _dot(vt2, pbd)
            both = cat(w_inters) * fronts[1] + cat(w_corrs) * intra
            den = both[hd:hd + 1, :]
            ht = both[:hd, :] * (1.0 / jnp.maximum(jnp.abs(den), cat(inv_floors)))
            for j in range(2):
                h_ref[:, (2 * p + j) * hd:(2 * p + j + 1) * hd] = ht[:, j * L:(j + 1) * L].T
        av = (vt2.astype(F32) * cat(a_toks)).astype(BF16)
        c_scr[bi, d, p] = cat(a_prevs) * c_old + _dot(av, block_diag(k_ref[:, sl2]))

    ready = front(items[0]) if with_output else None
    for i, item in enumerate(items):
        nxt = front(items[i + 1]) if with_output and i + 1 < len(items) else None
        finish(item, ready)
        ready = nxt

    if not with_output:
        @pl.when(step == n_steps - 1)
        def _():
            c_out_ref[...] = c_scr[...]
            m_out_ref[...] = m_scr[...]


def _mlstm_scan(q, k, vt, gcol, grow, c0, m0, with_output):
    bsz, n, _ = k.shape
    L, hd = ML_CHUNK, ML_HEAD_DIM
    nch = n // L
    fwd = lambda width: pl.BlockSpec((bsz, L, width), lambda c: (0, c, 0))
    rev = lambda width: pl.BlockSpec((bsz, L, width), lambda c: (0, nch - 1 - c, 0))
    fwd_t = lambda height: pl.BlockSpec((bsz, height, L), lambda c: (0, 0, c))
    rev_t = lambda height: pl.BlockSpec((bsz, height, L), lambda c: (0, 0, nch - 1 - c))
    c_spec = pl.BlockSpec((bsz, 2, ML_HEADS // 2, hd + ML_AUG_ROWS, 2 * hd), lambda c: (0, 0, 0, 0, 0))
    m_spec = pl.BlockSpec((bsz, 2, ML_HEADS, V7X_SUBLANES, V7X_LANES), lambda c: (0, 0, 0, 0, 0))
    if with_output:
        args = [q, k, vt, gcol, grow, q, k, vt, gcol, grow, c0, m0]
        in_specs = ([fwd(ML_WIDTH)] * 2 + [fwd_t(ML_WIDTH), fwd(V7X_LANES), fwd_t(N_GATES)]
                    + [rev(ML_WIDTH)] * 2 + [rev_t(ML_WIDTH), rev(V7X_LANES), rev_t(N_GATES)])
        out_shape = [jax.ShapeDtypeStruct((bsz, n, ML_WIDTH), F32)] * 2
        out_specs = [fwd(ML_WIDTH), rev(ML_WIDTH)]
    else:
        args = [k, vt, grow, k, vt, grow, c0, m0]
        in_specs = ([fwd(ML_WIDTH), fwd_t(ML_WIDTH), fwd_t(N_GATES)]
                    + [rev(ML_WIDTH), rev_t(ML_WIDTH), rev_t(N_GATES)])
        out_shape = [jax.ShapeDtypeStruct(c0.shape, F32), jax.ShapeDtypeStruct(m0.shape, F32)]
        out_specs = [c_spec, m_spec]
    return pl.pallas_call(
        functools.partial(_mlstm_body, with_output=with_output),
        out_shape=out_shape,
        grid=(nch,),
        in_specs=in_specs + [c_spec, m_spec],
        out_specs=out_specs,
        scratch_shapes=[pltpu.VMEM((bsz, 2, ML_HEADS // 2, hd + ML_AUG_ROWS, 2 * hd), F32),
                        pltpu.VMEM((bsz, 2, ML_HEADS, V7X_SUBLANES, V7X_LANES), F32)],
        compiler_params=_params(1),
        name="mlstm_scan_latent" if with_output else "mlstm_scan_context",
    )(*args)


def _gelu(x):
    k2 = 2.0 * (2.0 / np.pi) ** 0.5
    z = x * (k2 + (k2 * 0.044715) * (x * x))
    return x * jax.nn.sigmoid(z)


def _sg_body(x_ref, mod_ref, g_ref, win_ref, lng_ref, lnb_ref, ws_ref, bs_ref, wout_ref, o_ref, v_scr,
             *, d, width, tm):
    x = x_ref[...]
    mod = mod_ref[...]
    h = _modulated(x, g_ref[...], mod, d).astype(BF16)
    gw = width // SG_GROUPS
    groups = [slice(g * gw, (g + 1) * gw) for g in range(SG_GROUPS)]
    order = [width + cs.start for cs in groups] + [cs.start for cs in groups]
    project = lambda k: _dot(h, win_ref[:, order[k]:order[k] + gw])
    pre = project(0)
    total = None
    for k, cs in enumerate(groups):
        nxt = project(k + 1)
        vg = _gelu(pre)
        v_scr[:, cs] = vg
        part = jnp.sum(vg, axis=-1, keepdims=True)
        total = part if total is None else total + part
        pre = nxt
    mu = total * (1.0 / width)
    sq = None
    for cs in groups:
        vc = v_scr[:, cs] - mu
        part = jnp.sum(vc * vc, axis=-1, keepdims=True)
        sq = part if sq is None else sq + part
    rstd = lax.rsqrt(sq * (1.0 / width) + EPS)
    bs = bs_ref[...]
    y = None
    for g, cs in enumerate(groups):
        nxt = project(SG_GROUPS + g + 1) if g + 1 < SG_GROUPS else None
        vn = ((v_scr[:, cs] - mu) * rstd * lng_ref[:, cs] + lnb_ref[:, cs]).astype(BF16)
        u = _gelu(pre)
        pre = nxt
        parts = []
        for c in range(tm // SG_CHUNK):
            rs = slice(c * SG_CHUNK, (c + 1) * SG_CHUNK)
            mixed = _dot(ws_ref[g], vn[rs]) + bs[:, g:g + 1]
            parts.append((u[rs] * mixed).astype(BF16))
        yg = _dot(jnp.concatenate(parts, axis=0), wout_ref[cs, :])
        y = yg if y is None else y + yg
    o_ref[...] = x + mod[:, 2 * d:] * y


def _spatial_gating(x, mod3, g, w_in, ln_g, ln_b, w_s, b_s, w_out, tm=1024):
    bsz, n, d = x.shape
    width = w_out.shape[0]
    tm = min(tm, n)
    w_in, w_s, w_out = w_in.astype(BF16), w_s.astype(BF16), w_out.astype(BF16)
    const = lambda shape: pl.BlockSpec(shape, lambda b, i: (0,) * len(shape), pipeline_mode=pl.Buffered(1))
    return pl.pallas_call(
        functools.partial(_sg_body, d=d, width=width, tm=tm),
        out_shape=jax.ShapeDtypeStruct(x.shape, F32),
        grid=(bsz, n // tm),
        in_specs=[
            pl.BlockSpec((None, tm, d), lambda b, i: (b, i, 0)),
            pl.BlockSpec((None, 1, 3 * d), lambda b, i: (b, 0, 0)),
            const((1, d)), const(w_in.shape), const((1, width)), const((1, width)),
            const(w_s.shape), const((SG_CHUNK, SG_GROUPS)), const(w_out.shape),
        ],
        out_specs=pl.BlockSpec((None, tm, d), lambda b, i: (b, i, 0)),
        scratch_shapes=[pltpu.VMEM((tm, width), F32)],
        compiler_params=_params(2),
        name="spatial_gating",
    )(x, mod3, g.reshape(1, d), w_in, ln_g.reshape(1, width), ln_b.reshape(1, width), w_s, b_s.T, w_out)


def _rope_tables(n):
    hd = ML_HEAD_DIM
    n_pairs = hd // 4
    inv_freq = ROPE_THETA ** (-jnp.arange(n_pairs, dtype=F32) / n_pairs)
    row_ang = jnp.arange(n // GRID_W, dtype=F32)[:, None] * inv_freq
    col_ang = jnp.arange(GRID_W, dtype=F32)[:, None] * inv_freq

    def table(ang, is_row):
        zero = jnp.zeros_like(ang)
        cos, sin = jnp.cos(ang), jnp.sin(ang)
        half = lambda t: jnp.concatenate([t, zero] if is_row else [zero, t], axis=-1)
        return jnp.stack([jnp.concatenate([half(cos), half(cos)], axis=-1),
                          jnp.concatenate([half(-sin), half(sin)], axis=-1)])

    return table(row_ang, True), table(col_ang, False)


def _even_weights(w_in, gate_b):
    d = w_in.shape[0]
    hd = ML_HEAD_DIM
    base = 3 * NA_WIDTH
    deint = lambda w: w.reshape(d, ML_HEADS, hd // 2, 2).transpose(0, 1, 3, 2).reshape(d, ML_WIDTH)
    qb = deint(w_in[:, base:base + ML_WIDTH])
    kb = deint(w_in[:, base + ML_WIDTH:base + 2 * ML_WIDTH])
    w_main = jnp.concatenate([w_in[:, :base], qb, kb, w_in[:, base + 3 * ML_WIDTH:base + 4 * ML_WIDTH]],
                             axis=1).astype(BF16)
    w_vt = w_in[:, base + 2 * ML_WIDTH:base + 3 * ML_WIDTH].T.astype(BF16)
    wg = w_in[:, base + 4 * ML_WIDTH:]
    w_gc = jnp.pad(wg, ((0, 0), (0, V7X_LANES - N_GATES))).astype(BF16)
    w_gr = wg.T.astype(BF16)
    gb = gate_b.reshape(N_GATES).astype(F32)
    gb_col = jnp.pad(gb, (0, V7X_LANES - N_GATES)).reshape(1, V7X_LANES)
    gb_row = gb.reshape(N_GATES, 1)
    return w_main, w_vt, w_gc, w_gr, gb_col, gb_row


def kernel(x, c, ctx, c_ctx, w_mod, b_mod, norm_g, ffn_w_in, ffn_w_out, mix_w_in, na_rpb, ml_gate_b, ml_head_g,
           mix_w_out, sg_w_in, sg_ln_g, sg_ln_b, sg_w_s, sg_b_s, sg_w_out, final_g):
    bsz, n, d = x.shape
    depth = w_mod.shape[0]
    ctx_row = bsz
    cvec = jnp.zeros((V7X_SUBLANES, d), F32).at[:bsz].set(c).at[ctx_row].set(c_ctx)
    mod = _mod_vectors(cvec, w_mod, b_mod).reshape(depth, V7X_SUBLANES, 3, 1, 3 * d)
    batch_row = lambda b: b
    context_row = lambda b: ctx_row
    last_ctx_layer = ((depth - 1) // 2) * 2
    ffn_w_in = ffn_w_in.astype(BF16)
    ffn_w_out = ffn_w_out.astype(BF16)
    xc = ctx
    for l in range(depth):
        ctx_in = l <= last_ctx_layer
        ctx_out = l < last_ctx_layer
        x = _ffn(x, mod[l, :, 0], batch_row, norm_g[l, 0], ffn_w_in, ffn_w_out, l, 0)
        if ctx_in:
            xc = _ffn(xc, mod[l, :, 0], context_row, norm_g[l, 0], ffn_w_in, ffn_w_out, l, 0)
        if l % 2 == 0:
            e = l // 2
            weights = _even_weights(mix_w_in[e], ml_gate_b[e])
            qa, ka, va, qb, kb, vt, ob, gcol, grow = _even_projection(
                x, mod[l, :, 1], batch_row, norm_g[l, 1], weights, _rope_tables(n))
            _, kax, vax, _, kbx, vtx, _, _, growx = _even_projection(
                xc, mod[l, :, 1], context_row, norm_g[l, 1], weights, None)
            bias = _na_bias_tables(na_rpb[e], n // GRID_W)
            ya = _neighbourhood_attention(qa, ka, va, kax, vax, bias)
            c0 = jnp.zeros((bsz, 2, ML_HEADS // 2, ML_HEAD_DIM + ML_AUG_ROWS, 2 * ML_HEAD_DIM), F32)
            m0 = jnp.zeros((bsz, 2, ML_HEADS, V7X_SUBLANES, V7X_LANES), F32)
            c1, m1 = _mlstm_scan(None, kbx, vtx, None, growx, c0, m0, False)
            hf, hr = _mlstm_scan(qb, kb, vt, gcol, grow, c1, m1, True)
            mixer = (mod[l, :, 1], ya, hf, hr, ob, ml_head_g[e], mix_w_out[e].astype(BF16))
            assert not ctx_out, "context output path is not needed for this depth"
        else:
            o = l // 2
            mixer = None
            x = _spatial_gating(x, mod[l, :, 1], norm_g[l, 1], sg_w_in[o], sg_ln_g[o], sg_ln_b[o],
                                sg_w_s[o], sg_b_s[o], sg_w_out[o])
        fg = final_g if l == depth - 1 else None
        x = _ffn(x, mod[l, :, 2], batch_row, norm_g[l, 2], ffn_w_in, ffn_w_out, l, 1, final_g=fg, mixer=mixer)
    return x
```

```python
import functools

import numpy as np
import jax
import jax.numpy as jnp
from jax import lax
from jax.experimental import pallas as pl
from jax.experimental.pallas import tpu as pltpu

GRID_W = 64
NA_HEADS = 8
NA_HEAD_DIM = 64
NA_KH = 8
NA_KW = 16
ML_HEADS = 4
ML_HEAD_DIM = 128
ML_CHUNK = 128
ROPE_THETA = 10000.0
SG_CHUNK = 128
SG_GROUPS = 8
EPS = 1e-6
NA_WIDTH = NA_HEADS * NA_HEAD_DIM
ML_WIDTH = ML_HEADS * ML_HEAD_DIM
N_GATES = 4 * ML_HEADS
ML_AUG_ROWS = 16

V7X_LANES = 128
V7X_SUBLANES = 8
V7X_MXU_COLUMNS = 256
V7X_VMEM_LIMIT_BYTES = 56 * 1024 * 1024

NEG_BIG = -1e30
FFN_TOKENS_PER_STEP = 1024
V7X_BF16_SUBLANES = 16
NA_ROWS_PER_STEP = 16
NA_SUB_ROWS = 4
NA_BAND_ROWS = 12

BF16 = jnp.bfloat16
F32 = jnp.float32


def _dot(a, b):
    return jnp.dot(a, b, preferred_element_type=F32)


def _dot_nt(a, b):
    return lax.dot_general(a, b, (((1,), (1,)), ((), ())), preferred_element_type=F32)


def _params(n_axes):
    return pltpu.CompilerParams(
        dimension_semantics=("arbitrary",) * n_axes,
        vmem_limit_bytes=V7X_VMEM_LIMIT_BYTES,
    )


def _modulated(x, g, mod, d):
    shift = mod[:, :d]
    scale = mod[:, d:2 * d]
    gs = g * (1.0 + scale)
    ms = jnp.mean(x * x, axis=-1, keepdims=True)
    return x * lax.rsqrt(ms + EPS) * gs + shift


def _mod_body(c_ref, w_ref, b_ref, o_ref):
    c = c_ref[...]
    s = (c * jax.nn.sigmoid(c)).astype(BF16)
    o_ref[...] = _dot(s, w_ref[...].astype(BF16)) + b_ref[...]


def _mod_vectors(cvec, w_mod, b_mod):
    depth, d, width = w_mod.shape
    tn = width // 8
    return pl.pallas_call(
        _mod_body,
        out_shape=jax.ShapeDtypeStruct((depth, V7X_SUBLANES, width), F32),
        grid=(depth, width // tn),
        in_specs=[
            pl.BlockSpec((V7X_SUBLANES, d), lambda l, j: (0, 0)),
            pl.BlockSpec((None, d, tn), lambda l, j: (l, 0, j)),
            pl.BlockSpec((None, 1, tn), lambda l, j: (l, 0, j)),
        ],
        out_specs=pl.BlockSpec((None, V7X_SUBLANES, tn), lambda l, j: (l, 0, j)),
        compiler_params=_params(2),
        name="mod_vectors",
    )(cvec, w_mod, b_mod.reshape(depth, 1, width))


def _mixer_output(ya_ref, hf_ref, hr_ref, ob_ref, hg_ref, w_ref):
    hd = ML_HEAD_DIM
    hb = hf_ref[...] + hr_ref[...]
    ob = ob_ref[...]
    hg = hg_ref[...]
    y = _dot(ya_ref[...], w_ref[0:NA_WIDTH, :])
    ybs = []
    for hh in range(ML_HEADS):
        sl = slice(hh * hd, (hh + 1) * hd)
        hs = hb[:, sl]
        ms = jnp.mean(hs * hs, axis=-1, keepdims=True)
        ybs.append((hs * lax.rsqrt(ms + EPS) * hg[:, sl] * jax.nn.sigmoid(ob[:, sl])).astype(BF16))
    return y + _dot(jnp.concatenate(ybs, axis=1), w_ref[NA_WIDTH:, :])


def _ffn_body(*refs, d, d_ff, chunk, final, mixer, cast_next):
    refs = list(refs)
    if cast_next:
        next_in_ref, next_out_ref, o_ref, cast_in_ref, cast_out_ref = refs[-5:]
        cast_in_ref[...] = next_in_ref[...].astype(BF16)
        cast_out_ref[...] = next_out_ref[...].astype(BF16)
        refs = refs[:-5]
    else:
        o_ref = refs.pop()
    x_ref, mod_ref, g_ref, win_ref, wout_ref = refs[:5]
    rest = refs[5:]
    x = x_ref[...]
    if mixer:
        mixmod_ref = rest[0]
        x = x + mixmod_ref[...][:, 2 * d:] * _mixer_output(*rest[1:7])
        rest = rest[7:]
    if final:
        (fg_ref,) = rest
    mod = mod_ref[...]
    h = _modulated(x, g_ref[...], mod, d).astype(BF16)
    acc = None
    for c in range(d_ff // chunk):
        a = _dot(h, win_ref[:, c * chunk:(c + 1) * chunk])
        b = _dot(h, win_ref[:, d_ff + c * chunk:d_ff + (c + 1) * chunk])
        t = (a * jax.nn.sigmoid(a) * b).astype(BF16)
        y = _dot(t, wout_ref[c * chunk:(c + 1) * chunk, :])
        acc = y if acc is None else acc + y
    out = x + (0.5 * mod[:, 2 * d:]) * acc
    if final:
        ms = jnp.mean(out * out, axis=-1, keepdims=True)
        out = out * lax.rsqrt(ms + EPS) * fg_ref[...]
    o_ref[...] = out


def _ffn(x, mod3, mod_row, g, w_in, w_out, final_g=None, mixer=None, cast_next=None, tm=None):
    bsz, n, d = x.shape
    d_ff = w_out.shape[-2]
    if tm is None:
        tm = FFN_TOKENS_PER_STEP // 2 if mixer is not None else FFN_TOKENS_PER_STEP
    tm = min(tm, n)
    steps = n // tm
    final = final_g is not None
    tok_spec = lambda width: pl.BlockSpec((None, tm, width), lambda b, i: (b, i, 0))
    resident = dict(pipeline_mode=pl.Buffered(1))
    in_specs = [
        tok_spec(d),
        pl.BlockSpec((None, 1, 3 * d), lambda b, i: (mod_row(b), 0, 0)),
        pl.BlockSpec((1, d), lambda b, i: (0, 0)),
        pl.BlockSpec((d, 2 * d_ff), lambda b, i: (0, 0), **resident),
        pl.BlockSpec((None,) * (w_out.ndim - 2) + (d_ff, d), lambda b, i: (0,) * w_out.ndim, **resident),
    ]
    args = [x, mod3, g.reshape(1, d), w_in, w_out]
    if mixer is not None:
        mix_mod3, ya, hf, hr, ob, head_g, w_mix = mixer
        in_specs += [
            pl.BlockSpec((None, 1, 3 * d), lambda b, i: (mod_row(b), 0, 0)),
            tok_spec(NA_WIDTH), tok_spec(ML_WIDTH), tok_spec(ML_WIDTH), tok_spec(ML_WIDTH),
            pl.BlockSpec((1, ML_WIDTH), lambda b, i: (0, 0)),
            pl.BlockSpec(w_mix.shape, lambda b, i: (0, 0), **resident),
        ]
        args += [mix_mod3, ya, hf, hr, ob, head_g.reshape(1, ML_WIDTH), w_mix]
    if final:
        in_specs.append(pl.BlockSpec((1, d), lambda b, i: (0, 0)))
        args.append(final_g.reshape(1, d))
    out_shape = [jax.ShapeDtypeStruct(x.shape, F32)]
    out_specs = [tok_spec(d)]
    if cast_next is not None:
        w_in_all, w_out_all, layer, half = cast_next
        rows_in = d // (bsz * steps)
        assert rows_in * bsz * steps == d and rows_in % V7X_BF16_SUBLANES == 0, (d, bsz, steps)
        tiles = d_ff // V7X_BF16_SUBLANES
        assert tiles * V7X_BF16_SUBLANES == d_ff, d_ff
        blocks = max(k for k in range(1, steps + 1) if tiles % k == 0)
        rows_out = d_ff // blocks
        in_row = lambda b, i: b * steps + i
        out_row = lambda b, i: jnp.minimum(i, blocks - 1)
        in_specs += [
            pl.BlockSpec((None, None, rows_in, 2 * d_ff), lambda b, i: (layer, half, in_row(b, i), 0)),
            pl.BlockSpec((None, None, rows_out, d), lambda b, i: (layer, half, out_row(b, i), 0)),
        ]
        args += [w_in_all, w_out_all]
        out_shape += [jax.ShapeDtypeStruct((d, 2 * d_ff), BF16), jax.ShapeDtypeStruct((bsz, d_ff, d), BF16)]
        out_specs += [pl.BlockSpec((rows_in, 2 * d_ff), lambda b, i: (in_row(b, i), 0)),
                      pl.BlockSpec((None, rows_out, d), lambda b, i: (b, out_row(b, i), 0))]
    outs = pl.pallas_call(
        functools.partial(_ffn_body, d=d, d_ff=d_ff, chunk=V7X_MXU_COLUMNS, final=final,
                          mixer=mixer is not None, cast_next=cast_next is not None),
        out_shape=out_shape,
        grid=(bsz, steps),
        in_specs=in_specs,
        out_specs=out_specs,
        compiler_params=_params(2),
        name="macaron_ffn",
    )(*args)
    return (outs[0], (outs[1], outs[2])) if cast_next is not None else outs[0]


def _log_gates(z, is_forget):
    ls = jnp.minimum(z, 0.0) - jnp.log1p(jnp.exp(-jnp.abs(z)))
    return jnp.where(is_forget, ls, z)


def _split3(x):
    hi = x.astype(BF16)
    r1 = x - hi.astype(F32)
    mid = r1.astype(BF16)
    lo = (r1 - mid.astype(F32)).astype(BF16)
    return [hi, mid, lo]


def _evenproj_body(x_ref, mod_ref, g_ref, w_ref, wvt_ref, wg_ref, wgt_ref, gbc_ref, gbr_ref, rowt_ref, colt_ref,
                   qa_ref, ka_ref, va_ref, qb_ref, kb_ref, vt_ref, ob_ref, gc_ref, gr_ref, *, d, rope):
    h = _modulated(x_ref[...], g_ref[...], mod_ref[...], d).astype(BF16)
    gates = _gate_stages(h, wg_ref, wgt_ref, gbc_ref, gbr_ref, gc_ref, gr_ref)
    nw, mw, hd = NA_WIDTH, ML_WIDTH, ML_HEAD_DIM
    next(gates, None)
    qa_ref[...] = (_dot(h, w_ref[:, 0:nw]) * (NA_HEAD_DIM ** -0.5)).astype(BF16)
    next(gates, None)
    ka_ref[...] = _dot(h, w_ref[:, nw:2 * nw]).astype(BF16)
    next(gates, None)
    va_ref[...] = _dot(h, w_ref[:, 2 * nw:3 * nw]).astype(BF16)
    next(gates, None)
    base = 3 * nw
    qb = _dot(h, w_ref[:, base:base + mw])
    next(gates, None)
    kb = _dot(h, w_ref[:, base + mw:base + 2 * mw])
    next(gates, None)
    if rope:
        tm = qb.shape[0]
        expand = lambda t: jnp.broadcast_to(t[:, None, :], (tm // GRID_W, GRID_W, hd)).reshape(tm, hd)
        cos = expand(rowt_ref[0]) + colt_ref[0]
        sin = expand(rowt_ref[1]) + colt_ref[1]
        for hh in range(ML_HEADS):
            sl = slice(hh * hd, (hh + 1) * hd)
            qh = qb[:, sl]
            kh = kb[:, sl]
            qb_ref[:, sl] = (qh * cos + pltpu.roll(qh, hd // 2, 1) * sin).astype(BF16)
            kb_ref[:, sl] = ((kh * cos + pltpu.roll(kh, hd // 2, 1) * sin) * (hd ** -0.5)).astype(BF16)
    else:
        qb_ref[...] = qb.astype(BF16)
        kb_ref[...] = (kb * (hd ** -0.5)).astype(BF16)
    vt_ref[...] = _dot_nt(wvt_ref[...], h).astype(BF16)
    next(gates, None)
    ob_ref[...] = _dot(h, w_ref[:, base + 2 * mw:base + 3 * mw])
    for _ in gates:
        pass


def _gate_stages(h, wg_ref, wgt_ref, gbc_ref, gbr_ref, gc_ref, gr_ref):
    L = ML_CHUNK
    tm = h.shape[0]
    r_idx = lax.broadcasted_iota(jnp.int32, (L, L), 0)
    c_idx = lax.broadcasted_iota(jnp.int32, (L, L), 1)
    lower = (c_idx <= r_idx).astype(BF16)
    upper = (r_idx <= c_idx).astype(BF16)
    zc = _dot(h, wg_ref[...]) + gbc_ref[...]
    zr = _dot_nt(wgt_ref[...], h) + gbr_ref[...]
    yield
    lane = lax.broadcasted_iota(jnp.int32, zc.shape, 1)
    gates_c = _log_gates(zc, (lane % 2) == 1)
    split_c = jnp.concatenate(_split3(gates_c), axis=1)
    row = lax.broadcasted_iota(jnp.int32, zr.shape, 0)
    gates_r = _log_gates(zr, (row % 2) == 1)
    split_r = jnp.concatenate(_split3(gates_r), axis=0)
    yield
    bwd_lane = ((lax.broadcasted_iota(jnp.int32, (L, V7X_LANES), 1) // 2) % 2) == 1
    for c in range(tm // L):
        rs = slice(c * L, (c + 1) * L)
        r = _dot(lower, split_c[rs])
        prefix = r[:, 0:V7X_LANES] + r[:, V7X_LANES:2 * V7X_LANES] + r[:, 2 * V7X_LANES:]
        g = gates_c[rs]
        b = jnp.where(bwd_lane, prefix[L - 1:L, :] - prefix + g, prefix)
        gc_ref[rs, :] = g - pltpu.roll(b, V7X_LANES - 1, 1)
        if c % 2 == 1:
            yield
    row_l = lax.broadcasted_iota(jnp.int32, (N_GATES, L), 0)
    bwd_row = ((row_l // 2) % 2) == 1
    f_row = (row_l % 2) == 1
    for c in range(tm // L):
        ls = slice(c * L, (c + 1) * L)
        r = _dot(split_r[:, ls], upper)
        prefix = r[0:N_GATES] + r[N_GATES:2 * N_GATES] + r[2 * N_GATES:]
        g = gates_r[:, ls]
        b = jnp.where(bwd_row, prefix[:, L - 1:L] - prefix + g, prefix)
        gr_ref[:, ls] = jnp.where(f_row, b, g - pltpu.roll(b, N_GATES - 1, 0))
        if c % 2 == 1:
            yield


def _even_projection(x, mod3, mod_row, g, weights, rope_tables, tm=1024):
    w_main, w_vt, w_gc, w_gr, gb_col, gb_row = weights
    bsz, n, d = x.shape
    tm = min(tm, n)
    rope = rope_tables is not None
    if rope:
        row_t, col_t = rope_tables
        col_t = jnp.tile(col_t, (1, tm // GRID_W, 1))
    else:
        row_t = jnp.zeros((2, tm // GRID_W, ML_HEAD_DIM), F32)
        col_t = jnp.zeros((2, tm, ML_HEAD_DIM), F32)
    wm = w_main.shape[1]
    tok = lambda width, dt: jax.ShapeDtypeStruct((bsz, n, width), dt)
    tok_spec = lambda width: pl.BlockSpec((None, tm, width), lambda b, i: (b, i, 0))
    const = lambda shape: pl.BlockSpec(shape, lambda b, i: (0,) * len(shape))
    out_shape = [tok(NA_WIDTH, BF16)] * 3 + [tok(ML_WIDTH, BF16)] * 2 + [
        jax.ShapeDtypeStruct((bsz, ML_WIDTH, n), BF16), tok(ML_WIDTH, F32),
        tok(V7X_LANES, F32), jax.ShapeDtypeStruct((bsz, N_GATES, n), F32)]
    out_specs = [tok_spec(NA_WIDTH)] * 3 + [tok_spec(ML_WIDTH)] * 2 + [
        pl.BlockSpec((None, ML_WIDTH, tm), lambda b, i: (b, 0, i)), tok_spec(ML_WIDTH),
        tok_spec(V7X_LANES), pl.BlockSpec((None, N_GATES, tm), lambda b, i: (b, 0, i))]
    return pl.pallas_call(
        functools.partial(_evenproj_body, d=d, rope=rope),
        out_shape=out_shape,
        grid=(bsz, n // tm),
        in_specs=[
            tok_spec(d),
            pl.BlockSpec((None, 1, 3 * d), lambda b, i: (mod_row(b), 0, 0)),
            const((1, d)),
            const((d, wm)),
            const((ML_WIDTH, d)),
            const((d, V7X_LANES)),
            const((N_GATES, d)),
            const((1, V7X_LANES)),
            const((N_GATES, 1)),
            pl.BlockSpec((2, tm // GRID_W, ML_HEAD_DIM), lambda b, i: (0, i if rope else 0, 0)),
            pl.BlockSpec((2, tm, ML_HEAD_DIM), lambda b, i: (0, 0, 0)),
        ],
        out_specs=out_specs,
        compiler_params=_params(2),
        name="even_projection",
    )(x, mod3, g.reshape(1, d), w_main, w_vt, w_gc, w_gr, gb_col, gb_row, row_t, col_t)


def _na_body(q_ref, k_ref, v_ref, kx_ref, vx_ref, *rest, rows):
    rb = pl.program_id(2)
    sub, band = NA_SUB_ROWS, NA_BAND_ROWS
    n_sub = NA_ROWS_PER_STEP // sub
    tq = sub * GRID_W
    bias_refs, o_ref = rest[:n_sub], rest[n_sub]
    kx = kx_ref[...]
    vx = vx_ref[...]
    q = q_ref[...]
    lane = lax.broadcasted_iota(jnp.int32, q.shape, 1)
    bands = []
    for u in range(n_sub):
        band0 = jnp.clip((rb * n_sub + u) * sub - NA_KH // 2, 0, rows - band)
        start = pl.multiple_of(band0 * GRID_W, GRID_W)
        bands.append((k_ref[pl.ds(start, band * GRID_W), :], v_ref[pl.ds(start, band * GRID_W), :]))
    qms = [jnp.where((lane // NA_HEAD_DIM) == hh, q, jnp.zeros_like(q)) for hh in range(2)]
    items = [(hh, u) for hh in range(2) for u in range(n_sub)]

    def scores(item):
        hh, u = item
        qm = qms[hh][u * tq:(u + 1) * tq]
        return _dot_nt(qm, bands[u][0]) + bias_refs[u][hh], _dot_nt(qm, kx)

    def attend(item, s_loc, s_ctx):
        u = item[1]
        m = jnp.maximum(jnp.max(s_loc, axis=-1, keepdims=True), jnp.max(s_ctx, axis=-1, keepdims=True))
        p_loc = jnp.exp(s_loc - m)
        p_ctx = jnp.exp(s_ctx - m)
        denom = jnp.sum(p_loc, axis=-1, keepdims=True) + jnp.sum(p_ctx, axis=-1, keepdims=True)
        o = _dot(p_loc.astype(BF16), bands[u][1]) + _dot(p_ctx.astype(BF16), vx)
        return o * (1.0 / denom)

    outs = {}
    ready = scores(items[0])
    for k, item in enumerate(items):
        nxt = scores(items[k + 1]) if k + 1 < len(items) else None
        outs[item] = attend(item, *ready)
        ready = nxt
    per_head = [jnp.concatenate([outs[(hh, u)] for u in range(n_sub)], axis=0) for hh in range(2)]
    lane_o = lax.broadcasted_iota(jnp.int32, per_head[0].shape, 1)
    o_ref[...] = jnp.where(lane_o < NA_HEAD_DIM, per_head[0], per_head[1]).astype(BF16)


def _na_bias_tables(rpb, rows):
    sub, band, w = NA_SUB_ROWS, NA_BAND_ROWS, GRID_W
    kh, kw = NA_KH, NA_KW
    n_heads = rpb.shape[0]
    cols = np.arange(w)
    c0 = np.clip(cols - kw // 2, 0, w - kw)
    cc = np.arange(w)[None, :]
    col_ok = (cc >= c0[:, None]) & (cc < c0[:, None] + kw)
    padded = jnp.pad(rpb, ((0, 0), (0, 0), (w, w)))
    shifted = jnp.stack([padded[:, :, w + kw - 1 - j:2 * w + kw - 1 - j] for j in range(w)], axis=2)
    t1 = jnp.where(col_ok[None, None], shifted, NEG_BIG)
    masked = jnp.full((n_heads, w, w), NEG_BIG, F32)
    variants = []
    for i0, b0 in ((0, 0), (sub, sub - kh // 2), (rows - sub, rows - band)):
        blocks = []
        for qi in range(i0, i0 + sub):
            r0 = min(max(qi - kh // 2, 0), rows - kh)
            row = [t1[:, r - qi + kh - 1] if r0 <= r < r0 + kh else masked for r in range(b0, b0 + band)]
            blocks.append(jnp.concatenate(row, axis=-1))
        variants.append(jnp.concatenate(blocks, axis=1))
    return jnp.stack(variants)


def _neighbourhood_attention(q, k, v, kx, vx, bias):
    bsz, n, _ = q.shape
    lc = kx.shape[1]
    rows = n // GRID_W
    sub, band = NA_SUB_ROWS, NA_BAND_ROWS
    n_sub = NA_ROWS_PER_STEP // sub
    nrb = rows // NA_ROWS_PER_STEP
    tq = NA_ROWS_PER_STEP * GRID_W
    pair = 2 * NA_HEAD_DIM
    last = rows // sub - 1

    def bias_spec(u):
        def index(b, hp, rb):
            sb = rb * n_sub + u
            return (jnp.where(sb == 0, 0, jnp.where(sb == last, 2, 1)), hp, 0, 0)
        return pl.BlockSpec((None, 2, sub * GRID_W, band * GRID_W), index)

    return pl.pallas_call(
        functools.partial(_na_body, rows=rows),
        out_shape=jax.ShapeDtypeStruct((bsz, n, NA_WIDTH), BF16),
        grid=(bsz, NA_HEADS // 2, nrb),
        in_specs=[
            pl.BlockSpec((None, tq, pair), lambda b, hp, rb: (b, rb, hp)),
            pl.BlockSpec((None, n, pair), lambda b, hp, rb: (b, 0, hp)),
            pl.BlockSpec((None, n, pair), lambda b, hp, rb: (b, 0, hp)),
            pl.BlockSpec((None, lc, pair), lambda b, hp, rb: (b, 0, hp)),
            pl.BlockSpec((None, lc, pair), lambda b, hp, rb: (b, 0, hp)),
        ] + [bias_spec(u) for u in range(n_sub)],
        out_specs=pl.BlockSpec((None, tq, pair), lambda b, hp, rb: (b, rb, hp)),
        compiler_params=_params(3),
        name="neighbourhood_attention",
    )(q, k, v, kx, vx, *([bias] * n_sub))


def _mlstm_body(*refs, with_output):
    if with_output:
        (qf_ref, kf_ref, vtf_ref, gcf_ref, grf_ref, qr_ref, kr_ref, vtr_ref, gcr_ref, grr_ref,
         c0_ref, m0_ref, hf_ref, hr_ref, c_scr, m_scr) = refs
    else:
        (kf_ref, vtf_ref, grf_ref, kr_ref, vtr_ref, grr_ref,
         c0_ref, m0_ref, c_out_ref, m_out_ref, c_scr, m_scr) = refs
    step = pl.program_id(0)
    n_steps = pl.num_programs(0)
    L, hd = ML_CHUNK, ML_HEAD_DIM
    bsz = kf_ref.shape[0]

    @pl.when(step == 0)
    def _():
        c_scr[...] = c0_ref[...]
        m_scr[...] = m0_ref[...]

    s_idx = lax.broadcasted_iota(jnp.int32, (L, L), 0)
    t_idx = lax.broadcasted_iota(jnp.int32, (L, L), 1)
    ones_rows = (lax.broadcasted_iota(jnp.int32, (ML_AUG_ROWS, L), 0) == 0).astype(BF16)
    fwd_refs = (kf_ref, vtf_ref, grf_ref) + ((qf_ref, gcf_ref, hf_ref) if with_output else (None,) * 3)
    bwd_refs = (kr_ref, vtr_ref, grr_ref) + ((qr_ref, gcr_ref, hr_ref) if with_output else (None,) * 3)
    items = [(bi, d, p) for bi in range(bsz) for d in range(2) for p in range(ML_HEADS // 2)]
    lane2 = lax.broadcasted_iota(jnp.int32, (L, 2 * hd), 1)
    zero_ll = jnp.zeros((L, L), BF16)

    def refs_of(item):
        bi, d, _ = item
        return tuple(r if r is None else r.at[bi] for r in (fwd_refs if d == 0 else bwd_refs))

    def block_diag(x2):
        zero = jnp.zeros_like(x2)
        return jnp.concatenate([jnp.where(lane2 < hd, x2, zero), jnp.where(lane2 >= hd, x2, zero)], axis=0)

    def front(item):
        bi, d, p = item
        k_ref, _, _, q_ref, _, _ = refs_of(item)
        sl2 = slice(2 * p * hd, (2 * p + 2) * hd)
        qbd = block_diag(q_ref[:, sl2])
        return _dot_nt(k_ref[:, sl2], qbd), _dot_nt(c_scr[bi, d, p].astype(BF16), qbd)

    def finish(item, fronts):
        bi, d, p = item
        k_ref, vt_ref, gr_ref, _, gc_ref, h_ref = refs_of(item)
        visible = (s_idx <= t_idx) if d == 0 else (s_idx >= t_idx)
        end = L - 1 if d == 0 else 0
        sl2 = slice(2 * p * hd, (2 * p + 2) * hd)
        a_prevs, a_toks, vt_augs, ps, w_inters, w_corrs, inv_floors = [], [], [], [], [], [], []
        for j in range(2):
            hh = 2 * p + j
            gi = hh * 4 + d * 2
            c_row = gr_ref[gi:gi + 1, :]
            b_row = gr_ref[gi + 1:gi + 2, :]
            b_end = b_row[:, end:end + 1]
            m_old = m_scr[bi, d, hh][0:1, 0:1]
            m_new = jnp.maximum(b_end + m_old, b_end + jnp.max(c_row, axis=-1, keepdims=True))
            a_prevs.append(jnp.broadcast_to(jnp.exp(b_end + m_old - m_new), (1, hd)))
            a_toks.append(jnp.exp(b_end + c_row - m_new))
            vt_augs.append(jnp.concatenate([vt_ref[hh * hd:(hh + 1) * hd, :], ones_rows], axis=0))
            m_scr[bi, d, hh] = jnp.broadcast_to(m_new, (V7X_SUBLANES, V7X_LANES))
            if with_output:
                st = fronts[0][:, j * L:(j + 1) * L]
                log_w = jnp.where(visible, b_row + gc_ref[:, gi:gi + 1], NEG_BIG)
                m_intra = jnp.max(log_w, axis=0, keepdims=True)
                ps.append((st * jnp.exp(log_w - m_intra)).astype(BF16))
                log_inter = b_row + m_old
                m_t = jnp.maximum(log_inter, m_intra)
                w_inters.append(jnp.exp(log_inter - m_t))
                w_corrs.append(jnp.exp(m_intra - m_t))
                inv_floors.append(jnp.exp(-m_t))
        cat = lambda parts: jnp.concatenate(parts, axis=1)
        vt2 = cat(vt_augs)
        c_old = c_scr[bi, d, p]
        if with_output:
            pbd = jnp.concatenate([cat([ps[0], zero_ll]), cat([zero_ll, ps[1]])], axis=0)
            intra = _dot(vt2, pbd)
            both = cat(w_inters) * fronts[1] + cat(w_corrs) * intra
            den = both[hd:hd + 1, :]
            ht = both[:hd, :] * (1.0 / jnp.maximum(jnp.abs(den), cat(inv_floors)))
            for j in range(2):
                h_ref[:, (2 * p + j) * hd:(2 * p + j + 1) * hd] = ht[:, j * L:(j + 1) * L].T
        av = (vt2.astype(F32) * cat(a_toks)).astype(BF16)
        c_scr[bi, d, p] = cat(a_prevs) * c_old + _dot(av, block_diag(k_ref[:, sl2]))

    ready = front(items[0]) if with_output else None
    for i, item in enumerate(items):
        nxt = front(items[i + 1]) if with_output and i + 1 < len(items) else None
        finish(item, ready)
        ready = nxt

    if not with_output:
        @pl.when(step == n_steps - 1)
        def _():
            c_out_ref[...] = c_scr[...]
            m_out_ref[...] = m_scr[...]


def _mlstm_scan(q, k, vt, gcol, grow, c0, m0, with_output):
    bsz, n, _ = k.shape
    L, hd = ML_CHUNK, ML_HEAD_DIM
    nch = n // L
    fwd = lambda width: pl.BlockSpec((bsz, L, width), lambda c: (0, c, 0))
    rev = lambda width: pl.BlockSpec((bsz, L, width), lambda c: (0, nch - 1 - c, 0))
    fwd_t = lambda height: pl.BlockSpec((bsz, height, L), lambda c: (0, 0, c))
    rev_t = lambda height: pl.BlockSpec((bsz, height, L), lambda c: (0, 0, nch - 1 - c))
    c_spec = pl.BlockSpec((bsz, 2, ML_HEADS // 2, hd + ML_AUG_ROWS, 2 * hd), lambda c: (0, 0, 0, 0, 0))
    m_spec = pl.BlockSpec((bsz, 2, ML_HEADS, V7X_SUBLANES, V7X_LANES), lambda c: (0, 0, 0, 0, 0))
    if with_output:
        args = [q, k, vt, gcol, grow, q, k, vt, gcol, grow, c0, m0]
        in_specs = ([fwd(ML_WIDTH)] * 2 + [fwd_t(ML_WIDTH), fwd(V7X_LANES), fwd_t(N_GATES)]
                    + [rev(ML_WIDTH)] * 2 + [rev_t(ML_WIDTH), rev(V7X_LANES), rev_t(N_GATES)])
        out_shape = [jax.ShapeDtypeStruct((bsz, n, ML_WIDTH), F32)] * 2
        out_specs = [fwd(ML_WIDTH), rev(ML_WIDTH)]
    else:
        args = [k, vt, grow, k, vt, grow, c0, m0]
        in_specs = ([fwd(ML_WIDTH), fwd_t(ML_WIDTH), fwd_t(N_GATES)]
                    + [rev(ML_WIDTH), rev_t(ML_WIDTH), rev_t(N_GATES)])
        out_shape = [jax.ShapeDtypeStruct(c0.shape, F32), jax.ShapeDtypeStruct(m0.shape, F32)]
        out_specs = [c_spec, m_spec]
    return pl.pallas_call(
        functools.partial(_mlstm_body, with_output=with_output),
        out_shape=out_shape,
        grid=(nch,),
        in_specs=in_specs + [c_spec, m_spec],
        out_specs=out_specs,
        scratch_shapes=[pltpu.VMEM((bsz, 2, ML_HEADS // 2, hd + ML_AUG_ROWS, 2 * hd), F32),
                        pltpu.VMEM((bsz, 2, ML_HEADS, V7X_SUBLANES, V7X_LANES), F32)],
        compiler_params=_params(1),
        name="mlstm_scan_latent" if with_output else "mlstm_scan_context",
    )(*args)


def _gelu(x):
    k2 = 2.0 * (2.0 / np.pi) ** 0.5
    z = x * (k2 + (k2 * 0.044715) * (x * x))
    return x * jax.nn.sigmoid(z)


def _sg_body(x_ref, mod_ref, g_ref, win_ref, lng_ref, lnb_ref, ws_ref, bs_ref, wout_ref, o_ref, v_scr,
             *, d, width, tm):
    x = x_ref[...]
    mod = mod_ref[...]
    h = _modulated(x, g_ref[...], mod, d).astype(BF16)
    gw = width // SG_GROUPS
    groups = [slice(g * gw, (g + 1) * gw) for g in range(SG_GROUPS)]
    order = [width + cs.start for cs in groups] + [cs.start for cs in groups]
    project = lambda k: _dot(h, win_ref[:, order[k]:order[k] + gw])
    pre = project(0)
    total = None
    for k, cs in enumerate(groups):
        nxt = project(k + 1)
        vg = _gelu(pre)
        v_scr[:, cs] = vg
        part = jnp.sum(vg, axis=-1, keepdims=True)
        total = part if total is None else total + part
        pre = nxt
    mu = total * (1.0 / width)
    sq = None
    for cs in groups:
        vc = v_scr[:, cs] - mu
        part = jnp.sum(vc * vc, axis=-1, keepdims=True)
        sq = part if sq is None else sq + part
    rstd = lax.rsqrt(sq * (1.0 / width) + EPS)
    bs = bs_ref[...]
    y = None
    for g, cs in enumerate(groups):
        nxt = project(SG_GROUPS + g + 1) if g + 1 < SG_GROUPS else None
        vn = ((v_scr[:, cs] - mu) * rstd * lng_ref[:, cs] + lnb_ref[:, cs]).astype(BF16)
        u = _gelu(pre)
        pre = nxt
        parts = []
        for c in range(tm // SG_CHUNK):
            rs = slice(c * SG_CHUNK, (c + 1) * SG_CHUNK)
            mixed = _dot(ws_ref[g], vn[rs]) + bs[:, g:g + 1]
            parts.append((u[rs] * mixed).astype(BF16))
        yg = _dot(jnp.concatenate(parts, axis=0), wout_ref[cs, :])
        y = yg if y is None else y + yg
    o_ref[...] = x + mod[:, 2 * d:] * y


def _spatial_gating(x, mod3, g, w_in, ln_g, ln_b, w_s, b_s, w_out, tm=1024):
    bsz, n, d = x.shape
    width = w_out.shape[0]
    tm = min(tm, n)
    w_in, w_s, w_out = w_in.astype(BF16), w_s.astype(BF16), w_out.astype(BF16)
    const = lambda shape: pl.BlockSpec(shape, lambda b, i: (0,) * len(shape), pipeline_mode=pl.Buffered(1))
    return pl.pallas_call(
        functools.partial(_sg_body, d=d, width=width, tm=tm),
        out_shape=jax.ShapeDtypeStruct(x.shape, F32),
        grid=(bsz, n // tm),
        in_specs=[
            pl.BlockSpec((None, tm, d), lambda b, i: (b, i, 0)),
            pl.BlockSpec((None, 1, 3 * d), lambda b, i: (b, 0, 0)),
            const((1, d)), const(w_in.shape), const((1, width)), const((1, width)),
            const(w_s.shape), const((SG_CHUNK, SG_GROUPS)), const(w_out.shape),
        ],
        out_specs=pl.BlockSpec((None, tm, d), lambda b, i: (b, i, 0)),
        scratch_shapes=[pltpu.VMEM((tm, width), F32)],
        compiler_params=_params(2),
        name="spatial_gating",
    )(x, mod3, g.reshape(1, d), w_in, ln_g.reshape(1, width), ln_b.reshape(1, width), w_s, b_s.T, w_out)


def _rope_tables(n):
    hd = ML_HEAD_DIM
    n_pairs = hd // 4
    inv_freq = ROPE_THETA ** (-jnp.arange(n_pairs, dtype=F32) / n_pairs)
    row_ang = jnp.arange(n // GRID_W, dtype=F32)[:, None] * inv_freq
    col_ang = jnp.arange(GRID_W, dtype=F32)[:, None] * inv_freq

    def table(ang, is_row):
        zero = jnp.zeros_like(ang)
        cos, sin = jnp.cos(ang), jnp.sin(ang)
        half = lambda t: jnp.concatenate([t, zero] if is_row else [zero, t], axis=-1)
        return jnp.stack([jnp.concatenate([half(cos), half(cos)], axis=-1),
                          jnp.concatenate([half(-sin), half(sin)], axis=-1)])

    return table(row_ang, True), table(col_ang, False)


def _even_weights(w_in, gate_b):
    d = w_in.shape[0]
    hd = ML_HEAD_DIM
    base = 3 * NA_WIDTH
    deint = lambda w: w.reshape(d, ML_HEADS, hd // 2, 2).transpose(0, 1, 3, 2).reshape(d, ML_WIDTH)
    qb = deint(w_in[:, base:base + ML_WIDTH])
    kb = deint(w_in[:, base + ML_WIDTH:base + 2 * ML_WIDTH])
    w_main = jnp.concatenate([w_in[:, :base], qb, kb, w_in[:, base + 3 * ML_WIDTH:base + 4 * ML_WIDTH]],
                             axis=1).astype(BF16)
    w_vt = w_in[:, base + 2 * ML_WIDTH:base + 3 * ML_WIDTH].T.astype(BF16)
    wg = w_in[:, base + 4 * ML_WIDTH:]
    w_gc = jnp.pad(wg, ((0, 0), (0, V7X_LANES - N_GATES))).astype(BF16)
    w_gr = wg.T.astype(BF16)
    gb = gate_b.reshape(N_GATES).astype(F32)
    gb_col = jnp.pad(gb, (0, V7X_LANES - N_GATES)).reshape(1, V7X_LANES)
    gb_row = gb.reshape(N_GATES, 1)
    return w_main, w_vt, w_gc, w_gr, gb_col, gb_row


def kernel(x, c, ctx, c_ctx, w_mod, b_mod, norm_g, ffn_w_in, ffn_w_out, mix_w_in, na_rpb, ml_gate_b, ml_head_g,
           mix_w_out, sg_w_in, sg_ln_g, sg_ln_b, sg_w_s, sg_b_s, sg_w_out, final_g):
    bsz, n, d = x.shape
    depth = w_mod.shape[0]
    ctx_row = bsz
    cvec = jnp.zeros((V7X_SUBLANES, d), F32).at[:bsz].set(c).at[ctx_row].set(c_ctx)
    mod = _mod_vectors(cvec, w_mod, b_mod).reshape(depth, V7X_SUBLANES, 3, 1, 3 * d)
    batch_row = lambda b: b
    context_row = lambda b: ctx_row
    last_ctx_layer = ((depth - 1) // 2) * 2
    ffn_order = [(l, half) for l in range(depth) for half in range(2)]
    following = lambda l, half: dict(zip(ffn_order, ffn_order[1:])).get((l, half))
    cast_of = lambda nxt: None if nxt is None else (ffn_w_in, ffn_w_out) + nxt
    w_now = (ffn_w_in[0, 0].astype(BF16), ffn_w_out[0, 0].astype(BF16))
    xc = ctx
    for l in range(depth):
        ctx_in = l <= last_ctx_layer
        ctx_out = l < last_ctx_layer
        if ctx_in:
            xc = _ffn(xc, mod[l, :, 0], context_row, norm_g[l, 0], *w_now)
        x, w_now = _ffn(x, mod[l, :, 0], batch_row, norm_g[l, 0], *w_now, cast_next=cast_of(following(l, 0)))
        if l % 2 == 0:
            e = l // 2
            weights = _even_weights(mix_w_in[e], ml_gate_b[e])
            qa, ka, va, qb, kb, vt, ob, gcol, grow = _even_projection(
                x, mod[l, :, 1], batch_row, norm_g[l, 1], weights, _rope_tables(n))
            _, kax, vax, _, kbx, vtx, _, _, growx = _even_projection(
                xc, mod[l, :, 1], context_row, norm_g[l, 1], weights, None)
            bias = _na_bias_tables(na_rpb[e], n // GRID_W)
            ya = _neighbourhood_attention(qa, ka, va, kax, vax, bias)
            c0 = jnp.zeros((bsz, 2, ML_HEADS // 2, ML_HEAD_DIM + ML_AUG_ROWS, 2 * ML_HEAD_DIM), F32)
            m0 = jnp.zeros((bsz, 2, ML_HEADS, V7X_SUBLANES, V7X_LANES), F32)
            c1, m1 = _mlstm_scan(None, kbx, vtx, None, growx, c0, m0, False)
            hf, hr = _mlstm_scan(qb, kb, vt, gcol, grow, c1, m1, True)
            mixer = (mod[l, :, 1], ya, hf, hr, ob, ml_head_g[e], mix_w_out[e].astype(BF16))
            assert not ctx_out, "context output path is not needed for this depth"
        else:
            o = l // 2
            mixer = None
            x = _spatial_gating(x, mod[l, :, 1], norm_g[l, 1], sg_w_in[o], sg_ln_g[o], sg_ln_b[o],
                                sg_w_s[o], sg_b_s[o], sg_w_out[o])
        fg = final_g if l == depth - 1 else None
        nxt = following(l, 1)
        out = _ffn(x, mod[l, :, 2], batch_row, norm_g[l, 2], *w_now, final_g=fg, mixer=mixer, cast_next=cast_of(nxt))
        x, w_now = out if nxt is not None else (out, None)
    return x
```

```python
import functools

import numpy as np
import jax
import jax.numpy as jnp
from jax import lax
from jax.experimental import pallas as pl
from jax.experimental.pallas import tpu as pltpu

GRID_W = 64
NA_HEADS = 8
NA_HEAD_DIM = 64
NA_KH = 8
NA_KW = 16
ML_HEADS = 4
ML_HEAD_DIM = 128
ML_CHUNK = 128
ROPE_THETA = 10000.0
SG_CHUNK = 128
SG_GROUPS = 8
EPS = 1e-6
NA_WIDTH = NA_HEADS * NA_HEAD_DIM
ML_WIDTH = ML_HEADS * ML_HEAD_DIM
N_GATES = 4 * ML_HEADS
ML_AUG_ROWS = 16
ML_CHUNKS_PER_STEP = 2

V7X_LANES = 128
V7X_SUBLANES = 8
V7X_MXU_COLUMNS = 256
V7X_VMEM_LIMIT_BYTES = 56 * 1024 * 1024

NEG_BIG = -1e30
FFN_TOKENS_PER_STEP = 1024
V7X_BF16_SUBLANES = 16
NA_ROWS_PER_STEP = 32
NA_SUB_ROWS = 4
NA_BAND_ROWS = 12

BF16 = jnp.bfloat16
F32 = jnp.float32


def _dot(a, b):
    return jnp.dot(a, b, preferred_element_type=F32)


def _dot_nt(a, b):
    return lax.dot_general(a, b, (((1,), (1,)), ((), ())), preferred_element_type=F32)


def _params(n_axes):
    return pltpu.CompilerParams(
        dimension_semantics=("arbitrary",) * n_axes,
        vmem_limit_bytes=V7X_VMEM_LIMIT_BYTES,
    )


def _modulated(x, g, mod, d):
    shift = mod[:, :d]
    scale = mod[:, d:2 * d]
    gs = g * (1.0 + scale)
    ms = jnp.mean(x * x, axis=-1, keepdims=True)
    return x * lax.rsqrt(ms + EPS) * gs + shift


def _mod_body(c_ref, w_ref, b_ref, o_ref):
    c = c_ref[...]
    s = (c * jax.nn.sigmoid(c)).astype(BF16)
    o_ref[...] = _dot(s, w_ref[...].astype(BF16)) + b_ref[...]


def _mod_vectors(cvec, w_mod, b_mod):
    depth, d, width = w_mod.shape
    tn = width // 8
    return pl.pallas_call(
        _mod_body,
        out_shape=jax.ShapeDtypeStruct((depth, V7X_SUBLANES, width), F32),
        grid=(depth, width // tn),
        in_specs=[
            pl.BlockSpec((V7X_SUBLANES, d), lambda l, j: (0, 0)),
            pl.BlockSpec((None, d, tn), lambda l, j: (l, 0, j)),
            pl.BlockSpec((None, 1, tn), lambda l, j: (l, 0, j)),
        ],
        out_specs=pl.BlockSpec((None, V7X_SUBLANES, tn), lambda l, j: (l, 0, j)),
        compiler_params=_params(2),
        name="mod_vectors",
    )(cvec, w_mod, b_mod.reshape(depth, 1, width))


def _mixer_output(ya_ref, hf_ref, hr_ref, ob_ref, hg_ref, w_ref):
    hd = ML_HEAD_DIM
    hb = hf_ref[...] + hr_ref[...]
    ob = ob_ref[...]
    hg = hg_ref[...]
    y = _dot(ya_ref[...], w_ref[0:NA_WIDTH, :])
    ybs = []
    for hh in range(ML_HEADS):
        sl = slice(hh * hd, (hh + 1) * hd)
        hs = hb[:, sl]
        ms = jnp.mean(hs * hs, axis=-1, keepdims=True)
        ybs.append((hs * lax.rsqrt(ms + EPS) * hg[:, sl] * jax.nn.sigmoid(ob[:, sl])).astype(BF16))
    return y + _dot(jnp.concatenate(ybs, axis=1), w_ref[NA_WIDTH:, :])


def _ffn_body(*refs, d, d_ff, chunk, final, mixer, cast_next):
    refs = list(refs)
    if cast_next:
        next_in_ref, next_out_ref, o_ref, cast_in_ref, cast_out_ref = refs[-5:]
        cast_in_ref[...] = next_in_ref[...].astype(BF16)
        cast_out_ref[...] = next_out_ref[...].astype(BF16)
        refs = refs[:-5]
    else:
        o_ref = refs.pop()
    x_ref, mod_ref, g_ref, win_ref, wout_ref = refs[:5]
    rest = refs[5:]
    x = x_ref[...]
    if mixer:
        mixmod_ref = rest[0]
        x = x + mixmod_ref[...][:, 2 * d:] * _mixer_output(*rest[1:7])
        rest = rest[7:]
    if final:
        (fg_ref,) = rest
    mod = mod_ref[...]
    h = _modulated(x, g_ref[...], mod, d).astype(BF16)
    acc = None
    for c in range(d_ff // chunk):
        a = _dot(h, win_ref[:, c * chunk:(c + 1) * chunk])
        b = _dot(h, win_ref[:, d_ff + c * chunk:d_ff + (c + 1) * chunk])
        t = (a * jax.nn.sigmoid(a) * b).astype(BF16)
        y = _dot(t, wout_ref[c * chunk:(c + 1) * chunk, :])
        acc = y if acc is None else acc + y
    out = x + (0.5 * mod[:, 2 * d:]) * acc
    if final:
        ms = jnp.mean(out * out, axis=-1, keepdims=True)
        out = out * lax.rsqrt(ms + EPS) * fg_ref[...]
    o_ref[...] = out


def _ffn(x, mod3, mod_row, g, w_in, w_out, final_g=None, mixer=None, cast_next=None, tm=None):
    bsz, n, d = x.shape
    d_ff = w_out.shape[-2]
    if tm is None:
        tm = FFN_TOKENS_PER_STEP // 2 if mixer is not None else FFN_TOKENS_PER_STEP
    tm = min(tm, n)
    steps = n // tm
    final = final_g is not None
    tok_spec = lambda width: pl.BlockSpec((None, tm, width), lambda b, i: (b, i, 0))
    resident = dict(pipeline_mode=pl.Buffered(1))
    in_specs = [
        tok_spec(d),
        pl.BlockSpec((None, 1, 3 * d), lambda b, i: (mod_row(b), 0, 0)),
        pl.BlockSpec((1, d), lambda b, i: (0, 0)),
        pl.BlockSpec((d, 2 * d_ff), lambda b, i: (0, 0), **resident),
        pl.BlockSpec((None,) * (w_out.ndim - 2) + (d_ff, d), lambda b, i: (0,) * w_out.ndim, **resident),
    ]
    args = [x, mod3, g.reshape(1, d), w_in, w_out]
    if mixer is not None:
        mix_mod3, ya, hf, hr, ob, head_g, w_mix = mixer
        in_specs += [
            pl.BlockSpec((None, 1, 3 * d), lambda b, i: (mod_row(b), 0, 0)),
            tok_spec(NA_WIDTH), tok_spec(ML_WIDTH), tok_spec(ML_WIDTH), tok_spec(ML_WIDTH),
            pl.BlockSpec((1, ML_WIDTH), lambda b, i: (0, 0)),
            pl.BlockSpec(w_mix.shape, lambda b, i: (0, 0), **resident),
        ]
        args += [mix_mod3, ya, hf, hr, ob, head_g.reshape(1, ML_WIDTH), w_mix]
    if final:
        in_specs.append(pl.BlockSpec((1, d), lambda b, i: (0, 0)))
        args.append(final_g.reshape(1, d))
    out_shape = [jax.ShapeDtypeStruct(x.shape, F32)]
    out_specs = [tok_spec(d)]
    if cast_next is not None:
        w_in_all, w_out_all, layer, half = cast_next
        rows_in = d // (bsz * steps)
        assert rows_in * bsz * steps == d and rows_in % V7X_BF16_SUBLANES == 0, (d, bsz, steps)
        tiles = d_ff // V7X_BF16_SUBLANES
        assert tiles * V7X_BF16_SUBLANES == d_ff, d_ff
        blocks = max(k for k in range(1, steps + 1) if tiles % k == 0)
        rows_out = d_ff // blocks
        in_row = lambda b, i: b * steps + i
        out_row = lambda b, i: jnp.minimum(i, blocks - 1)
        in_specs += [
            pl.BlockSpec((None, None, rows_in, 2 * d_ff), lambda b, i: (layer, half, in_row(b, i), 0)),
            pl.BlockSpec((None, None, rows_out, d), lambda b, i: (layer, half, out_row(b, i), 0)),
        ]
        args += [w_in_all, w_out_all]
        out_shape += [jax.ShapeDtypeStruct((d, 2 * d_ff), BF16), jax.ShapeDtypeStruct((bsz, d_ff, d), BF16)]
        out_specs += [pl.BlockSpec((rows_in, 2 * d_ff), lambda b, i: (in_row(b, i), 0)),
                      pl.BlockSpec((None, rows_out, d), lambda b, i: (b, out_row(b, i), 0))]
    outs = pl.pallas_call(
        functools.partial(_ffn_body, d=d, d_ff=d_ff, chunk=V7X_MXU_COLUMNS, final=final,
                          mixer=mixer is not None, cast_next=cast_next is not None),
        out_shape=out_shape,
        grid=(bsz, steps),
        in_specs=in_specs,
        out_specs=out_specs,
        compiler_params=_params(2),
        name="macaron_ffn",
    )(*args)
    return (outs[0], (outs[1], outs[2])) if cast_next is not None else outs[0]


def _log_gates(z, is_forget):
    ls = jnp.minimum(z, 0.0) - jnp.log1p(jnp.exp(-jnp.abs(z)))
    return jnp.where(is_forget, ls, z)


def _split3(x):
    hi = x.astype(BF16)
    r1 = x - hi.astype(F32)
    mid = r1.astype(BF16)
    lo = (r1 - mid.astype(F32)).astype(BF16)
    return [hi, mid, lo]


def _evenproj_body(x_ref, mod_ref, g_ref, w_ref, wvt_ref, wg_ref, wgt_ref, gbc_ref, gbr_ref, rowt_ref, colt_ref,
                   qa_ref, ka_ref, va_ref, qb_ref, kb_ref, vt_ref, ob_ref, gc_ref, gr_ref, *, d, rope):
    h = _modulated(x_ref[...], g_ref[...], mod_ref[...], d).astype(BF16)
    gates = _gate_stages(h, wg_ref, wgt_ref, gbc_ref, gbr_ref, gc_ref, gr_ref)
    nw, mw, hd = NA_WIDTH, ML_WIDTH, ML_HEAD_DIM
    next(gates, None)
    qa_ref[...] = (_dot(h, w_ref[:, 0:nw]) * (NA_HEAD_DIM ** -0.5)).astype(BF16)
    next(gates, None)
    ka_ref[...] = _dot(h, w_ref[:, nw:2 * nw]).astype(BF16)
    next(gates, None)
    va_ref[...] = _dot(h, w_ref[:, 2 * nw:3 * nw]).astype(BF16)
    next(gates, None)
    base = 3 * nw
    qb = _dot(h, w_ref[:, base:base + mw])
    next(gates, None)
    kb = _dot(h, w_ref[:, base + mw:base + 2 * mw])
    next(gates, None)
    if rope:
        tm = qb.shape[0]
        expand = lambda t: jnp.broadcast_to(t[:, None, :], (tm // GRID_W, GRID_W, hd)).reshape(tm, hd)
        cos = expand(rowt_ref[0]) + colt_ref[0]
        sin = expand(rowt_ref[1]) + colt_ref[1]
        for hh in range(ML_HEADS):
            sl = slice(hh * hd, (hh + 1) * hd)
            qh = qb[:, sl]
            kh = kb[:, sl]
            qb_ref[:, sl] = (qh * cos + pltpu.roll(qh, hd // 2, 1) * sin).astype(BF16)
            kb_ref[:, sl] = ((kh * cos + pltpu.roll(kh, hd // 2, 1) * sin) * (hd ** -0.5)).astype(BF16)
    else:
        qb_ref[...] = qb.astype(BF16)
        kb_ref[...] = (kb * (hd ** -0.5)).astype(BF16)
    vt_ref[...] = _dot_nt(wvt_ref[...], h).astype(BF16)
    next(gates, None)
    ob_ref[...] = _dot(h, w_ref[:, base + 2 * mw:base + 3 * mw])
    for _ in gates:
        pass


def _gate_stages(h, wg_ref, wgt_ref, gbc_ref, gbr_ref, gc_ref, gr_ref):
    L = ML_CHUNK
    tm = h.shape[0]
    r_idx = lax.broadcasted_iota(jnp.int32, (L, L), 0)
    c_idx = lax.broadcasted_iota(jnp.int32, (L, L), 1)
    lower = (c_idx <= r_idx).astype(BF16)
    upper = (r_idx <= c_idx).astype(BF16)
    zc = _dot(h, wg_ref[...]) + gbc_ref[...]
    zr = _dot_nt(wgt_ref[...], h) + gbr_ref[...]
    yield
    lane = lax.broadcasted_iota(jnp.int32, zc.shape, 1)
    gates_c = _log_gates(zc, (lane % 2) == 1)
    split_c = jnp.concatenate(_split3(gates_c), axis=1)
    row = lax.broadcasted_iota(jnp.int32, zr.shape, 0)
    gates_r = _log_gates(zr, (row % 2) == 1)
    split_r = jnp.concatenate(_split3(gates_r), axis=0)
    yield
    bwd_lane = ((lax.broadcasted_iota(jnp.int32, (L, V7X_LANES), 1) // 2) % 2) == 1
    for c in range(tm // L):
        rs = slice(c * L, (c + 1) * L)
        r = _dot(lower, split_c[rs])
        prefix = r[:, 0:V7X_LANES] + r[:, V7X_LANES:2 * V7X_LANES] + r[:, 2 * V7X_LANES:]
        g = gates_c[rs]
        b = jnp.where(bwd_lane, prefix[L - 1:L, :] - prefix + g, prefix)
        gc_ref[rs, :] = g - pltpu.roll(b, V7X_LANES - 1, 1)
        if c % 2 == 1:
            yield
    row_l = lax.broadcasted_iota(jnp.int32, (N_GATES, L), 0)
    bwd_row = ((row_l // 2) % 2) == 1
    f_row = (row_l % 2) == 1
    for c in range(tm // L):
        ls = slice(c * L, (c + 1) * L)
        r = _dot(split_r[:, ls], upper)
        prefix = r[0:N_GATES] + r[N_GATES:2 * N_GATES] + r[2 * N_GATES:]
        g = gates_r[:, ls]
        b = jnp.where(bwd_row, prefix[:, L - 1:L] - prefix + g, prefix)
        gr_ref[:, ls] = jnp.where(f_row, b, g - pltpu.roll(b, N_GATES - 1, 0))
        if c % 2 == 1:
            yield


def _even_projection(x, mod3, mod_row, g, weights, rope_tables, tm=1024):
    w_main, w_vt, w_gc, w_gr, gb_col, gb_row = weights
    bsz, n, d = x.shape
    tm = min(tm, n)
    rope = rope_tables is not None
    if rope:
        row_t, col_t = rope_tables
        col_t = jnp.tile(col_t, (1, tm // GRID_W, 1))
    else:
        row_t = jnp.zeros((2, tm // GRID_W, ML_HEAD_DIM), F32)
        col_t = jnp.zeros((2, tm, ML_HEAD_DIM), F32)
    wm = w_main.shape[1]
    tok = lambda width, dt: jax.ShapeDtypeStruct((bsz, n, width), dt)
    tok_spec = lambda width: pl.BlockSpec((None, tm, width), lambda b, i: (b, i, 0))
    const = lambda shape: pl.BlockSpec(shape, lambda b, i: (0,) * len(shape))
    out_shape = [tok(NA_WIDTH, BF16)] * 3 + [tok(ML_WIDTH, BF16)] * 2 + [
        jax.ShapeDtypeStruct((bsz, ML_WIDTH, n), BF16), tok(ML_WIDTH, F32),
        tok(V7X_LANES, F32), jax.ShapeDtypeStruct((bsz, N_GATES, n), F32)]
    out_specs = [tok_spec(NA_WIDTH)] * 3 + [tok_spec(ML_WIDTH)] * 2 + [
        pl.BlockSpec((None, ML_WIDTH, tm), lambda b, i: (b, 0, i)), tok_spec(ML_WIDTH),
        tok_spec(V7X_LANES), pl.BlockSpec((None, N_GATES, tm), lambda b, i: (b, 0, i))]
    return pl.pallas_call(
        functools.partial(_evenproj_body, d=d, rope=rope),
        out_shape=out_shape,
        grid=(bsz, n // tm),
        in_specs=[
            tok_spec(d),
            pl.BlockSpec((None, 1, 3 * d), lambda b, i: (mod_row(b), 0, 0)),
            const((1, d)),
            const((d, wm)),
            const((ML_WIDTH, d)),
            const((d, V7X_LANES)),
            const((N_GATES, d)),
            const((1, V7X_LANES)),
            const((N_GATES, 1)),
            pl.BlockSpec((2, tm // GRID_W, ML_HEAD_DIM), lambda b, i: (0, i if rope else 0, 0)),
            pl.BlockSpec((2, tm, ML_HEAD_DIM), lambda b, i: (0, 0, 0)),
        ],
        out_specs=out_specs,
        compiler_params=_params(2),
        name="even_projection",
    )(x, mod3, g.reshape(1, d), w_main, w_vt, w_gc, w_gr, gb_col, gb_row, row_t, col_t)


def _na_body(q_ref, k_ref, v_ref, kx_ref, vx_ref, *rest, rows):
    rb = pl.program_id(2)
    sub, band = NA_SUB_ROWS, NA_BAND_ROWS
    n_sub = NA_ROWS_PER_STEP // sub
    tq = sub * GRID_W
    bias_refs, o_ref = rest[:n_sub], rest[n_sub]
    kx = kx_ref[...]
    vx = vx_ref[...]
    q = q_ref[...]
    lane = lax.broadcasted_iota(jnp.int32, q.shape, 1)
    bands = []
    for u in range(n_sub):
        band0 = jnp.clip((rb * n_sub + u) * sub - NA_KH // 2, 0, rows - band)
        start = pl.multiple_of(band0 * GRID_W, GRID_W)
        bands.append((k_ref[pl.ds(start, band * GRID_W), :], v_ref[pl.ds(start, band * GRID_W), :]))
    qms = [jnp.where((lane // NA_HEAD_DIM) == hh, q, jnp.zeros_like(q)) for hh in range(2)]
    items = [(hh, u) for hh in range(2) for u in range(n_sub)]

    def scores(item):
        hh, u = item
        qm = qms[hh][u * tq:(u + 1) * tq]
        return _dot_nt(qm, bands[u][0]) + bias_refs[u][hh], _dot_nt(qm, kx)

    def attend(item, s_loc, s_ctx):
        u = item[1]
        m = jnp.maximum(jnp.max(s_loc, axis=-1, keepdims=True), jnp.max(s_ctx, axis=-1, keepdims=True))
        p_loc = jnp.exp(s_loc - m)
        p_ctx = jnp.exp(s_ctx - m)
        denom = jnp.sum(p_loc, axis=-1, keepdims=True) + jnp.sum(p_ctx, axis=-1, keepdims=True)
        o = _dot(p_loc.astype(BF16), bands[u][1]) + _dot(p_ctx.astype(BF16), vx)
        return o * (1.0 / denom)

    outs = {}
    ready = scores(items[0])
    for k, item in enumerate(items):
        nxt = scores(items[k + 1]) if k + 1 < len(items) else None
        outs[item] = attend(item, *ready)
        ready = nxt
    per_head = [jnp.concatenate([outs[(hh, u)] for u in range(n_sub)], axis=0) for hh in range(2)]
    lane_o = lax.broadcasted_iota(jnp.int32, per_head[0].shape, 1)
    o_ref[...] = jnp.where(lane_o < NA_HEAD_DIM, per_head[0], per_head[1]).astype(BF16)


def _na_bias_tables(rpb, rows):
    sub, band, w = NA_SUB_ROWS, NA_BAND_ROWS, GRID_W
    kh, kw = NA_KH, NA_KW
    n_heads = rpb.shape[0]
    cols = np.arange(w)
    c0 = np.clip(cols - kw // 2, 0, w - kw)
    cc = np.arange(w)[None, :]
    col_ok = (cc >= c0[:, None]) & (cc < c0[:, None] + kw)
    padded = jnp.pad(rpb, ((0, 0), (0, 0), (w, w)))
    shifted = jnp.stack([padded[:, :, w + kw - 1 - j:2 * w + kw - 1 - j] for j in range(w)], axis=2)
    t1 = jnp.where(col_ok[None, None], shifted, NEG_BIG)
    masked = jnp.full((n_heads, w, w), NEG_BIG, F32)
    variants = []
    for i0, b0 in ((0, 0), (sub, sub - kh // 2), (rows - sub, rows - band)):
        blocks = []
        for qi in range(i0, i0 + sub):
            r0 = min(max(qi - kh // 2, 0), rows - kh)
            row = [t1[:, r - qi + kh - 1] if r0 <= r < r0 + kh else masked for r in range(b0, b0 + band)]
            blocks.append(jnp.concatenate(row, axis=-1))
        variants.append(jnp.concatenate(blocks, axis=1))
    return jnp.stack(variants)


def _neighbourhood_attention(q, k, v, kx, vx, bias):
    bsz, n, _ = q.shape
    lc = kx.shape[1]
    rows = n // GRID_W
    sub, band = NA_SUB_ROWS, NA_BAND_ROWS
    n_sub = NA_ROWS_PER_STEP // sub
    nrb = rows // NA_ROWS_PER_STEP
    tq = NA_ROWS_PER_STEP * GRID_W
    pair = 2 * NA_HEAD_DIM
    last = rows // sub - 1

    def bias_spec(u):
        def index(b, hp, rb):
            sb = rb * n_sub + u
            return (jnp.where(sb == 0, 0, jnp.where(sb == last, 2, 1)), hp, 0, 0)
        return pl.BlockSpec((None, 2, sub * GRID_W, band * GRID_W), index)

    return pl.pallas_call(
        functools.partial(_na_body, rows=rows),
        out_shape=jax.ShapeDtypeStruct((bsz, n, NA_WIDTH), BF16),
        grid=(bsz, NA_HEADS // 2, nrb),
        in_specs=[
            pl.BlockSpec((None, tq, pair), lambda b, hp, rb: (b, rb, hp)),
            pl.BlockSpec((None, n, pair), lambda b, hp, rb: (b, 0, hp)),
            pl.BlockSpec((None, n, pair), lambda b, hp, rb: (b, 0, hp)),
            pl.BlockSpec((None, lc, pair), lambda b, hp, rb: (b, 0, hp)),
            pl.BlockSpec((None, lc, pair), lambda b, hp, rb: (b, 0, hp)),
        ] + [bias_spec(u) for u in range(n_sub)],
        out_specs=pl.BlockSpec((None, tq, pair), lambda b, hp, rb: (b, rb, hp)),
        compiler_params=_params(3),
        name="neighbourhood_attention",
    )(q, k, v, kx, vx, *([bias] * n_sub))


def _mlstm_body(*refs, with_output):
    if with_output:
        (qf_ref, kf_ref, vtf_ref, gcf_ref, grf_ref, qr_ref, kr_ref, vtr_ref, gcr_ref, grr_ref,
         c0_ref, m0_ref, hf_ref, hr_ref, c_scr, m_scr) = refs
    else:
        (kf_ref, vtf_ref, grf_ref, kr_ref, vtr_ref, grr_ref,
         c0_ref, m0_ref, c_out_ref, m_out_ref, c_scr, m_scr) = refs
    step = pl.program_id(0)
    n_steps = pl.num_programs(0)
    L, hd = ML_CHUNK, ML_HEAD_DIM
    bsz = kf_ref.shape[0]

    @pl.when(step == 0)
    def _():
        c_scr[...] = c0_ref[...]
        m_scr[...] = m0_ref[...]

    s_idx = lax.broadcasted_iota(jnp.int32, (L, L), 0)
    t_idx = lax.broadcasted_iota(jnp.int32, (L, L), 1)
    ones_rows = (lax.broadcasted_iota(jnp.int32, (ML_AUG_ROWS, L), 0) == 0).astype(BF16)
    fwd_refs = (kf_ref, vtf_ref, grf_ref) + ((qf_ref, gcf_ref, hf_ref) if with_output else (None,) * 3)
    bwd_refs = (kr_ref, vtr_ref, grr_ref) + ((qr_ref, gcr_ref, hr_ref) if with_output else (None,) * 3)
    cps = kf_ref.shape[1] // L
    items = [(sub, bi, d, p) for sub in range(cps) for bi in range(bsz) for d in range(2)
             for p in range(ML_HEADS // 2)]
    lane2 = lax.broadcasted_iota(jnp.int32, (L, 2 * hd), 1)
    zero_ll = jnp.zeros((L, L), BF16)

    def refs_of(item):
        _, bi, d, _ = item
        return tuple(r if r is None else r.at[bi] for r in (fwd_refs if d == 0 else bwd_refs))

    def tokens_of(item):
        sub, _, d, _ = item
        first = sub if d == 0 else cps - 1 - sub
        return slice(first * L, (first + 1) * L)

    def block_diag(x2):
        zero = jnp.zeros_like(x2)
        return jnp.concatenate([jnp.where(lane2 < hd, x2, zero), jnp.where(lane2 >= hd, x2, zero)], axis=0)

    def front(item):
        _, bi, d, p = item
        k_ref, _, _, q_ref, _, _ = refs_of(item)
        tok = tokens_of(item)
        sl2 = slice(2 * p * hd, (2 * p + 2) * hd)
        qbd = block_diag(q_ref[tok, sl2])
        return _dot_nt(k_ref[tok, sl2], qbd), _dot_nt(c_scr[bi, d, p].astype(BF16), qbd)

    def finish(item, fronts):
        _, bi, d, p = item
        k_ref, vt_ref, gr_ref, _, gc_ref, h_ref = refs_of(item)
        tok = tokens_of(item)
        visible = (s_idx <= t_idx) if d == 0 else (s_idx >= t_idx)
        end = L - 1 if d == 0 else 0
        sl2 = slice(2 * p * hd, (2 * p + 2) * hd)
        a_prevs, a_toks, vt_augs, ps, w_inters, w_corrs, inv_floors = [], [], [], [], [], [], []
        for j in range(2):
            hh = 2 * p + j
            gi = hh * 4 + d * 2
            c_row = gr_ref[gi:gi + 1, tok]
            b_row = gr_ref[gi + 1:gi + 2, tok]
            b_end = b_row[:, end:end + 1]
            m_old = m_scr[bi, d, hh][0:1, 0:1]
            m_new = jnp.maximum(b_end + m_old, b_end + jnp.max(c_row, axis=-1, keepdims=True))
            a_prevs.append(jnp.broadcast_to(jnp.exp(b_end + m_old - m_new), (1, hd)))
            a_toks.append(jnp.exp(b_end + c_row - m_new))
            vt_augs.append(jnp.concatenate([vt_ref[hh * hd:(hh + 1) * hd, tok], ones_rows], axis=0))
            m_scr[bi, d, hh] = jnp.broadcast_to(m_new, (V7X_SUBLANES, V7X_LANES))
            if with_output:
                st = fronts[0][:, j * L:(j + 1) * L]
                log_w = jnp.where(visible, b_row + gc_ref[tok, gi:gi + 1], NEG_BIG)
                m_intra = jnp.max(log_w, axis=0, keepdims=True)
                ps.append((st * jnp.exp(log_w - m_intra)).astype(BF16))
                log_inter = b_row + m_old
                m_t = jnp.maximum(log_inter, m_intra)
                w_inters.append(jnp.exp(log_inter - m_t))
                w_corrs.append(jnp.exp(m_intra - m_t))
                inv_floors.append(jnp.exp(-m_t))
        cat = lambda parts: jnp.concatenate(parts, axis=1)
        vt2 = cat(vt_augs)
        c_old = c_scr[bi, d, p]
        if with_output:
            pbd = jnp.concatenate([cat([ps[0], zero_ll]), cat([zero_ll, ps[1]])], axis=0)
            intra = _dot(vt2, pbd)
            both = cat(w_inters) * fronts[1] + cat(w_corrs) * intra
            den = both[hd:hd + 1, :]
            ht = both[:hd, :] * (1.0 / jnp.maximum(jnp.abs(den), cat(inv_floors)))
            for j in range(2):
                h_ref[tok, (2 * p + j) * hd:(2 * p + j + 1) * hd] = ht[:, j * L:(j + 1) * L].T
        av = (vt2.astype(F32) * cat(a_toks)).astype(BF16)
        c_scr[bi, d, p] = cat(a_prevs) * c_old + _dot(av, block_diag(k_ref[tok, sl2]))

    ready = front(items[0]) if with_output else None
    for i, item in enumerate(items):
        nxt = front(items[i + 1]) if with_output and i + 1 < len(items) else None
        finish(item, ready)
        ready = nxt

    if not with_output:
        @pl.when(step == n_steps - 1)
        def _():
            c_out_ref[...] = c_scr[...]
            m_out_ref[...] = m_scr[...]


def _mlstm_scan(q, k, vt, gcol, grow, c0, m0, with_output):
    bsz, n, _ = k.shape
    L, hd = ML_CHUNK, ML_HEAD_DIM
    span = min(ML_CHUNKS_PER_STEP, n // L) * L
    nsteps = n // span
    fwd = lambda width: pl.BlockSpec((bsz, span, width), lambda c: (0, c, 0))
    rev = lambda width: pl.BlockSpec((bsz, span, width), lambda c: (0, nsteps - 1 - c, 0))
    fwd_t = lambda height: pl.BlockSpec((bsz, height, span), lambda c: (0, 0, c))
    rev_t = lambda height: pl.BlockSpec((bsz, height, span), lambda c: (0, 0, nsteps - 1 - c))
    c_spec = pl.BlockSpec((bsz, 2, ML_HEADS // 2, hd + ML_AUG_ROWS, 2 * hd), lambda c: (0, 0, 0, 0, 0))
    m_spec = pl.BlockSpec((bsz, 2, ML_HEADS, V7X_SUBLANES, V7X_LANES), lambda c: (0, 0, 0, 0, 0))
    if with_output:
        args = [q, k, vt, gcol, grow, q, k, vt, gcol, grow, c0, m0]
        in_specs = ([fwd(ML_WIDTH)] * 2 + [fwd_t(ML_WIDTH), fwd(V7X_LANES), fwd_t(N_GATES)]
                    + [rev(ML_WIDTH)] * 2 + [rev_t(ML_WIDTH), rev(V7X_LANES), rev_t(N_GATES)])
        out_shape = [jax.ShapeDtypeStruct((bsz, n, ML_WIDTH), F32)] * 2
        out_specs = [fwd(ML_WIDTH), rev(ML_WIDTH)]
    else:
        args = [k, vt, grow, k, vt, grow, c0, m0]
        in_specs = ([fwd(ML_WIDTH), fwd_t(ML_WIDTH), fwd_t(N_GATES)]
                    + [rev(ML_WIDTH), rev_t(ML_WIDTH), rev_t(N_GATES)])
        out_shape = [jax.ShapeDtypeStruct(c0.shape, F32), jax.ShapeDtypeStruct(m0.shape, F32)]
        out_specs = [c_spec, m_spec]
    return pl.pallas_call(
        functools.partial(_mlstm_body, with_output=with_output),
        out_shape=out_shape,
        grid=(nsteps,),
        in_specs=in_specs + [c_spec, m_spec],
        out_specs=out_specs,
        scratch_shapes=[pltpu.VMEM((bsz, 2, ML_HEADS // 2, hd + ML_AUG_ROWS, 2 * hd), F32),
                        pltpu.VMEM((bsz, 2, ML_HEADS, V7X_SUBLANES, V7X_LANES), F32)],
        compiler_params=_params(1),
        name="mlstm_scan_latent" if with_output else "mlstm_scan_context",
    )(*args)


def _gelu(x):
    k2 = 2.0 * (2.0 / np.pi) ** 0.5
    z = x * (k2 + (k2 * 0.044715) * (x * x))
    return x * jax.nn.sigmoid(z)


def _sg_body(x_ref, mod_ref, g_ref, win_ref, lng_ref, lnb_ref, ws_ref, bs_ref, wout_ref, o_ref, v_scr,
             *, d, width, tm):
    x = x_ref[...]
    mod = mod_ref[...]
    h = _modulated(x, g_ref[...], mod, d).astype(BF16)
    gw = width // SG_GROUPS
    groups = [slice(g * gw, (g + 1) * gw) for g in range(SG_GROUPS)]
    order = [width + cs.start for cs in groups] + [cs.start for cs in groups]
    project = lambda k: _dot(h, win_ref[:, order[k]:order[k] + gw])
    pre = project(0)
    total = None
    for k, cs in enumerate(groups):
        nxt = project(k + 1)
        vg = _gelu(pre)
        v_scr[:, cs] = vg
        part = jnp.sum(vg, axis=-1, keepdims=True)
        total = part if total is None else total + part
        pre = nxt
    mu = total * (1.0 / width)
    sq = None
    for cs in groups:
        vc = v_scr[:, cs] - mu
        part = jnp.sum(vc * vc, axis=-1, keepdims=True)
        sq = part if sq is None else sq + part
    rstd = lax.rsqrt(sq * (1.0 / width) + EPS)
    bs = bs_ref[...]
    y = None
    for g, cs in enumerate(groups):
        nxt = project(SG_GROUPS + g + 1) if g + 1 < SG_GROUPS else None
        vn = ((v_scr[:, cs] - mu) * rstd * lng_ref[:, cs] + lnb_ref[:, cs]).astype(BF16)
        u = _gelu(pre)
        pre = nxt
        parts = []
        for c in range(tm // SG_CHUNK):
            rs = slice(c * SG_CHUNK, (c + 1) * SG_CHUNK)
            mixed = _dot(ws_ref[g], vn[rs]) + bs[:, g:g + 1]
            parts.append((u[rs] * mixed).astype(BF16))
        yg = _dot(jnp.concatenate(parts, axis=0), wout_ref[cs, :])
        y = yg if y is None else y + yg
    o_ref[...] = x + mod[:, 2 * d:] * y


def _spatial_gating(x, mod3, g, w_in, ln_g, ln_b, w_s, b_s, w_out, tm=1024):
    bsz, n, d = x.shape
    width = w_out.shape[0]
    tm = min(tm, n)
    w_in, w_s, w_out = w_in.astype(BF16), w_s.astype(BF16), w_out.astype(BF16)
    const = lambda shape: pl.BlockSpec(shape, lambda b, i: (0,) * len(shape), pipeline_mode=pl.Buffered(1))
    return pl.pallas_call(
        functools.partial(_sg_body, d=d, width=width, tm=tm),
        out_shape=jax.ShapeDtypeStruct(x.shape, F32),
        grid=(bsz, n // tm),
        in_specs=[
            pl.BlockSpec((None, tm, d), lambda b, i: (b, i, 0)),
            pl.BlockSpec((None, 1, 3 * d), lambda b, i: (b, 0, 0)),
            const((1, d)), const(w_in.shape), const((1, width)), const((1, width)),
            const(w_s.shape), const((SG_CHUNK, SG_GROUPS)), const(w_out.shape),
        ],
        out_specs=pl.BlockSpec((None, tm, d), lambda b, i: (b, i, 0)),
        scratch_shapes=[pltpu.VMEM((tm, width), F32)],
        compiler_params=_params(2),
        name="spatial_gating",
    )(x, mod3, g.reshape(1, d), w_in, ln_g.reshape(1, width), ln_b.reshape(1, width), w_s, b_s.T, w_out)


def _rope_tables(n):
    hd = ML_HEAD_DIM
    n_pairs = hd // 4
    inv_freq = ROPE_THETA ** (-jnp.arange(n_pairs, dtype=F32) / n_pairs)
    row_ang = jnp.arange(n // GRID_W, dtype=F32)[:, None] * inv_freq
    col_ang = jnp.arange(GRID_W, dtype=F32)[:, None] * inv_freq

    def table(ang, is_row):
        zero = jnp.zeros_like(ang)
        cos, sin = jnp.cos(ang), jnp.sin(ang)
        half = lambda t: jnp.concatenate([t, zero] if is_row else [zero, t], axis=-1)
        return jnp.stack([jnp.concatenate([half(cos), half(cos)], axis=-1),
                          jnp.concatenate([half(-sin), half(sin)], axis=-1)])

    return table(row_ang, True), table(col_ang, False)


def _even_weights(w_in, gate_b):
    d = w_in.shape[0]
    hd = ML_HEAD_DIM
    base = 3 * NA_WIDTH
    deint = lambda w: w.reshape(d, ML_HEADS, hd // 2, 2).transpose(0, 1, 3, 2).reshape(d, ML_WIDTH)
    qb = deint(w_in[:, base:base + ML_WIDTH])
    kb = deint(w_in[:, base + ML_WIDTH:base + 2 * ML_WIDTH])
    w_main = jnp.concatenate([w_in[:, :base], qb, kb, w_in[:, base + 3 * ML_WIDTH:base + 4 * ML_WIDTH]],
                             axis=1).astype(BF16)
    w_vt = w_in[:, base + 2 * ML_WIDTH:base + 3 * ML_WIDTH].T.astype(BF16)
    wg = w_in[:, base + 4 * ML_WIDTH:]
    w_gc = jnp.pad(wg, ((0, 0), (0, V7X_LANES - N_GATES))).astype(BF16)
    w_gr = wg.T.astype(BF16)
    gb = gate_b.reshape(N_GATES).astype(F32)
    gb_col = jnp.pad(gb, (0, V7X_LANES - N_GATES)).reshape(1, V7X_LANES)
    gb_row = gb.reshape(N_GATES, 1)
    return w_main, w_vt, w_gc, w_gr, gb_col, gb_row


def kernel(x, c, ctx, c_ctx, w_mod, b_mod, norm_g, ffn_w_in, ffn_w_out, mix_w_in, na_rpb, ml_gate_b, ml_head_g,
           mix_w_out, sg_w_in, sg_ln_g, sg_ln_b, sg_w_s, sg_b_s, sg_w_out, final_g):
    bsz, n, d = x.shape
    depth = w_mod.shape[0]
    ctx_row = bsz
    cvec = jnp.zeros((V7X_SUBLANES, d), F32).at[:bsz].set(c).at[ctx_row].set(c_ctx)
    mod = _mod_vectors(cvec, w_mod, b_mod).reshape(depth, V7X_SUBLANES, 3, 1, 3 * d)
    batch_row = lambda b: b
    context_row = lambda b: ctx_row
    last_ctx_layer = ((depth - 1) // 2) * 2
    ffn_order = [(l, half) for l in range(depth) for half in range(2)]
    following = lambda l, half: dict(zip(ffn_order, ffn_order[1:])).get((l, half))
    cast_of = lambda nxt: None if nxt is None else (ffn_w_in, ffn_w_out) + nxt
    w_now = (ffn_w_in[0, 0].astype(BF16), ffn_w_out[0, 0].astype(BF16))
    xc = ctx
    for l in range(depth):
        ctx_in = l <= last_ctx_layer
        ctx_out = l < last_ctx_layer
        if ctx_in:
            xc = _ffn(xc, mod[l, :, 0], context_row, norm_g[l, 0], *w_now)
        x, w_now = _ffn(x, mod[l, :, 0], batch_row, norm_g[l, 0], *w_now, cast_next=cast_of(following(l, 0)))
        if l % 2 == 0:
            e = l // 2
            weights = _even_weights(mix_w_in[e], ml_gate_b[e])
            qa, ka, va, qb, kb, vt, ob, gcol, grow = _even_projection(
                x, mod[l, :, 1], batch_row, norm_g[l, 1], weights, _rope_tables(n))
            _, kax, vax, _, kbx, vtx, _, _, growx = _even_projection(
                xc, mod[l, :, 1], context_row, norm_g[l, 1], weights, None)
            bias = _na_bias_tables(na_rpb[e], n // GRID_W)
            ya = _neighbourhood_attention(qa, ka, va, kax, vax, bias)
            c0 = jnp.zeros((bsz, 2, ML_HEADS // 2, ML_HEAD_DIM + ML_AUG_ROWS, 2 * ML_HEAD_DIM), F32)
            m0 = jnp.zeros((bsz, 2, ML_HEADS, V7X_SUBLANES, V7X_LANES), F32)
            c1, m1 = _mlstm_scan(None, kbx, vtx, None, growx, c0, m0, False)
            hf, hr = _mlstm_scan(qb, kb, vt, gcol, grow, c1, m1, True)
            mixer = (mod[l, :, 1], ya, hf, hr, ob, ml_head_g[e], mix_w_out[e].astype(BF16))
            assert not ctx_out, "context output path is not needed for this depth"
        else:
            o = l // 2
            mixer = None
            x = _spatial_gating(x, mod[l, :, 1], norm_g[l, 1], sg_w_in[o], sg_ln_g[o], sg_ln_b[o],
                                sg_w_s[o], sg_b_s[o], sg_w_out[o])
        fg = final_g if l == depth - 1 else None
        nxt = following(l, 1)
        out = _ffn(x, mod[l, :, 2], batch_row, norm_g[l, 2], *w_now, final_g=fg, mixer=mixer, cast_next=cast_of(nxt))
        x, w_now = out if nxt is not None else (out, None)
    return x
```

```python
import functools

import numpy as np
import jax
import jax.numpy as jnp
from jax import lax
from jax.experimental import pallas as pl
from jax.experimental.pallas import tpu as pltpu

GRID_W = 64
NA_HEADS = 8
NA_HEAD_DIM = 64
NA_KH = 8
NA_KW = 16
ML_HEADS = 4
ML_HEAD_DIM = 128
ML_CHUNK = 128
ROPE_THETA = 10000.0
SG_CHUNK = 128
SG_GROUPS = 8
EPS = 1e-6
NA_WIDTH = NA_HEADS * NA_HEAD_DIM
ML_WIDTH = ML_HEADS * ML_HEAD_DIM
N_GATES = 4 * ML_HEADS
ML_AUG_ROWS = 16
ML_CHUNKS_PER_STEP = 1

V7X_LANES = 128
V7X_SUBLANES = 8
V7X_MXU_COLUMNS = 256
V7X_VMEM_LIMIT_BYTES = 56 * 1024 * 1024

NEG_BIG = -1e30
FFN_TOKENS_PER_STEP = 1024
V7X_BF16_SUBLANES = 16
NA_ROWS_PER_STEP = 32
NA_SUB_ROWS = 4
NA_BAND_ROWS = 12

BF16 = jnp.bfloat16
F32 = jnp.float32


def _dot(a, b):
    return jnp.dot(a, b, preferred_element_type=F32)


def _dot_nt(a, b):
    return lax.dot_general(a, b, (((1,), (1,)), ((), ())), preferred_element_type=F32)


def _params(n_axes):
    return pltpu.CompilerParams(
        dimension_semantics=("arbitrary",) * n_axes,
        vmem_limit_bytes=V7X_VMEM_LIMIT_BYTES,
    )


def _modulated(x, g, mod, d):
    shift = mod[:, :d]
    scale = mod[:, d:2 * d]
    gs = g * (1.0 + scale)
    ms = jnp.mean(x * x, axis=-1, keepdims=True)
    return x * lax.rsqrt(ms + EPS) * gs + shift


def _mod_body(c_ref, w_ref, b_ref, o_ref):
    c = c_ref[...]
    s = (c * jax.nn.sigmoid(c)).astype(BF16)
    o_ref[...] = _dot(s, w_ref[...].astype(BF16)) + b_ref[...]


def _mod_vectors(cvec, w_mod, b_mod):
    depth, d, width = w_mod.shape
    tn = width // 8
    return pl.pallas_call(
        _mod_body,
        out_shape=jax.ShapeDtypeStruct((depth, V7X_SUBLANES, width), F32),
        grid=(depth, width // tn),
        in_specs=[
            pl.BlockSpec((V7X_SUBLANES, d), lambda l, j: (0, 0)),
            pl.BlockSpec((None, d, tn), lambda l, j: (l, 0, j)),
            pl.BlockSpec((None, 1, tn), lambda l, j: (l, 0, j)),
        ],
        out_specs=pl.BlockSpec((None, V7X_SUBLANES, tn), lambda l, j: (l, 0, j)),
        compiler_params=_params(2),
        name="mod_vectors",
    )(cvec, w_mod, b_mod.reshape(depth, 1, width))


def _mixer_output(ya_ref, hf_ref, hr_ref, ob_ref, hg_ref, w_ref):
    hd = ML_HEAD_DIM
    hb = hf_ref[...] + hr_ref[...]
    ob = ob_ref[...]
    hg = hg_ref[...]
    y = _dot(ya_ref[...], w_ref[0:NA_WIDTH, :])
    ybs = []
    for hh in range(ML_HEADS):
        sl = slice(hh * hd, (hh + 1) * hd)
        hs = hb[:, sl]
        ms = jnp.mean(hs * hs, axis=-1, keepdims=True)
        ybs.append((hs * lax.rsqrt(ms + EPS) * hg[:, sl] * jax.nn.sigmoid(ob[:, sl])).astype(BF16))
    return y + _dot(jnp.concatenate(ybs, axis=1), w_ref[NA_WIDTH:, :])


def _ffn_body(*refs, d, d_ff, chunk, final, mixer, cast_next):
    refs = list(refs)
    if cast_next:
        next_in_ref, next_out_ref, o_ref, cast_in_ref, cast_out_ref = refs[-5:]
        cast_in_ref[...] = next_in_ref[...].astype(BF16)
        cast_out_ref[...] = next_out_ref[...].astype(BF16)
        refs = refs[:-5]
    else:
        o_ref = refs.pop()
    x_ref, mod_ref, g_ref, win_ref, wout_ref = refs[:5]
    rest = refs[5:]
    x = x_ref[...]
    if mixer:
        mixmod_ref = rest[0]
        x = x + mixmod_ref[...][:, 2 * d:] * _mixer_output(*rest[1:7])
        rest = rest[7:]
    if final:
        (fg_ref,) = rest
    mod = mod_ref[...]
    h = _modulated(x, g_ref[...], mod, d).astype(BF16)
    acc = None
    for c in range(d_ff // chunk):
        a = _dot(h, win_ref[:, c * chunk:(c + 1) * chunk])
        b = _dot(h, win_ref[:, d_ff + c * chunk:d_ff + (c + 1) * chunk])
        t = (a * jax.nn.sigmoid(a) * b).astype(BF16)
        y = _dot(t, wout_ref[c * chunk:(c + 1) * chunk, :])
        acc = y if acc is None else acc + y
    out = x + (0.5 * mod[:, 2 * d:]) * acc
    if final:
        ms = jnp.mean(out * out, axis=-1, keepdims=True)
        out = out * lax.rsqrt(ms + EPS) * fg_ref[...]
    o_ref[...] = out


def _ffn(x, mod3, mod_row, g, w_in, w_out, final_g=None, mixer=None, cast_next=None, tm=None):
    bsz, n, d = x.shape
    d_ff = w_out.shape[-2]
    if tm is None:
        tm = FFN_TOKENS_PER_STEP // 2 if mixer is not None else FFN_TOKENS_PER_STEP
    tm = min(tm, n)
    steps = n // tm
    final = final_g is not None
    tok_spec = lambda width: pl.BlockSpec((None, tm, width), lambda b, i: (b, i, 0))
    resident = dict(pipeline_mode=pl.Buffered(1))
    in_specs = [
        tok_spec(d),
        pl.BlockSpec((None, 1, 3 * d), lambda b, i: (mod_row(b), 0, 0)),
        pl.BlockSpec((1, d), lambda b, i: (0, 0)),
        pl.BlockSpec((d, 2 * d_ff), lambda b, i: (0, 0), **resident),
        pl.BlockSpec((None,) * (w_out.ndim - 2) + (d_ff, d), lambda b, i: (0,) * w_out.ndim, **resident),
    ]
    args = [x, mod3, g.reshape(1, d), w_in, w_out]
    if mixer is not None:
        mix_mod3, ya, hf, hr, ob, head_g, w_mix = mixer
        in_specs += [
            pl.BlockSpec((None, 1, 3 * d), lambda b, i: (mod_row(b), 0, 0)),
            tok_spec(NA_WIDTH), tok_spec(ML_WIDTH), tok_spec(ML_WIDTH), tok_spec(ML_WIDTH),
            pl.BlockSpec((1, ML_WIDTH), lambda b, i: (0, 0)),
            pl.BlockSpec(w_mix.shape, lambda b, i: (0, 0), **resident),
        ]
        args += [mix_mod3, ya, hf, hr, ob, head_g.reshape(1, ML_WIDTH), w_mix]
    if final:
        in_specs.append(pl.BlockSpec((1, d), lambda b, i: (0, 0)))
        args.append(final_g.reshape(1, d))
    out_shape = [jax.ShapeDtypeStruct(x.shape, F32)]
    out_specs = [tok_spec(d)]
    if cast_next is not None:
        w_in_all, w_out_all, layer, half = cast_next
        rows_in = d // (bsz * steps)
        assert rows_in * bsz * steps == d and rows_in % V7X_BF16_SUBLANES == 0, (d, bsz, steps)
        tiles = d_ff // V7X_BF16_SUBLANES
        assert tiles * V7X_BF16_SUBLANES == d_ff, d_ff
        blocks = max(k for k in range(1, steps + 1) if tiles % k == 0)
        rows_out = d_ff // blocks
        in_row = lambda b, i: b * steps + i
        out_row = lambda b, i: jnp.minimum(i, blocks - 1)
        in_specs += [
            pl.BlockSpec((None, None, rows_in, 2 * d_ff), lambda b, i: (layer, half, in_row(b, i), 0)),
            pl.BlockSpec((None, None, rows_out, d), lambda b, i: (layer, half, out_row(b, i), 0)),
        ]
        args += [w_in_all, w_out_all]
        out_shape += [jax.ShapeDtypeStruct((d, 2 * d_ff), BF16), jax.ShapeDtypeStruct((bsz, d_ff, d), BF16)]
        out_specs += [pl.BlockSpec((rows_in, 2 * d_ff), lambda b, i: (in_row(b, i), 0)),
                      pl.BlockSpec((None, rows_out, d), lambda b, i: (b, out_row(b, i), 0))]
    outs = pl.pallas_call(
        functools.partial(_ffn_body, d=d, d_ff=d_ff, chunk=V7X_MXU_COLUMNS, final=final,
                          mixer=mixer is not None, cast_next=cast_next is not None),
        out_shape=out_shape,
        grid=(bsz, steps),
        in_specs=in_specs,
        out_specs=out_specs,
        compiler_params=_params(2),
        name="macaron_ffn",
    )(*args)
    return (outs[0], (outs[1], outs[2])) if cast_next is not None else outs[0]


def _log_gates(z, is_forget):
    ls = jnp.minimum(z, 0.0) - jnp.log1p(jnp.exp(-jnp.abs(z)))
    return jnp.where(is_forget, ls, z)


def _split3(x):
    hi = x.astype(BF16)
    r1 = x - hi.astype(F32)
    mid = r1.astype(BF16)
    lo = (r1 - mid.astype(F32)).astype(BF16)
    return [hi, mid, lo]


def _evenproj_body(x_ref, mod_ref, g_ref, w_ref, wvt_ref, wg_ref, wgt_ref, gbc_ref, gbr_ref, rowt_ref, colt_ref,
                   qa_ref, ka_ref, va_ref, qb_ref, kb_ref, vt_ref, ob_ref, gc_ref, gr_ref, *, d, rope):
    h = _modulated(x_ref[...], g_ref[...], mod_ref[...], d).astype(BF16)
    gates = _gate_stages(h, wg_ref, wgt_ref, gbc_ref, gbr_ref, gc_ref, gr_ref)
    nw, mw, hd = NA_WIDTH, ML_WIDTH, ML_HEAD_DIM
    next(gates, None)
    qa_ref[...] = (_dot(h, w_ref[:, 0:nw]) * (NA_HEAD_DIM ** -0.5)).astype(BF16)
    next(gates, None)
    ka_ref[...] = _dot(h, w_ref[:, nw:2 * nw]).astype(BF16)
    next(gates, None)
    va_ref[...] = _dot(h, w_ref[:, 2 * nw:3 * nw]).astype(BF16)
    next(gates, None)
    base = 3 * nw
    qb = _dot(h, w_ref[:, base:base + mw])
    next(gates, None)
    kb = _dot(h, w_ref[:, base + mw:base + 2 * mw])
    next(gates, None)
    if rope:
        tm = qb.shape[0]
        expand = lambda t: jnp.broadcast_to(t[:, None, :], (tm // GRID_W, GRID_W, hd)).reshape(tm, hd)
        cos = expand(rowt_ref[0]) + colt_ref[0]
        sin = expand(rowt_ref[1]) + colt_ref[1]
        for hh in range(ML_HEADS):
            sl = slice(hh * hd, (hh + 1) * hd)
            qh = qb[:, sl]
            kh = kb[:, sl]
            qb_ref[:, sl] = (qh * cos + pltpu.roll(qh, hd // 2, 1) * sin).astype(BF16)
            kb_ref[:, sl] = ((kh * cos + pltpu.roll(kh, hd // 2, 1) * sin) * (hd ** -0.5)).astype(BF16)
    else:
        qb_ref[...] = qb.astype(BF16)
        kb_ref[...] = (kb * (hd ** -0.5)).astype(BF16)
    vt_ref[...] = _dot_nt(wvt_ref[...], h).astype(BF16)
    next(gates, None)
    ob_ref[...] = _dot(h, w_ref[:, base + 2 * mw:base + 3 * mw])
    for _ in gates:
        pass


def _gate_stages(h, wg_ref, wgt_ref, gbc_ref, gbr_ref, gc_ref, gr_ref):
    L = ML_CHUNK
    tm = h.shape[0]
    r_idx = lax.broadcasted_iota(jnp.int32, (L, L), 0)
    c_idx = lax.broadcasted_iota(jnp.int32, (L, L), 1)
    lower = (c_idx <= r_idx).astype(BF16)
    upper = (r_idx <= c_idx).astype(BF16)
    zc = _dot(h, wg_ref[...]) + gbc_ref[...]
    zr = _dot_nt(wgt_ref[...], h) + gbr_ref[...]
    yield
    lane = lax.broadcasted_iota(jnp.int32, zc.shape, 1)
    gates_c = _log_gates(zc, (lane % 2) == 1)
    split_c = jnp.concatenate(_split3(gates_c), axis=1)
    row = lax.broadcasted_iota(jnp.int32, zr.shape, 0)
    gates_r = _log_gates(zr, (row % 2) == 1)
    split_r = jnp.concatenate(_split3(gates_r), axis=0)
    yield
    bwd_lane = ((lax.broadcasted_iota(jnp.int32, (L, V7X_LANES), 1) // 2) % 2) == 1
    for c in range(tm // L):
        rs = slice(c * L, (c + 1) * L)
        r = _dot(lower, split_c[rs])
        prefix = r[:, 0:V7X_LANES] + r[:, V7X_LANES:2 * V7X_LANES] + r[:, 2 * V7X_LANES:]
        g = gates_c[rs]
        b = jnp.where(bwd_lane, prefix[L - 1:L, :] - prefix + g, prefix)
        gc_ref[rs, :] = g - pltpu.roll(b, V7X_LANES - 1, 1)
        if c % 2 == 1:
            yield
    row_l = lax.broadcasted_iota(jnp.int32, (N_GATES, L), 0)
    bwd_row = ((row_l // 2) % 2) == 1
    f_row = (row_l % 2) == 1
    for c in range(tm // L):
        ls = slice(c * L, (c + 1) * L)
        r = _dot(split_r[:, ls], upper)
        prefix = r[0:N_GATES] + r[N_GATES:2 * N_GATES] + r[2 * N_GATES:]
        g = gates_r[:, ls]
        b = jnp.where(bwd_row, prefix[:, L - 1:L] - prefix + g, prefix)
        gr_ref[:, ls] = jnp.where(f_row, b, g - pltpu.roll(b, N_GATES - 1, 0))
        if c % 2 == 1:
            yield


def _even_projection(x, mod3, mod_row, g, weights, rope_tables, tm=1024):
    w_main, w_vt, w_gc, w_gr, gb_col, gb_row = weights
    bsz, n, d = x.shape
    tm = min(tm, n)
    rope = rope_tables is not None
    if rope:
        row_t, col_t = rope_tables
        col_t = jnp.tile(col_t, (1, tm // GRID_W, 1))
    else:
        row_t = jnp.zeros((2, tm // GRID_W, ML_HEAD_DIM), F32)
        col_t = jnp.zeros((2, tm, ML_HEAD_DIM), F32)
    wm = w_main.shape[1]
    tok = lambda width, dt: jax.ShapeDtypeStruct((bsz, n, width), dt)
    tok_spec = lambda width: pl.BlockSpec((None, tm, width), lambda b, i: (b, i, 0))
    const = lambda shape: pl.BlockSpec(shape, lambda b, i: (0,) * len(shape))
    out_shape = [tok(NA_WIDTH, BF16)] * 3 + [tok(ML_WIDTH, BF16)] * 2 + [
        jax.ShapeDtypeStruct((bsz, ML_WIDTH, n), BF16), tok(ML_WIDTH, F32),
        tok(V7X_LANES, F32), jax.ShapeDtypeStruct((bsz, N_GATES, n), F32)]
    out_specs = [tok_spec(NA_WIDTH)] * 3 + [tok_spec(ML_WIDTH)] * 2 + [
        pl.BlockSpec((None, ML_WIDTH, tm), lambda b, i: (b, 0, i)), tok_spec(ML_WIDTH),
        tok_spec(V7X_LANES), pl.BlockSpec((None, N_GATES, tm), lambda b, i: (b, 0, i))]
    return pl.pallas_call(
        functools.partial(_evenproj_body, d=d, rope=rope),
        out_shape=out_shape,
        grid=(bsz, n // tm),
        in_specs=[
            tok_spec(d),
            pl.BlockSpec((None, 1, 3 * d), lambda b, i: (mod_row(b), 0, 0)),
            const((1, d)),
            const((d, wm)),
            const((ML_WIDTH, d)),
            const((d, V7X_LANES)),
            const((N_GATES, d)),
            const((1, V7X_LANES)),
            const((N_GATES, 1)),
            pl.BlockSpec((2, tm // GRID_W, ML_HEAD_DIM), lambda b, i: (0, i if rope else 0, 0)),
            pl.BlockSpec((2, tm, ML_HEAD_DIM), lambda b, i: (0, 0, 0)),
        ],
        out_specs=out_specs,
        compiler_params=_params(2),
        name="even_projection",
    )(x, mod3, g.reshape(1, d), w_main, w_vt, w_gc, w_gr, gb_col, gb_row, row_t, col_t)


def _na_body(q_ref, k_ref, v_ref, kx_ref, vx_ref, *rest, rows):
    rb = pl.program_id(2)
    sub, band = NA_SUB_ROWS, NA_BAND_ROWS
    n_sub = NA_ROWS_PER_STEP // sub
    tq = sub * GRID_W
    bias_refs, o_ref = rest[:n_sub], rest[n_sub]
    kx = kx_ref[...]
    vx = vx_ref[...]
    q = q_ref[...]
    lane = lax.broadcasted_iota(jnp.int32, q.shape, 1)
    bands = []
    for u in range(n_sub):
        band0 = jnp.clip((rb * n_sub + u) * sub - NA_KH // 2, 0, rows - band)
        start = pl.multiple_of(band0 * GRID_W, GRID_W)
        bands.append((k_ref[pl.ds(start, band * GRID_W), :], v_ref[pl.ds(start, band * GRID_W), :]))
    qms = [jnp.where((lane // NA_HEAD_DIM) == hh, q, jnp.zeros_like(q)) for hh in range(2)]
    items = [(hh, u) for hh in range(2) for u in range(n_sub)]

    def scores(item):
        hh, u = item
        qm = qms[hh][u * tq:(u + 1) * tq]
        return _dot_nt(qm, bands[u][0]) + bias_refs[u][hh], _dot_nt(qm, kx)

    def attend(item, s_loc, s_ctx):
        u = item[1]
        m = jnp.maximum(jnp.max(s_loc, axis=-1, keepdims=True), jnp.max(s_ctx, axis=-1, keepdims=True))
        p_loc = jnp.exp(s_loc - m)
        p_ctx = jnp.exp(s_ctx - m)
        denom = jnp.sum(p_loc, axis=-1, keepdims=True) + jnp.sum(p_ctx, axis=-1, keepdims=True)
        o = _dot(p_loc.astype(BF16), bands[u][1]) + _dot(p_ctx.astype(BF16), vx)
        return o * (1.0 / denom)

    outs = {}
    ready = scores(items[0])
    for k, item in enumerate(items):
        nxt = scores(items[k + 1]) if k + 1 < len(items) else None
        outs[item] = attend(item, *ready)
        ready = nxt
    per_head = [jnp.concatenate([outs[(hh, u)] for u in range(n_sub)], axis=0) for hh in range(2)]
    lane_o = lax.broadcasted_iota(jnp.int32, per_head[0].shape, 1)
    o_ref[...] = jnp.where(lane_o < NA_HEAD_DIM, per_head[0], per_head[1]).astype(BF16)


def _na_bias_tables(rpb, rows):
    sub, band, w = NA_SUB_ROWS, NA_BAND_ROWS, GRID_W
    kh, kw = NA_KH, NA_KW
    n_heads = rpb.shape[0]
    cols = np.arange(w)
    c0 = np.clip(cols - kw // 2, 0, w - kw)
    cc = np.arange(w)[None, :]
    col_ok = (cc >= c0[:, None]) & (cc < c0[:, None] + kw)
    padded = jnp.pad(rpb, ((0, 0), (0, 0), (w, w)))
    shifted = jnp.stack([padded[:, :, w + kw - 1 - j:2 * w + kw - 1 - j] for j in range(w)], axis=2)
    t1 = jnp.where(col_ok[None, None], shifted, NEG_BIG)
    masked = jnp.full((n_heads, w, w), NEG_BIG, F32)
    blocks = []
    for i0, b0 in ((0, 0), (sub, sub - kh // 2), (rows - sub, rows - band)):
        for qi in range(i0, i0 + sub):
            r0 = min(max(qi - kh // 2, 0), rows - kh)
            row = [t1[:, r - qi + kh - 1] if r0 <= r < r0 + kh else masked for r in range(b0, b0 + band)]
            blocks.append(jnp.concatenate(row, axis=-1))
    return jnp.concatenate(blocks, axis=1)


def _neighbourhood_attention(q, k, v, kx, vx, bias):
    bsz, n, _ = q.shape
    lc = kx.shape[1]
    rows = n // GRID_W
    sub, band = NA_SUB_ROWS, NA_BAND_ROWS
    n_sub = NA_ROWS_PER_STEP // sub
    nrb = rows // NA_ROWS_PER_STEP
    tq = NA_ROWS_PER_STEP * GRID_W
    pair = 2 * NA_HEAD_DIM
    last = rows // sub - 1

    def bias_spec(u):
        def index(b, hp, rb):
            sb = rb * n_sub + u
            return (hp, jnp.where(sb == 0, 0, jnp.where(sb == last, 2, 1)), 0)
        return pl.BlockSpec((2, sub * GRID_W, band * GRID_W), index)

    return pl.pallas_call(
        functools.partial(_na_body, rows=rows),
        out_shape=jax.ShapeDtypeStruct((bsz, n, NA_WIDTH), BF16),
        grid=(bsz, NA_HEADS // 2, nrb),
        in_specs=[
            pl.BlockSpec((None, tq, pair), lambda b, hp, rb: (b, rb, hp)),
            pl.BlockSpec((None, n, pair), lambda b, hp, rb: (b, 0, hp)),
            pl.BlockSpec((None, n, pair), lambda b, hp, rb: (b, 0, hp)),
            pl.BlockSpec((None, lc, pair), lambda b, hp, rb: (b, 0, hp)),
            pl.BlockSpec((None, lc, pair), lambda b, hp, rb: (b, 0, hp)),
        ] + [bias_spec(u) for u in range(n_sub)],
        out_specs=pl.BlockSpec((None, tq, pair), lambda b, hp, rb: (b, rb, hp)),
        compiler_params=_params(3),
        name="neighbourhood_attention",
    )(q, k, v, kx, vx, *([bias] * n_sub))


def _mlstm_body(*refs, with_output):
    if with_output:
        (qf_ref, kf_ref, vtf_ref, gcf_ref, grf_ref, qr_ref, kr_ref, vtr_ref, gcr_ref, grr_ref,
         c0_ref, m0_ref, hf_ref, hr_ref, c_scr, m_scr) = refs
    else:
        (kf_ref, vtf_ref, grf_ref, kr_ref, vtr_ref, grr_ref,
         c0_ref, m0_ref, c_out_ref, m_out_ref, c_scr, m_scr) = refs
    step = pl.program_id(0)
    n_steps = pl.num_programs(0)
    L, hd = ML_CHUNK, ML_HEAD_DIM
    bsz = kf_ref.shape[0]

    @pl.when(step == 0)
    def _():
        c_scr[...] = c0_ref[...]
        m_scr[...] = m0_ref[...]

    s_idx = lax.broadcasted_iota(jnp.int32, (L, L), 0)
    t_idx = lax.broadcasted_iota(jnp.int32, (L, L), 1)
    ones_rows = (lax.broadcasted_iota(jnp.int32, (ML_AUG_ROWS, L), 0) == 0).astype(BF16)
    fwd_refs = (kf_ref, vtf_ref, grf_ref) + ((qf_ref, gcf_ref, hf_ref) if with_output else (None,) * 3)
    bwd_refs = (kr_ref, vtr_ref, grr_ref) + ((qr_ref, gcr_ref, hr_ref) if with_output else (None,) * 3)
    cps = kf_ref.shape[1] // L
    items = [(sub, bi, d, p) for sub in range(cps) for bi in range(bsz) for d in range(2)
             for p in range(ML_HEADS // 2)]
    lane2 = lax.broadcasted_iota(jnp.int32, (L, 2 * hd), 1)
    zero_ll = jnp.zeros((L, L), BF16)

    def refs_of(item):
        _, bi, d, _ = item
        return tuple(r if r is None else r.at[bi] for r in (fwd_refs if d == 0 else bwd_refs))

    def tokens_of(item):
        sub, _, d, _ = item
        first = sub if d == 0 else cps - 1 - sub
        return slice(first * L, (first + 1) * L)

    def block_diag(x2):
        zero = jnp.zeros_like(x2)
        return jnp.concatenate([jnp.where(lane2 < hd, x2, zero), jnp.where(lane2 >= hd, x2, zero)], axis=0)

    def front(item):
        _, bi, d, p = item
        k_ref, _, _, q_ref, _, _ = refs_of(item)
        tok = tokens_of(item)
        sl2 = slice(2 * p * hd, (2 * p + 2) * hd)
        qbd = block_diag(q_ref[tok, sl2])
        return _dot_nt(k_ref[tok, sl2], qbd), _dot_nt(c_scr[bi, d, p].astype(BF16), qbd)

    def finish(item, fronts):
        _, bi, d, p = item
        k_ref, vt_ref, gr_ref, _, gc_ref, h_ref = refs_of(item)
        tok = tokens_of(item)
        visible = (s_idx <= t_idx) if d == 0 else (s_idx >= t_idx)
        end = L - 1 if d == 0 else 0
        sl2 = slice(2 * p * hd, (2 * p + 2) * hd)
        a_prevs, a_toks, vt_augs, ps, w_inters, w_corrs, inv_floors = [], [], [], [], [], [], []
        for j in range(2):
            hh = 2 * p + j
            gi = hh * 4 + d * 2
            c_row = gr_ref[gi:gi + 1, tok]
            b_row = gr_ref[gi + 1:gi + 2, tok]
            b_end = b_row[:, end:end + 1]
            m_old = m_scr[bi, d, hh][0:1, 0:1]
            m_new = jnp.maximum(b_end + m_old, b_end + jnp.max(c_row, axis=-1, keepdims=True))
            a_prevs.append(jnp.broadcast_to(jnp.exp(b_end + m_old - m_new), (1, hd)))
            a_toks.append(jnp.exp(b_end + c_row - m_new))
            vt_augs.append(jnp.concatenate([vt_ref[hh * hd:(hh + 1) * hd, tok], ones_rows], axis=0))
            m_scr[bi, d, hh] = jnp.broadcast_to(m_new, (V7X_SUBLANES, V7X_LANES))
            if with_output:
                st = fronts[0][:, j * L:(j + 1) * L]
                log_w = jnp.where(visible, b_row + gc_ref[tok, gi:gi + 1], NEG_BIG)
                m_intra = jnp.max(log_w, axis=0, keepdims=True)
                ps.append((st * jnp.exp(log_w - m_intra)).astype(BF16))
                log_inter = b_row + m_old
                m_t = jnp.maximum(log_inter, m_intra)
                w_inters.append(jnp.exp(log_inter - m_t))
                w_corrs.append(jnp.exp(m_intra - m_t))
                inv_floors.append(jnp.exp(-m_t))
        cat = lambda parts: jnp.concatenate(parts, axis=1)
        vt2 = cat(vt_augs)
        c_old = c_scr[bi, d, p]
        if with_output:
            pbd = jnp.concatenate([cat([ps[0], zero_ll]), cat([zero_ll, ps[1]])], axis=0)
            intra = _dot(vt2, pbd)
            both = cat(w_inters) * fronts[1] + cat(w_corrs) * intra
            den = both[hd:hd + 1, :]
            ht = both[:hd, :] * (1.0 / jnp.maximum(jnp.abs(den), cat(inv_floors)))
            for j in range(2):
                h_ref[tok, (2 * p + j) * hd:(2 * p + j + 1) * hd] = ht[:, j * L:(j + 1) * L].T
        av = (vt2.astype(F32) * cat(a_toks)).astype(BF16)
        c_scr[bi, d, p] = cat(a_prevs) * c_old + _dot(av, block_diag(k_ref[tok, sl2]))

    ready = front(items[0]) if with_output else None
    for i, item in enumerate(items):
        nxt = front(items[i + 1]) if with_output and i + 1 < len(items) else None
        finish(item, ready)
        ready = nxt

    if not with_output:
        @pl.when(step == n_steps - 1)
        def _():
            c_out_ref[...] = c_scr[...]
            m_out_ref[...] = m_scr[...]


def _mlstm_scan(q, k, vt, gcol, grow, c0, m0, with_output):
    bsz, n, _ = k.shape
    L, hd = ML_CHUNK, ML_HEAD_DIM
    span = min(ML_CHUNKS_PER_STEP, n // L) * L
    nsteps = n // span
    fwd = lambda width: pl.BlockSpec((bsz, span, width), lambda c: (0, c, 0))
    rev = lambda width: pl.BlockSpec((bsz, span, width), lambda c: (0, nsteps - 1 - c, 0))
    fwd_t = lambda height: pl.BlockSpec((bsz, height, span), lambda c: (0, 0, c))
    rev_t = lambda height: pl.BlockSpec((bsz, height, span), lambda c: (0, 0, nsteps - 1 - c))
    c_spec = pl.BlockSpec((bsz, 2, ML_HEADS // 2, hd + ML_AUG_ROWS, 2 * hd), lambda c: (0, 0, 0, 0, 0))
    m_spec = pl.BlockSpec((bsz, 2, ML_HEADS, V7X_SUBLANES, V7X_LANES), lambda c: (0, 0, 0, 0, 0))
    if with_output:
        args = [q, k, vt, gcol, grow, q, k, vt, gcol, grow, c0, m0]
        in_specs = ([fwd(ML_WIDTH)] * 2 + [fwd_t(ML_WIDTH), fwd(V7X_LANES), fwd_t(N_GATES)]
                    + [rev(ML_WIDTH)] * 2 + [rev_t(ML_WIDTH), rev(V7X_LANES), rev_t(N_GATES)])
        out_shape = [jax.ShapeDtypeStruct((bsz, n, ML_WIDTH), F32)] * 2
        out_specs = [fwd(ML_WIDTH), rev(ML_WIDTH)]
    else:
        args = [k, vt, grow, k, vt, grow, c0, m0]
        in_specs = ([fwd(ML_WIDTH), fwd_t(ML_WIDTH), fwd_t(N_GATES)]
                    + [rev(ML_WIDTH), rev_t(ML_WIDTH), rev_t(N_GATES)])
        out_shape = [jax.ShapeDtypeStruct(c0.shape, F32), jax.ShapeDtypeStruct(m0.shape, F32)]
        out_specs = [c_spec, m_spec]
    return pl.pallas_call(
        functools.partial(_mlstm_body, with_output=with_output),
        out_shape=out_shape,
        grid=(nsteps,),
        in_specs=in_specs + [c_spec, m_spec],
        out_specs=out_specs,
        scratch_shapes=[pltpu.VMEM((bsz, 2, ML_HEADS // 2, hd + ML_AUG_ROWS, 2 * hd), F32),
                        pltpu.VMEM((bsz, 2, ML_HEADS, V7X_SUBLANES, V7X_LANES), F32)],
        compiler_params=_params(1),
        name="mlstm_scan_latent" if with_output else "mlstm_scan_context",
    )(*args)


def _gelu(x):
    k2 = 2.0 * (2.0 / np.pi) ** 0.5
    z = x * (k2 + (k2 * 0.044715) * (x * x))
    return x * jax.nn.sigmoid(z)


def _sg_body(x_ref, mod_ref, g_ref, win_ref, lng_ref, lnb_ref, ws_ref, bs_ref, wout_ref, o_ref, v_scr,
             *, d, width, tm):
    x = x_ref[...]
    mod = mod_ref[...]
    h = _modulated(x, g_ref[...], mod, d).astype(BF16)
    gw = width // SG_GROUPS
    groups = [slice(g * gw, (g + 1) * gw) for g in range(SG_GROUPS)]
    order = [width + cs.start for cs in groups] + [cs.start for cs in groups]
    project = lambda k: _dot(h, win_ref[:, order[k]:order[k] + gw])
    pre = project(0)
    total = None
    for k, cs in enumerate(groups):
        nxt = project(k + 1)
        vg = _gelu(pre)
        v_scr[:, cs] = vg
        part = jnp.sum(vg, axis=-1, keepdims=True)
        total = part if total is None else total + part
        pre = nxt
    mu = total * (1.0 / width)
    sq = None
    for cs in groups:
        vc = v_scr[:, cs] - mu
        part = jnp.sum(vc * vc, axis=-1, keepdims=True)
        sq = part if sq is None else sq + part
    rstd = lax.rsqrt(sq * (1.0 / width) + EPS)
    bs = bs_ref[...]
    y = None
    for g, cs in enumerate(groups):
        nxt = project(SG_GROUPS + g + 1) if g + 1 < SG_GROUPS else None
        vn = ((v_scr[:, cs] - mu) * rstd * lng_ref[:, cs] + lnb_ref[:, cs]).astype(BF16)
        u = _gelu(pre)
        pre = nxt
        parts = []
        for c in range(tm // SG_CHUNK):
            rs = slice(c * SG_CHUNK, (c + 1) * SG_CHUNK)
            mixed = _dot(ws_ref[g], vn[rs]) + bs[:, g:g + 1]
            parts.append((u[rs] * mixed).astype(BF16))
        yg = _dot(jnp.concatenate(parts, axis=0), wout_ref[cs, :])
        y = yg if y is None else y + yg
    o_ref[...] = x + mod[:, 2 * d:] * y


def _spatial_gating(x, mod3, g, w_in, ln_g, ln_b, w_s, b_s, w_out, tm=1024):
    bsz, n, d = x.shape
    width = w_out.shape[0]
    tm = min(tm, n)
    w_in, w_s, w_out = w_in.astype(BF16), w_s.astype(BF16), w_out.astype(BF16)
    const = lambda shape: pl.BlockSpec(shape, lambda b, i: (0,) * len(shape), pipeline_mode=pl.Buffered(1))
    return pl.pallas_call(
        functools.partial(_sg_body, d=d, width=width, tm=tm),
        out_shape=jax.ShapeDtypeStruct(x.shape, F32),
        grid=(bsz, n // tm),
        in_specs=[
            pl.BlockSpec((None, tm, d), lambda b, i: (b, i, 0)),
            pl.BlockSpec((None, 1, 3 * d), lambda b, i: (b, 0, 0)),
            const((1, d)), const(w_in.shape), const((1, width)), const((1, width)),
            const(w_s.shape), const((SG_CHUNK, SG_GROUPS)), const(w_out.shape),
        ],
        out_specs=pl.BlockSpec((None, tm, d), lambda b, i: (b, i, 0)),
        scratch_shapes=[pltpu.VMEM((tm, width), F32)],
        compiler_params=_params(2),
        name="spatial_gating",
    )(x, mod3, g.reshape(1, d), w_in, ln_g.reshape(1, width), ln_b.reshape(1, width), w_s, b_s.T, w_out)


def _rope_tables(n):
    hd = ML_HEAD_DIM
    n_pairs = hd // 4
    inv_freq = ROPE_THETA ** (-jnp.arange(n_pairs, dtype=F32) / n_pairs)
    row_ang = jnp.arange(n // GRID_W, dtype=F32)[:, None] * inv_freq
    col_ang = jnp.arange(GRID_W, dtype=F32)[:, None] * inv_freq

    def table(ang, is_row):
        zero = jnp.zeros_like(ang)
        cos, sin = jnp.cos(ang), jnp.sin(ang)
        half = lambda t: jnp.concatenate([t, zero] if is_row else [zero, t], axis=-1)
        return jnp.stack([jnp.concatenate([half(cos), half(cos)], axis=-1),
                          jnp.concatenate([half(-sin), half(sin)], axis=-1)])

    return table(row_ang, True), table(col_ang, False)


def _even_weights(w_in, gate_b):
    d = w_in.shape[0]
    hd = ML_HEAD_DIM
    base = 3 * NA_WIDTH
    w_in = w_in.astype(BF16)
    deint = lambda w: w.reshape(d, ML_HEADS, hd // 2, 2).transpose(0, 1, 3, 2).reshape(d, ML_WIDTH)
    qb = deint(w_in[:, base:base + ML_WIDTH])
    kb = deint(w_in[:, base + ML_WIDTH:base + 2 * ML_WIDTH])
    w_main = jnp.concatenate([w_in[:, :base], qb, kb, w_in[:, base + 3 * ML_WIDTH:base + 4 * ML_WIDTH]],
                             axis=1).astype(BF16)
    w_vt = w_in[:, base + 2 * ML_WIDTH:base + 3 * ML_WIDTH].T.astype(BF16)
    wg = w_in[:, base + 4 * ML_WIDTH:]
    w_gc = jnp.pad(wg, ((0, 0), (0, V7X_LANES - N_GATES))).astype(BF16)
    w_gr = wg.T.astype(BF16)
    gb = gate_b.reshape(N_GATES).astype(F32)
    gb_col = jnp.pad(gb, (0, V7X_LANES - N_GATES)).reshape(1, V7X_LANES)
    gb_row = gb.reshape(N_GATES, 1)
    return w_main, w_vt, w_gc, w_gr, gb_col, gb_row


def kernel(x, c, ctx, c_ctx, w_mod, b_mod, norm_g, ffn_w_in, ffn_w_out, mix_w_in, na_rpb, ml_gate_b, ml_head_g,
           mix_w_out, sg_w_in, sg_ln_g, sg_ln_b, sg_w_s, sg_b_s, sg_w_out, final_g):
    bsz, n, d = x.shape
    depth = w_mod.shape[0]
    ctx_row = bsz
    cvec = jnp.zeros((V7X_SUBLANES, d), F32).at[:bsz].set(c).at[ctx_row].set(c_ctx)
    mod = _mod_vectors(cvec, w_mod, b_mod).reshape(depth, V7X_SUBLANES, 3, 1, 3 * d)
    batch_row = lambda b: b
    context_row = lambda b: ctx_row
    last_ctx_layer = ((depth - 1) // 2) * 2
    ffn_order = [(l, half) for l in range(depth) for half in range(2)]
    following = lambda l, half: dict(zip(ffn_order, ffn_order[1:])).get((l, half))
    cast_of = lambda nxt: None if nxt is None else (ffn_w_in, ffn_w_out) + nxt
    w_now = (ffn_w_in[0, 0].astype(BF16), ffn_w_out[0, 0].astype(BF16))
    xc = ctx
    for l in range(depth):
        ctx_in = l <= last_ctx_layer
        ctx_out = l < last_ctx_layer
        if ctx_in:
            xc = _ffn(xc, mod[l, :, 0], context_row, norm_g[l, 0], *w_now)
        x, w_now = _ffn(x, mod[l, :, 0], batch_row, norm_g[l, 0], *w_now, cast_next=cast_of(following(l, 0)))
        if l % 2 == 0:
            e = l // 2
            weights = _even_weights(mix_w_in[e], ml_gate_b[e])
            qa, ka, va, qb, kb, vt, ob, gcol, grow = _even_projection(
                x, mod[l, :, 1], batch_row, norm_g[l, 1], weights, _rope_tables(n))
            _, kax, vax, _, kbx, vtx, _, _, growx = _even_projection(
                xc, mod[l, :, 1], context_row, norm_g[l, 1], weights, None)
            bias = _na_bias_tables(na_rpb[e], n // GRID_W)
            ya = _neighbourhood_attention(qa, ka, va, kax, vax, bias)
            c0 = jnp.zeros((bsz, 2, ML_HEADS // 2, ML_HEAD_DIM + ML_AUG_ROWS, 2 * ML_HEAD_DIM), F32)
            m0 = jnp.zeros((bsz, 2, ML_HEADS, V7X_SUBLANES, V7X_LANES), F32)
            c1, m1 = _mlstm_scan(None, kbx, vtx, None, growx, c0, m0, False)
            hf, hr = _mlstm_scan(qb, kb, vt, gcol, grow, c1, m1, True)
            mixer = (mod[l, :, 1], ya, hf, hr, ob, ml_head_g[e], mix_w_out[e].astype(BF16))
            assert not ctx_out, "context output path is not needed for this depth"
        else:
            o = l // 2
            mixer = None
            x = _spatial_gating(x, mod[l, :, 1], norm_g[l, 1], sg_w_in[o], sg_ln_g[o], sg_ln_b[o],
                                sg_w_s[o], sg_b_s[o], sg_w_out[o])
        fg = final_g if l == depth - 1 else None
        nxt = following(l, 1)
        out = _ffn(x, mod[l, :, 2], batch_row, norm_g[l, 2], *w_now, final_g=fg, mixer=mixer, cast_next=cast_of(nxt))
        x, w_now = out if nxt is not None else (out, None)
    return x
```

```python
import functools

import numpy as np
import jax
import jax.numpy as jnp
from jax import lax
from jax.experimental import pallas as pl
from jax.experimental.pallas import tpu as pltpu

GRID_W = 64
NA_HEADS = 8
NA_HEAD_DIM = 64
NA_KH = 8
NA_KW = 16
ML_HEADS = 4
ML_HEAD_DIM = 128
ML_CHUNK = 128
ROPE_THETA = 10000.0
SG_CHUNK = 128
SG_GROUPS = 8
EPS = 1e-6
NA_WIDTH = NA_HEADS * NA_HEAD_DIM
ML_WIDTH = ML_HEADS * ML_HEAD_DIM
N_GATES = 4 * ML_HEADS
ML_AUG_ROWS = 16
ML_CHUNKS_PER_STEP = 1

V7X_LANES = 128
V7X_SUBLANES = 8
V7X_MXU_COLUMNS = 256
V7X_VMEM_LIMIT_BYTES = 56 * 1024 * 1024

NEG_BIG = -1e30
FFN_TOKENS_PER_STEP = 1024
V7X_BF16_SUBLANES = 16
NA_ROWS_PER_STEP = 32
NA_SUB_ROWS = 4
NA_BAND_ROWS = 12

BF16 = jnp.bfloat16
F32 = jnp.float32


def _dot(a, b):
    return jnp.dot(a, b, preferred_element_type=F32)


def _dot_nt(a, b):
    return lax.dot_general(a, b, (((1,), (1,)), ((), ())), preferred_element_type=F32)


def _params(n_axes):
    return pltpu.CompilerParams(
        dimension_semantics=("arbitrary",) * n_axes,
        vmem_limit_bytes=V7X_VMEM_LIMIT_BYTES,
    )


def _modulated(x, g, mod, d):
    shift = mod[:, :d]
    scale = mod[:, d:2 * d]
    gs = g * (1.0 + scale)
    ms = jnp.mean(x * x, axis=-1, keepdims=True)
    return x * lax.rsqrt(ms + EPS) * gs + shift


def _mod_body(c_ref, w_ref, b_ref, o_ref):
    c = c_ref[...]
    s = (c * jax.nn.sigmoid(c)).astype(BF16)
    o_ref[...] = _dot(s, w_ref[...].astype(BF16)) + b_ref[...]


def _mod_vectors(cvec, w_mod, b_mod):
    depth, d, width = w_mod.shape
    tn = width // 4
    return pl.pallas_call(
        _mod_body,
        out_shape=jax.ShapeDtypeStruct((depth, V7X_SUBLANES, width), F32),
        grid=(depth, width // tn),
        in_specs=[
            pl.BlockSpec((V7X_SUBLANES, d), lambda l, j: (0, 0)),
            pl.BlockSpec((None, d, tn), lambda l, j: (l, 0, j)),
            pl.BlockSpec((None, 1, tn), lambda l, j: (l, 0, j)),
        ],
        out_specs=pl.BlockSpec((None, V7X_SUBLANES, tn), lambda l, j: (l, 0, j)),
        compiler_params=_params(2),
        name="mod_vectors",
    )(cvec, w_mod, b_mod.reshape(depth, 1, width))


def _mixer_output(ya_ref, hf_ref, hr_ref, ob_ref, hg_ref, w_ref):
    hd = ML_HEAD_DIM
    hb = hf_ref[...] + hr_ref[...]
    ob = ob_ref[...]
    hg = hg_ref[...]
    y = _dot(ya_ref[...], w_ref[0:NA_WIDTH, :])
    ybs = []
    for hh in range(ML_HEADS):
        sl = slice(hh * hd, (hh + 1) * hd)
        hs = hb[:, sl]
        ms = jnp.mean(hs * hs, axis=-1, keepdims=True)
        ybs.append((hs * lax.rsqrt(ms + EPS) * hg[:, sl] * jax.nn.sigmoid(ob[:, sl])).astype(BF16))
    return y + _dot(jnp.concatenate(ybs, axis=1), w_ref[NA_WIDTH:, :])


def _ffn_body(*refs, d, d_ff, chunk, final, mixer, cast_next):
    refs = list(refs)
    if cast_next:
        next_in_ref, next_out_ref, o_ref, cast_in_ref, cast_out_ref = refs[-5:]
        cast_in_ref[...] = next_in_ref[...].astype(BF16)
        cast_out_ref[...] = next_out_ref[...].astype(BF16)
        refs = refs[:-5]
    else:
        o_ref = refs.pop()
    x_ref, mod_ref, g_ref, win_ref, wout_ref = refs[:5]
    rest = refs[5:]
    x = x_ref[...]
    if mixer:
        mixmod_ref = rest[0]
        x = x + mixmod_ref[...][:, 2 * d:] * _mixer_output(*rest[1:7])
        rest = rest[7:]
    if final:
        (fg_ref,) = rest
    mod = mod_ref[...]
    h = _modulated(x, g_ref[...], mod, d).astype(BF16)
    acc = None
    for c in range(d_ff // chunk):
        a = _dot(h, win_ref[:, c * chunk:(c + 1) * chunk])
        b = _dot(h, win_ref[:, d_ff + c * chunk:d_ff + (c + 1) * chunk])
        t = (a * jax.nn.sigmoid(a) * b).astype(BF16)
        y = _dot(t, wout_ref[c * chunk:(c + 1) * chunk, :])
        acc = y if acc is None else acc + y
    out = x + (0.5 * mod[:, 2 * d:]) * acc
    if final:
        ms = jnp.mean(out * out, axis=-1, keepdims=True)
        out = out * lax.rsqrt(ms + EPS) * fg_ref[...]
    o_ref[...] = out


def _ffn(x, mod3, mod_row, g, w_in, w_out, final_g=None, mixer=None, cast_next=None, tm=None):
    bsz, n, d = x.shape
    d_ff = w_out.shape[-2]
    if tm is None:
        tm = FFN_TOKENS_PER_STEP // 2 if mixer is not None else FFN_TOKENS_PER_STEP
    tm = min(tm, n)
    steps = n // tm
    final = final_g is not None
    tok_spec = lambda width: pl.BlockSpec((None, tm, width), lambda b, i: (b, i, 0))
    resident = dict(pipeline_mode=pl.Buffered(1))
    in_specs = [
        tok_spec(d),
        pl.BlockSpec((None, 1, 3 * d), lambda b, i: (mod_row(b), 0, 0)),
        pl.BlockSpec((1, d), lambda b, i: (0, 0)),
        pl.BlockSpec((d, 2 * d_ff), lambda b, i: (0, 0), **resident),
        pl.BlockSpec((None,) * (w_out.ndim - 2) + (d_ff, d), lambda b, i: (0,) * w_out.ndim, **resident),
    ]
    args = [x, mod3, g.reshape(1, d), w_in, w_out]
    if mixer is not None:
        mix_mod3, ya, hf, hr, ob, head_g, w_mix = mixer
        in_specs += [
            pl.BlockSpec((None, 1, 3 * d), lambda b, i: (mod_row(b), 0, 0)),
            tok_spec(NA_WIDTH), tok_spec(ML_WIDTH), tok_spec(ML_WIDTH), tok_spec(ML_WIDTH),
            pl.BlockSpec((1, ML_WIDTH), lambda b, i: (0, 0)),
            pl.BlockSpec(w_mix.shape, lambda b, i: (0, 0), **resident),
        ]
        args += [mix_mod3, ya, hf, hr, ob, head_g.reshape(1, ML_WIDTH), w_mix]
    if final:
        in_specs.append(pl.BlockSpec((1, d), lambda b, i: (0, 0)))
        args.append(final_g.reshape(1, d))
    out_shape = [jax.ShapeDtypeStruct(x.shape, F32)]
    out_specs = [tok_spec(d)]
    if cast_next is not None:
        w_in_all, w_out_all, layer, half = cast_next
        rows_in = d // (bsz * steps)
        assert rows_in * bsz * steps == d and rows_in % V7X_BF16_SUBLANES == 0, (d, bsz, steps)
        tiles = d_ff // V7X_BF16_SUBLANES
        assert tiles * V7X_BF16_SUBLANES == d_ff, d_ff
        blocks = max(k for k in range(1, steps + 1) if tiles % k == 0)
        rows_out = d_ff // blocks
        in_row = lambda b, i: b * steps + i
        out_row = lambda b, i: jnp.minimum(i, blocks - 1)
        in_specs += [
            pl.BlockSpec((None, None, rows_in, 2 * d_ff), lambda b, i: (layer, half, in_row(b, i), 0)),
            pl.BlockSpec((None, None, rows_out, d), lambda b, i: (layer, half, out_row(b, i), 0)),
        ]
        args += [w_in_all, w_out_all]
        out_shape += [jax.ShapeDtypeStruct((d, 2 * d_ff), BF16), jax.ShapeDtypeStruct((bsz, d_ff, d), BF16)]
        out_specs += [pl.BlockSpec((rows_in, 2 * d_ff), lambda b, i: (in_row(b, i), 0)),
                      pl.BlockSpec((None, rows_out, d), lambda b, i: (b, out_row(b, i), 0))]
    outs = pl.pallas_call(
        functools.partial(_ffn_body, d=d, d_ff=d_ff, chunk=V7X_MXU_COLUMNS, final=final,
                          mixer=mixer is not None, cast_next=cast_next is not None),
        out_shape=out_shape,
        grid=(bsz, steps),
        in_specs=in_specs,
        out_specs=out_specs,
        compiler_params=_params(2),
        name="macaron_ffn",
    )(*args)
    return (outs[0], (outs[1], outs[2])) if cast_next is not None else outs[0]


def _log_gates(z, is_forget):
    ls = jnp.minimum(z, 0.0) - jnp.log1p(jnp.exp(-jnp.abs(z)))
    return jnp.where(is_forget, ls, z)


def _split3(x):
    hi = x.astype(BF16)
    r1 = x - hi.astype(F32)
    mid = r1.astype(BF16)
    lo = (r1 - mid.astype(F32)).astype(BF16)
    return [hi, mid, lo]


def _evenproj_body(x_ref, mod_ref, g_ref, w_ref, wvt_ref, wg_ref, wgt_ref, gbc_ref, gbr_ref, rowt_ref, colt_ref,
                   qa_ref, ka_ref, va_ref, qb_ref, kb_ref, vt_ref, ob_ref, gc_ref, gr_ref, *, d, rope):
    h = _modulated(x_ref[...], g_ref[...], mod_ref[...], d).astype(BF16)
    gates = _gate_stages(h, wg_ref, wgt_ref, gbc_ref, gbr_ref, gc_ref, gr_ref)
    nw, mw, hd = NA_WIDTH, ML_WIDTH, ML_HEAD_DIM
    next(gates, None)
    qa_ref[...] = (_dot(h, w_ref[:, 0:nw]) * (NA_HEAD_DIM ** -0.5)).astype(BF16)
    next(gates, None)
    ka_ref[...] = _dot(h, w_ref[:, nw:2 * nw]).astype(BF16)
    next(gates, None)
    va_ref[...] = _dot(h, w_ref[:, 2 * nw:3 * nw]).astype(BF16)
    next(gates, None)
    base = 3 * nw
    qb = _dot(h, w_ref[:, base:base + mw])
    next(gates, None)
    kb = _dot(h, w_ref[:, base + mw:base + 2 * mw])
    next(gates, None)
    if rope:
        tm = qb.shape[0]
        expand = lambda t: jnp.broadcast_to(t[:, None, :], (tm // GRID_W, GRID_W, hd)).reshape(tm, hd)
        cos = expand(rowt_ref[0]) + colt_ref[0]
        sin = expand(rowt_ref[1]) + colt_ref[1]
        for hh in range(ML_HEADS):
            sl = slice(hh * hd, (hh + 1) * hd)
            qh = qb[:, sl]
            kh = kb[:, sl]
            qb_ref[:, sl] = (qh * cos + pltpu.roll(qh, hd // 2, 1) * sin).astype(BF16)
            kb_ref[:, sl] = ((kh * cos + pltpu.roll(kh, hd // 2, 1) * sin) * (hd ** -0.5)).astype(BF16)
    else:
        qb_ref[...] = qb.astype(BF16)
        kb_ref[...] = (kb * (hd ** -0.5)).astype(BF16)
    vt_ref[...] = _dot_nt(wvt_ref[...], h).astype(BF16)
    next(gates, None)
    ob_ref[...] = _dot(h, w_ref[:, base + 2 * mw:base + 3 * mw])
    for _ in gates:
        pass


def _gate_stages(h, wg_ref, wgt_ref, gbc_ref, gbr_ref, gc_ref, gr_ref):
    L = ML_CHUNK
    tm = h.shape[0]
    r_idx = lax.broadcasted_iota(jnp.int32, (L, L), 0)
    c_idx = lax.broadcasted_iota(jnp.int32, (L, L), 1)
    lower = (c_idx <= r_idx).astype(BF16)
    upper = (r_idx <= c_idx).astype(BF16)
    zc = _dot(h, wg_ref[...]) + gbc_ref[...]
    zr = _dot_nt(wgt_ref[...], h) + gbr_ref[...]
    yield
    lane = lax.broadcasted_iota(jnp.int32, zc.shape, 1)
    gates_c = _log_gates(zc, (lane % 2) == 1)
    split_c = jnp.concatenate(_split3(gates_c), axis=1)
    row = lax.broadcasted_iota(jnp.int32, zr.shape, 0)
    gates_r = _log_gates(zr, (row % 2) == 1)
    split_r = jnp.concatenate(_split3(gates_r), axis=0)
    yield
    bwd_lane = ((lax.broadcasted_iota(jnp.int32, (L, V7X_LANES), 1) // 2) % 2) == 1
    for c in range(tm // L):
        rs = slice(c * L, (c + 1) * L)
        r = _dot(lower, split_c[rs])
        prefix = r[:, 0:V7X_LANES] + r[:, V7X_LANES:2 * V7X_LANES] + r[:, 2 * V7X_LANES:]
        g = gates_c[rs]
        b = jnp.where(bwd_lane, prefix[L - 1:L, :] - prefix + g, prefix)
        gc_ref[rs, :] = g - pltpu.roll(b, V7X_LANES - 1, 1)
        if c % 2 == 1:
            yield
    row_l = lax.broadcasted_iota(jnp.int32, (N_GATES, L), 0)
    bwd_row = ((row_l // 2) % 2) == 1
    f_row = (row_l % 2) == 1
    for c in range(tm // L):
        ls = slice(c * L, (c + 1) * L)
        r = _dot(split_r[:, ls], upper)
        prefix = r[0:N_GATES] + r[N_GATES:2 * N_GATES] + r[2 * N_GATES:]
        g = gates_r[:, ls]
        b = jnp.where(bwd_row, prefix[:, L - 1:L] - prefix + g, prefix)
        gr_ref[:, ls] = jnp.where(f_row, b, g - pltpu.roll(b, N_GATES - 1, 0))
        if c % 2 == 1:
            yield


def _even_projection(x, mod3, mod_row, g, weights, rope_tables, tm=1024):
    w_main, w_vt, w_gc, w_gr, gb_col, gb_row = weights
    bsz, n, d = x.shape
    tm = min(tm, n)
    rope = rope_tables is not None
    if rope:
        row_t, col_t = rope_tables
        col_t = jnp.tile(col_t, (1, tm // GRID_W, 1))
    else:
        row_t = jnp.zeros((2, tm // GRID_W, ML_HEAD_DIM), F32)
        col_t = jnp.zeros((2, tm, ML_HEAD_DIM), F32)
    wm = w_main.shape[1]
    tok = lambda width, dt: jax.ShapeDtypeStruct((bsz, n, width), dt)
    tok_spec = lambda width: pl.BlockSpec((None, tm, width), lambda b, i: (b, i, 0))
    const = lambda shape: pl.BlockSpec(shape, lambda b, i: (0,) * len(shape))
    out_shape = [tok(NA_WIDTH, BF16)] * 3 + [tok(ML_WIDTH, BF16)] * 2 + [
        jax.ShapeDtypeStruct((bsz, ML_WIDTH, n), BF16), tok(ML_WIDTH, F32),
        tok(V7X_LANES, F32), jax.ShapeDtypeStruct((bsz, N_GATES, n), F32)]
    out_specs = [tok_spec(NA_WIDTH)] * 3 + [tok_spec(ML_WIDTH)] * 2 + [
        pl.BlockSpec((None, ML_WIDTH, tm), lambda b, i: (b, 0, i)), tok_spec(ML_WIDTH),
        tok_spec(V7X_LANES), pl.BlockSpec((None, N_GATES, tm), lambda b, i: (b, 0, i))]
    return pl.pallas_call(
        functools.partial(_evenproj_body, d=d, rope=rope),
        out_shape=out_shape,
        grid=(bsz, n // tm),
        in_specs=[
            tok_spec(d),
            pl.BlockSpec((None, 1, 3 * d), lambda b, i: (mod_row(b), 0, 0)),
            const((1, d)),
            const((d, wm)),
            const((ML_WIDTH, d)),
            const((d, V7X_LANES)),
            const((N_GATES, d)),
            const((1, V7X_LANES)),
            const((N_GATES, 1)),
            pl.BlockSpec((2, tm // GRID_W, ML_HEAD_DIM), lambda b, i: (0, i if rope else 0, 0)),
            pl.BlockSpec((2, tm, ML_HEAD_DIM), lambda b, i: (0, 0, 0)),
        ],
        out_specs=out_specs,
        compiler_params=_params(2),
        name="even_projection",
    )(x, mod3, g.reshape(1, d), w_main, w_vt, w_gc, w_gr, gb_col, gb_row, row_t, col_t)


def _na_body(q_ref, k_ref, v_ref, kx_ref, vx_ref, *rest, rows):
    rb = pl.program_id(2)
    sub, band = NA_SUB_ROWS, NA_BAND_ROWS
    n_sub = NA_ROWS_PER_STEP // sub
    tq = sub * GRID_W
    bias_refs, o_ref = rest[:n_sub], rest[n_sub]
    kx = kx_ref[...]
    vx = vx_ref[...]
    q = q_ref[...]
    lane = lax.broadcasted_iota(jnp.int32, q.shape, 1)
    bands = []
    for u in range(n_sub):
        band0 = jnp.clip((rb * n_sub + u) * sub - NA_KH // 2, 0, rows - band)
        start = pl.multiple_of(band0 * GRID_W, GRID_W)
        bands.append((k_ref[pl.ds(start, band * GRID_W), :], v_ref[pl.ds(start, band * GRID_W), :]))
    qms = [jnp.where((lane // NA_HEAD_DIM) == hh, q, jnp.zeros_like(q)) for hh in range(2)]
    items = [(hh, u) for hh in range(2) for u in range(n_sub)]

    def scores(item):
        hh, u = item
        qm = qms[hh][u * tq:(u + 1) * tq]
        return _dot_nt(qm, bands[u][0]) + bias_refs[u][hh], _dot_nt(qm, kx)

    def attend(item, s_loc, s_ctx):
        u = item[1]
        m = jnp.maximum(jnp.max(s_loc, axis=-1, keepdims=True), jnp.max(s_ctx, axis=-1, keepdims=True))
        p_loc = jnp.exp(s_loc - m)
        p_ctx = jnp.exp(s_ctx - m)
        denom = jnp.sum(p_loc, axis=-1, keepdims=True) + jnp.sum(p_ctx, axis=-1, keepdims=True)
        o = _dot(p_loc.astype(BF16), bands[u][1]) + _dot(p_ctx.astype(BF16), vx)
        return o * (1.0 / denom)

    outs = {}
    ready = scores(items[0])
    for k, item in enumerate(items):
        nxt = scores(items[k + 1]) if k + 1 < len(items) else None
        outs[item] = attend(item, *ready)
        ready = nxt
    per_head = [jnp.concatenate([outs[(hh, u)] for u in range(n_sub)], axis=0) for hh in range(2)]
    lane_o = lax.broadcasted_iota(jnp.int32, per_head[0].shape, 1)
    o_ref[...] = jnp.where(lane_o < NA_HEAD_DIM, per_head[0], per_head[1]).astype(BF16)


def _na_bias_tables(rpb, rows):
    assert NA_SUB_ROWS >= NA_KH // 2 and rows >= NA_BAND_ROWS + NA_SUB_ROWS
    sub, band, w = NA_SUB_ROWS, NA_BAND_ROWS, GRID_W
    kh, kw = NA_KH, NA_KW
    n_heads = rpb.shape[0]
    cols = np.arange(w)
    c0 = np.clip(cols - kw // 2, 0, w - kw)
    cc = np.arange(w)[None, :]
    col_ok = (cc >= c0[:, None]) & (cc < c0[:, None] + kw)
    period = 2 * w
    u = jnp.pad(rpb, ((0, 0), (0, 0), (w - kw, period - (w - kw) - rpb.shape[-1])))
    skew = jnp.tile(u, (1, 1, w))[..., :w * (period - 1)].reshape(n_heads, rpb.shape[1], w, period - 1)
    shifted = skew[..., w - 1:period - 1]
    t1 = jnp.where(col_ok[None, None], shifted, NEG_BIG)
    masked = jnp.full((n_heads, w, w), NEG_BIG, F32)
    blocks = []
    for i0, b0 in ((0, 0), (sub, sub - kh // 2), (rows - sub, rows - band)):
        for qi in range(i0, i0 + sub):
            r0 = min(max(qi - kh // 2, 0), rows - kh)
            row = [t1[:, r - qi + kh - 1] if r0 <= r < r0 + kh else masked for r in range(b0, b0 + band)]
            blocks.append(jnp.concatenate(row, axis=-1))
    return jnp.concatenate(blocks, axis=1)


def _neighbourhood_attention(q, k, v, kx, vx, bias):
    bsz, n, _ = q.shape
    lc = kx.shape[1]
    rows = n // GRID_W
    sub, band = NA_SUB_ROWS, NA_BAND_ROWS
    n_sub = NA_ROWS_PER_STEP // sub
    nrb = rows // NA_ROWS_PER_STEP
    tq = NA_ROWS_PER_STEP * GRID_W
    pair = 2 * NA_HEAD_DIM
    last = rows // sub - 1

    def bias_spec(u):
        def index(b, hp, rb):
            sb = rb * n_sub + u
            return (hp, jnp.where(sb == 0, 0, jnp.where(sb == last, 2, 1)), 0)
        return pl.BlockSpec((2, sub * GRID_W, band * GRID_W), index)

    return pl.pallas_call(
        functools.partial(_na_body, rows=rows),
        out_shape=jax.ShapeDtypeStruct((bsz, n, NA_WIDTH), BF16),
        grid=(bsz, NA_HEADS // 2, nrb),
        in_specs=[
            pl.BlockSpec((None, tq, pair), lambda b, hp, rb: (b, rb, hp)),
            pl.BlockSpec((None, n, pair), lambda b, hp, rb: (b, 0, hp)),
            pl.BlockSpec((None, n, pair), lambda b, hp, rb: (b, 0, hp)),
            pl.BlockSpec((None, lc, pair), lambda b, hp, rb: (b, 0, hp)),
            pl.BlockSpec((None, lc, pair), lambda b, hp, rb: (b, 0, hp)),
        ] + [bias_spec(u) for u in range(n_sub)],
        out_specs=pl.BlockSpec((None, tq, pair), lambda b, hp, rb: (b, rb, hp)),
        compiler_params=_params(3),
        name="neighbourhood_attention",
    )(q, k, v, kx, vx, *([bias] * n_sub))


def _mlstm_body(*refs, with_output):
    if with_output:
        (qf_ref, kf_ref, vtf_ref, gcf_ref, grf_ref, qr_ref, kr_ref, vtr_ref, gcr_ref, grr_ref,
         c0_ref, m0_ref, hf_ref, hr_ref, c_scr, m_scr) = refs
    else:
        (kf_ref, vtf_ref, grf_ref, kr_ref, vtr_ref, grr_ref,
         c0_ref, m0_ref, c_out_ref, m_out_ref, c_scr, m_scr) = refs
    step = pl.program_id(0)
    n_steps = pl.num_programs(0)
    L, hd = ML_CHUNK, ML_HEAD_DIM
    bsz = kf_ref.shape[0]

    @pl.when(step == 0)
    def _():
        c_scr[...] = c0_ref[...]
        m_scr[...] = m0_ref[...]

    s_idx = lax.broadcasted_iota(jnp.int32, (L, L), 0)
    t_idx = lax.broadcasted_iota(jnp.int32, (L, L), 1)
    ones_rows = (lax.broadcasted_iota(jnp.int32, (ML_AUG_ROWS, L), 0) == 0).astype(BF16)
    fwd_refs = (kf_ref, vtf_ref, grf_ref) + ((qf_ref, gcf_ref, hf_ref) if with_output else (None,) * 3)
    bwd_refs = (kr_ref, vtr_ref, grr_ref) + ((qr_ref, gcr_ref, hr_ref) if with_output else (None,) * 3)
    cps = kf_ref.shape[1] // L
    items = [(sub, bi, d, p) for sub in range(cps) for bi in range(bsz) for d in range(2)
             for p in range(ML_HEADS // 2)]
    lane2 = lax.broadcasted_iota(jnp.int32, (L, 2 * hd), 1)
    zero_ll = jnp.zeros((L, L), BF16)

    def refs_of(item):
        _, bi, d, _ = item
        return tuple(r if r is None else r.at[bi] for r in (fwd_refs if d == 0 else bwd_refs))

    def tokens_of(item):
        sub, _, d, _ = item
        first = sub if d == 0 else cps - 1 - sub
        return slice(first * L, (first + 1) * L)

    def block_diag(x2):
        zero = jnp.zeros_like(x2)
        return jnp.concatenate([jnp.where(lane2 < hd, x2, zero), jnp.where(lane2 >= hd, x2, zero)], axis=0)

    def front(item):
        _, bi, d, p = item
        k_ref, _, _, q_ref, _, _ = refs_of(item)
        tok = tokens_of(item)
        sl2 = slice(2 * p * hd, (2 * p + 2) * hd)
        qbd = block_diag(q_ref[tok, sl2])
        return _dot_nt(k_ref[tok, sl2], qbd), _dot_nt(c_scr[bi, d, p].astype(BF16), qbd)

    def finish(item, fronts):
        _, bi, d, p = item
        k_ref, vt_ref, gr_ref, _, gc_ref, h_ref = refs_of(item)
        tok = tokens_of(item)
        visible = (s_idx <= t_idx) if d == 0 else (s_idx >= t_idx)
        end = L - 1 if d == 0 else 0
        sl2 = slice(2 * p * hd, (2 * p + 2) * hd)
        a_prevs, a_toks, vt_augs, ps, w_inters, w_corrs, inv_floors = [], [], [], [], [], [], []
        for j in range(2):
            hh = 2 * p + j
            gi = hh * 4 + d * 2
            c_row = gr_ref[gi:gi + 1, tok]
            b_row = gr_ref[gi + 1:gi + 2, tok]
            b_end = b_row[:, end:end + 1]
            m_old = m_scr[bi, d, hh][0:1, 0:1]
            m_new = jnp.maximum(b_end + m_old, b_end + jnp.max(c_row, axis=-1, keepdims=True))
            a_prevs.append(jnp.broadcast_to(jnp.exp(b_end + m_old - m_new), (1, hd)))
            a_toks.append(jnp.exp(b_end + c_row - m_new))
            vt_augs.append(jnp.concatenate([vt_ref[hh * hd:(hh + 1) * hd, tok], ones_rows], axis=0))
            m_scr[bi, d, hh] = jnp.broadcast_to(m_new, (V7X_SUBLANES, V7X_LANES))
            if with_output:
                st = fronts[0][:, j * L:(j + 1) * L]
                log_w = jnp.where(visible, b_row + gc_ref[tok, gi:gi + 1], NEG_BIG)
                m_intra = jnp.max(log_w, axis=0, keepdims=True)
                ps.append((st * jnp.exp(log_w - m_intra)).astype(BF16))
                log_inter = b_row + m_old
                m_t = jnp.maximum(log_inter, m_intra)
                w_inters.append(jnp.exp(log_inter - m_t))
                w_corrs.append(jnp.exp(m_intra - m_t))
                inv_floors.append(jnp.exp(-m_t))
        cat = lambda parts: jnp.concatenate(parts, axis=1)
        vt2 = cat(vt_augs)
        c_old = c_scr[bi, d, p]
        if with_output:
            pbd = jnp.concatenate([cat([ps[0], zero_ll]), cat([zero_ll, ps[1]])], axis=0)
            intra = _dot(vt2, pbd)
            both = cat(w_inters) * fronts[1] + cat(w_corrs) * intra
            den = both[hd:hd + 1, :]
            ht = both[:hd, :] * (1.0 / jnp.maximum(jnp.abs(den), cat(inv_floors)))
            for j in range(2):
                h_ref[tok, (2 * p + j) * hd:(2 * p + j + 1) * hd] = ht[:, j * L:(j + 1) * L].T
        av = (vt2.astype(F32) * cat(a_toks)).astype(BF16)
        c_scr[bi, d, p] = cat(a_prevs) * c_old + _dot(av, block_diag(k_ref[tok, sl2]))

    ready = front(items[0]) if with_output else None
    for i, item in enumerate(items):
        nxt = front(items[i + 1]) if with_output and i + 1 < len(items) else None
        finish(item, ready)
        ready = nxt

    if not with_output:
        @pl.when(step == n_steps - 1)
        def _():
            c_out_ref[...] = c_scr[...]
            m_out_ref[...] = m_scr[...]


def _mlstm_scan(q, k, vt, gcol, grow, c0, m0, with_output):
    bsz, n, _ = k.shape
    L, hd = ML_CHUNK, ML_HEAD_DIM
    span = min(ML_CHUNKS_PER_STEP, n // L) * L
    nsteps = n // span
    fwd = lambda width: pl.BlockSpec((bsz, span, width), lambda c: (0, c, 0))
    rev = lambda width: pl.BlockSpec((bsz, span, width), lambda c: (0, nsteps - 1 - c, 0))
    fwd_t = lambda height: pl.BlockSpec((bsz, height, span), lambda c: (0, 0, c))
    rev_t = lambda height: pl.BlockSpec((bsz, height, span), lambda c: (0, 0, nsteps - 1 - c))
    c_spec = pl.BlockSpec((bsz, 2, ML_HEADS // 2, hd + ML_AUG_ROWS, 2 * hd), lambda c: (0, 0, 0, 0, 0))
    m_spec = pl.BlockSpec((bsz, 2, ML_HEADS, V7X_SUBLANES, V7X_LANES), lambda c: (0, 0, 0, 0, 0))
    if with_output:
        args = [q, k, vt, gcol, grow, q, k, vt, gcol, grow, c0, m0]
        in_specs = ([fwd(ML_WIDTH)] * 2 + [fwd_t(ML_WIDTH), fwd(V7X_LANES), fwd_t(N_GATES)]
                    + [rev(ML_WIDTH)] * 2 + [rev_t(ML_WIDTH), rev(V7X_LANES), rev_t(N_GATES)])
        out_shape = [jax.ShapeDtypeStruct((bsz, n, ML_WIDTH), F32)] * 2
        out_specs = [fwd(ML_WIDTH), rev(ML_WIDTH)]
    else:
        args = [k, vt, grow, k, vt, grow, c0, m0]
        in_specs = ([fwd(ML_WIDTH), fwd_t(ML_WIDTH), fwd_t(N_GATES)]
                    + [rev(ML_WIDTH), rev_t(ML_WIDTH), rev_t(N_GATES)])
        out_shape = [jax.ShapeDtypeStruct(c0.shape, F32), jax.ShapeDtypeStruct(m0.shape, F32)]
        out_specs = [c_spec, m_spec]
    return pl.pallas_call(
        functools.partial(_mlstm_body, with_output=with_output),
        out_shape=out_shape,
        grid=(nsteps,),
        in_specs=in_specs + [c_spec, m_spec],
        out_specs=out_specs,
        scratch_shapes=[pltpu.VMEM((bsz, 2, ML_HEADS // 2, hd + ML_AUG_ROWS, 2 * hd), F32),
                        pltpu.VMEM((bsz, 2, ML_HEADS, V7X_SUBLANES, V7X_LANES), F32)],
        compiler_params=_params(1),
        name="mlstm_scan_latent" if with_output else "mlstm_scan_context",
    )(*args)


def _gelu(x):
    k2 = 2.0 * (2.0 / np.pi) ** 0.5
    z = x * (k2 + (k2 * 0.044715) * (x * x))
    return x * jax.nn.sigmoid(z)


def _sg_body(x_ref, mod_ref, g_ref, win_ref, lng_ref, lnb_ref, ws_ref, bs_ref, wout_ref, o_ref, v_scr,
             *, d, width, tm):
    x = x_ref[...]
    mod = mod_ref[...]
    h = _modulated(x, g_ref[...], mod, d).astype(BF16)
    gw = width // SG_GROUPS
    groups = [slice(g * gw, (g + 1) * gw) for g in range(SG_GROUPS)]
    order = [width + cs.start for cs in groups] + [cs.start for cs in groups]
    project = lambda k: _dot(h, win_ref[:, order[k]:order[k] + gw])
    pre = project(0)
    total = None
    for k, cs in enumerate(groups):
        nxt = project(k + 1)
        vg = _gelu(pre)
        v_scr[:, cs] = vg
        part = jnp.sum(vg, axis=-1, keepdims=True)
        total = part if total is None else total + part
        pre = nxt
    mu = total * (1.0 / width)
    sq = None
    for cs in groups:
        vc = v_scr[:, cs] - mu
        part = jnp.sum(vc * vc, axis=-1, keepdims=True)
        sq = part if sq is None else sq + part
    rstd = lax.rsqrt(sq * (1.0 / width) + EPS)
    bs = bs_ref[...]
    y = None
    for g, cs in enumerate(groups):
        nxt = project(SG_GROUPS + g + 1) if g + 1 < SG_GROUPS else None
        vn = ((v_scr[:, cs] - mu) * rstd * lng_ref[:, cs] + lnb_ref[:, cs]).astype(BF16)
        u = _gelu(pre)
        pre = nxt
        parts = []
        for c in range(tm // SG_CHUNK):
            rs = slice(c * SG_CHUNK, (c + 1) * SG_CHUNK)
            mixed = _dot(ws_ref[g], vn[rs]) + bs[:, g:g + 1]
            parts.append((u[rs] * mixed).astype(BF16))
        yg = _dot(jnp.concatenate(parts, axis=0), wout_ref[cs, :])
        y = yg if y is None else y + yg
    o_ref[...] = x + mod[:, 2 * d:] * y


def _spatial_gating(x, mod3, g, w_in, ln_g, ln_b, w_s, b_s, w_out, tm=1024):
    bsz, n, d = x.shape
    width = w_out.shape[0]
    tm = min(tm, n)
    w_in, w_s, w_out = w_in.astype(BF16), w_s.astype(BF16), w_out.astype(BF16)
    const = lambda shape: pl.BlockSpec(shape, lambda b, i: (0,) * len(shape), pipeline_mode=pl.Buffered(1))
    return pl.pallas_call(
        functools.partial(_sg_body, d=d, width=width, tm=tm),
        out_shape=jax.ShapeDtypeStruct(x.shape, F32),
        grid=(bsz, n // tm),
        in_specs=[
            pl.BlockSpec((None, tm, d), lambda b, i: (b, i, 0)),
            pl.BlockSpec((None, 1, 3 * d), lambda b, i: (b, 0, 0)),
            const((1, d)), const(w_in.shape), const((1, width)), const((1, width)),
            const(w_s.shape), const((SG_CHUNK, SG_GROUPS)), const(w_out.shape),
        ],
        out_specs=pl.BlockSpec((None, tm, d), lambda b, i: (b, i, 0)),
        scratch_shapes=[pltpu.VMEM((tm, width), F32)],
        compiler_params=_params(2),
        name="spatial_gating",
    )(x, mod3, g.reshape(1, d), w_in, ln_g.reshape(1, width), ln_b.reshape(1, width), w_s, b_s.T, w_out)


def _rope_tables(n):
    hd = ML_HEAD_DIM
    n_pairs = hd // 4
    inv_freq = ROPE_THETA ** (-jnp.arange(n_pairs, dtype=F32) / n_pairs)
    row_ang = jnp.arange(n // GRID_W, dtype=F32)[:, None] * inv_freq
    col_ang = jnp.arange(GRID_W, dtype=F32)[:, None] * inv_freq

    def table(ang, is_row):
        zero = jnp.zeros_like(ang)
        cos, sin = jnp.cos(ang), jnp.sin(ang)
        half = lambda t: jnp.concatenate([t, zero] if is_row else [zero, t], axis=-1)
        return jnp.stack([jnp.concatenate([half(cos), half(cos)], axis=-1),
                          jnp.concatenate([half(-sin), half(sin)], axis=-1)])

    return table(row_ang, True), table(col_ang, False)


def _even_weights(w_in, gate_b):
    d = w_in.shape[0]
    hd = ML_HEAD_DIM
    base = 3 * NA_WIDTH
    w_in = w_in.astype(BF16)
    deint = lambda w: w.reshape(d, ML_HEADS, hd // 2, 2).transpose(0, 1, 3, 2).reshape(d, ML_WIDTH)
    qb = deint(w_in[:, base:base + ML_WIDTH])
    kb = deint(w_in[:, base + ML_WIDTH:base + 2 * ML_WIDTH])
    w_main = jnp.concatenate([w_in[:, :base], qb, kb, w_in[:, base + 3 * ML_WIDTH:base + 4 * ML_WIDTH]],
                             axis=1).astype(BF16)
    w_vt = w_in[:, base + 2 * ML_WIDTH:base + 3 * ML_WIDTH].T.astype(BF16)
    wg = w_in[:, base + 4 * ML_WIDTH:]
    w_gc = jnp.pad(wg, ((0, 0), (0, V7X_LANES - N_GATES))).astype(BF16)
    w_gr = wg.T.astype(BF16)
    gb = gate_b.reshape(N_GATES).astype(F32)
    gb_col = jnp.pad(gb, (0, V7X_LANES - N_GATES)).reshape(1, V7X_LANES)
    gb_row = gb.reshape(N_GATES, 1)
    return w_main, w_vt, w_gc, w_gr, gb_col, gb_row


def kernel(x, c, ctx, c_ctx, w_mod, b_mod, norm_g, ffn_w_in, ffn_w_out, mix_w_in, na_rpb, ml_gate_b, ml_head_g,
           mix_w_out, sg_w_in, sg_ln_g, sg_ln_b, sg_w_s, sg_b_s, sg_w_out, final_g):
    bsz, n, d = x.shape
    depth = w_mod.shape[0]
    ctx_row = bsz
    cvec = jnp.zeros((V7X_SUBLANES, d), F32).at[:bsz].set(c).at[ctx_row].set(c_ctx)
    mod = _mod_vectors(cvec, w_mod, b_mod).reshape(depth, V7X_SUBLANES, 3, 1, 3 * d)
    batch_row = lambda b: b
    context_row = lambda b: ctx_row
    last_ctx_layer = ((depth - 1) // 2) * 2
    ffn_order = [(l, half) for l in range(depth) for half in range(2)]
    following = lambda l, half: dict(zip(ffn_order, ffn_order[1:])).get((l, half))
    cast_of = lambda nxt: None if nxt is None else (ffn_w_in, ffn_w_out) + nxt
    w_now = (ffn_w_in[0, 0].astype(BF16), ffn_w_out[0, 0].astype(BF16))
    lc = ctx.shape[1]
    xc = ctx.reshape(1, bsz * lc, d)
    per_batch_rows = lambda a: a.reshape(bsz, lc, a.shape[-1])
    per_batch_cols = lambda a: a.reshape(a.shape[1], bsz, lc).transpose(1, 0, 2)
    for l in range(depth):
        ctx_in = l <= last_ctx_layer
        ctx_out = l < last_ctx_layer
        if ctx_in:
            xc = _ffn(xc, mod[l, :, 0], context_row, norm_g[l, 0], *w_now)
        x, w_now = _ffn(x, mod[l, :, 0], batch_row, norm_g[l, 0], *w_now, cast_next=cast_of(following(l, 0)))
        if l % 2 == 0:
            e = l // 2
            weights = _even_weights(mix_w_in[e], ml_gate_b[e])
            qa, ka, va, qb, kb, vt, ob, gcol, grow = _even_projection(
                x, mod[l, :, 1], batch_row, norm_g[l, 1], weights, _rope_tables(n))
            _, kax, vax, _, kbx, vtx, _, _, growx = _even_projection(
                xc, mod[l, :, 1], context_row, norm_g[l, 1], weights, None)
            kax, vax, kbx = per_batch_rows(kax), per_batch_rows(vax), per_batch_rows(kbx)
            vtx, growx = per_batch_cols(vtx), per_batch_cols(growx)
            bias = _na_bias_tables(na_rpb[e], n // GRID_W)
            ya = _neighbourhood_attention(qa, ka, va, kax, vax, bias)
            c0 = jnp.zeros((bsz, 2, ML_HEADS // 2, ML_HEAD_DIM + ML_AUG_ROWS, 2 * ML_HEAD_DIM), F32)
            m0 = jnp.zeros((bsz, 2, ML_HEADS, V7X_SUBLANES, V7X_LANES), F32)
            c1, m1 = _mlstm_scan(None, kbx, vtx, None, growx, c0, m0, False)
            hf, hr = _mlstm_scan(qb, kb, vt, gcol, grow, c1, m1, True)
            mixer = (mod[l, :, 1], ya, hf, hr, ob, ml_head_g[e], mix_w_out[e].astype(BF16))
            assert not ctx_out, "context output path is not needed for this depth"
        else:
            o = l // 2
            mixer = None
            x = _spatial_gating(x, mod[l, :, 1], norm_g[l, 1], sg_w_in[o], sg_ln_g[o], sg_ln_b[o],
                                sg_w_s[o], sg_b_s[o], sg_w_out[o])
        fg = final_g if l == depth - 1 else None
        nxt = following(l, 1)
        out = _ffn(x, mod[l, :, 2], batch_row, norm_g[l, 2], *w_now, final_g=fg, mixer=mixer, cast_next=cast_of(nxt))
        x, w_now = out if nxt is not None else (out, None)
    return x
```

```python
import functools

import numpy as np
import jax
import jax.numpy as jnp
from jax import lax
from jax.experimental import pallas as pl
from jax.experimental.pallas import tpu as pltpu

GRID_W = 64
NA_HEADS = 8
NA_HEAD_DIM = 64
NA_KH = 8
NA_KW = 16
ML_HEADS = 4
ML_HEAD_DIM = 128
ML_CHUNK = 128
ROPE_THETA = 10000.0
SG_CHUNK = 128
SG_GROUPS = 8
EPS = 1e-6
NA_WIDTH = NA_HEADS * NA_HEAD_DIM
ML_WIDTH = ML_HEADS * ML_HEAD_DIM
N_GATES = 4 * ML_HEADS
ML_AUG_ROWS = 16
ML_CHUNKS_PER_STEP = 1

V7X_LANES = 128
V7X_SUBLANES = 8
V7X_MXU_COLUMNS = 256
V7X_VMEM_LIMIT_BYTES = 56 * 1024 * 1024

NEG_BIG = -1e30
FFN_TOKENS_PER_STEP = 1024
V7X_BF16_SUBLANES = 16
NA_ROWS_PER_STEP = 32
NA_SUB_ROWS = 4
NA_BAND_ROWS = 12

BF16 = jnp.bfloat16
F32 = jnp.float32


def _dot(a, b):
    return jnp.dot(a, b, preferred_element_type=F32)


def _dot_nt(a, b):
    return lax.dot_general(a, b, (((1,), (1,)), ((), ())), preferred_element_type=F32)


def _params(n_axes):
    return pltpu.CompilerParams(
        dimension_semantics=("arbitrary",) * n_axes,
        vmem_limit_bytes=V7X_VMEM_LIMIT_BYTES,
    )


def _modulated(x, g, mod, d):
    shift = mod[:, :d]
    scale = mod[:, d:2 * d]
    gs = g * (1.0 + scale)
    ms = jnp.mean(x * x, axis=-1, keepdims=True)
    return x * lax.rsqrt(ms + EPS) * gs + shift


def _mod_body(c_ref, w_ref, b_ref, o_ref):
    c = c_ref[...]
    s = (c * jax.nn.sigmoid(c)).astype(BF16)
    o_ref[...] = _dot(s, w_ref[...].astype(BF16)) + b_ref[...]


def _mod_vectors(cvec, w_mod, b_mod):
    depth, d, width = w_mod.shape
    tn = width // 4
    return pl.pallas_call(
        _mod_body,
        out_shape=jax.ShapeDtypeStruct((depth, V7X_SUBLANES, width), F32),
        grid=(depth, width // tn),
        in_specs=[
            pl.BlockSpec((V7X_SUBLANES, d), lambda l, j: (0, 0)),
            pl.BlockSpec((None, d, tn), lambda l, j: (l, 0, j)),
            pl.BlockSpec((None, 1, tn), lambda l, j: (l, 0, j)),
        ],
        out_specs=pl.BlockSpec((None, V7X_SUBLANES, tn), lambda l, j: (l, 0, j)),
        compiler_params=_params(2),
        name="mod_vectors",
    )(cvec, w_mod, b_mod.reshape(depth, 1, width))


def _mixer_output(ya_ref, hf_ref, hr_ref, ob_ref, hg_ref, w_ref):
    hd = ML_HEAD_DIM
    hb = hf_ref[...] + hr_ref[...]
    ob = ob_ref[...]
    hg = hg_ref[...]
    y = _dot(ya_ref[...], w_ref[0:NA_WIDTH, :])
    ybs = []
    for hh in range(ML_HEADS):
        sl = slice(hh * hd, (hh + 1) * hd)
        hs = hb[:, sl]
        ms = jnp.mean(hs * hs, axis=-1, keepdims=True)
        ybs.append((hs * lax.rsqrt(ms + EPS) * hg[:, sl] * jax.nn.sigmoid(ob[:, sl])).astype(BF16))
    return y + _dot(jnp.concatenate(ybs, axis=1), w_ref[NA_WIDTH:, :])


def _ffn_body(*refs, d, d_ff, chunk, final, mixer, cast_next):
    refs = list(refs)
    if cast_next:
        next_in_ref, next_out_ref, o_ref, cast_in_ref, cast_out_ref = refs[-5:]
        cast_in_ref[...] = next_in_ref[...].astype(BF16)
        cast_out_ref[...] = next_out_ref[...].astype(BF16)
        refs = refs[:-5]
    else:
        o_ref = refs.pop()
    x_ref, mod_ref, g_ref, win_ref, wout_ref = refs[:5]
    rest = refs[5:]
    x = x_ref[...]
    if mixer:
        mixmod_ref = rest[0]
        x = x + mixmod_ref[...][:, 2 * d:] * _mixer_output(*rest[1:7])
        rest = rest[7:]
    if final:
        (fg_ref,) = rest
    mod = mod_ref[...]
    h = _modulated(x, g_ref[...], mod, d).astype(BF16)
    acc = None
    for c in range(d_ff // chunk):
        a = _dot(h, win_ref[:, c * chunk:(c + 1) * chunk])
        b = _dot(h, win_ref[:, d_ff + c * chunk:d_ff + (c + 1) * chunk])
        t = (a * jax.nn.sigmoid(a) * b).astype(BF16)
        y = _dot(t, wout_ref[c * chunk:(c + 1) * chunk, :])
        acc = y if acc is None else acc + y
    out = x + (0.5 * mod[:, 2 * d:]) * acc
    if final:
        ms = jnp.mean(out * out, axis=-1, keepdims=True)
        out = out * lax.rsqrt(ms + EPS) * fg_ref[...]
    o_ref[...] = out


def _ffn(x, mod3, mod_row, g, w_in, w_out, final_g=None, mixer=None, cast_next=None, tm=None):
    bsz, n, d = x.shape
    d_ff = w_out.shape[-2]
    if tm is None:
        tm = FFN_TOKENS_PER_STEP // 2 if mixer is not None else FFN_TOKENS_PER_STEP
    tm = min(tm, n)
    steps = n // tm
    final = final_g is not None
    tok_spec = lambda width: pl.BlockSpec((None, tm, width), lambda b, i: (b, i, 0))
    resident = dict(pipeline_mode=pl.Buffered(1))
    in_specs = [
        tok_spec(d),
        pl.BlockSpec((None, 1, 3 * d), lambda b, i: (mod_row(b), 0, 0)),
        pl.BlockSpec((1, d), lambda b, i: (0, 0)),
        pl.BlockSpec((d, 2 * d_ff), lambda b, i: (0, 0), **resident),
        pl.BlockSpec((None,) * (w_out.ndim - 2) + (d_ff, d), lambda b, i: (0,) * w_out.ndim, **resident),
    ]
    args = [x, mod3, g.reshape(1, d), w_in, w_out]
    if mixer is not None:
        mix_mod3, ya, hf, hr, ob, head_g, w_mix = mixer
        in_specs += [
            pl.BlockSpec((None, 1, 3 * d), lambda b, i: (mod_row(b), 0, 0)),
            tok_spec(NA_WIDTH), tok_spec(ML_WIDTH), tok_spec(ML_WIDTH), tok_spec(ML_WIDTH),
            pl.BlockSpec((1, ML_WIDTH), lambda b, i: (0, 0)),
            pl.BlockSpec(w_mix.shape, lambda b, i: (0, 0), **resident),
        ]
        args += [mix_mod3, ya, hf, hr, ob, head_g.reshape(1, ML_WIDTH), w_mix]
    if final:
        in_specs.append(pl.BlockSpec((1, d), lambda b, i: (0, 0)))
        args.append(final_g.reshape(1, d))
    out_shape = [jax.ShapeDtypeStruct(x.shape, F32)]
    out_specs = [tok_spec(d)]
    if cast_next is not None:
        w_in_all, w_out_all, layer, half = cast_next
        rows_in = d // (bsz * steps)
        assert rows_in * bsz * steps == d and rows_in % V7X_BF16_SUBLANES == 0, (d, bsz, steps)
        tiles = d_ff // V7X_BF16_SUBLANES
        assert tiles * V7X_BF16_SUBLANES == d_ff, d_ff
        blocks = max(k for k in range(1, steps + 1) if tiles % k == 0)
        rows_out = d_ff // blocks
        in_row = lambda b, i: b * steps + i
        out_row = lambda b, i: jnp.minimum(i, blocks - 1)
        in_specs += [
            pl.BlockSpec((None, None, rows_in, 2 * d_ff), lambda b, i: (layer, half, in_row(b, i), 0)),
            pl.BlockSpec((None, None, rows_out, d), lambda b, i: (layer, half, out_row(b, i), 0)),
        ]
        args += [w_in_all, w_out_all]
        out_shape += [jax.ShapeDtypeStruct((d, 2 * d_ff), BF16), jax.ShapeDtypeStruct((bsz, d_ff, d), BF16)]
        out_specs += [pl.BlockSpec((rows_in, 2 * d_ff), lambda b, i: (in_row(b, i), 0)),
                      pl.BlockSpec((None, rows_out, d), lambda b, i: (b, out_row(b, i), 0))]
    outs = pl.pallas_call(
        functools.partial(_ffn_body, d=d, d_ff=d_ff, chunk=V7X_MXU_COLUMNS, final=final,
                          mixer=mixer is not None, cast_next=cast_next is not None),
        out_shape=out_shape,
        grid=(bsz, steps),
        in_specs=in_specs,
        out_specs=out_specs,
        compiler_params=_params(2),
        name="macaron_ffn",
    )(*args)
    return (outs[0], (outs[1], outs[2])) if cast_next is not None else outs[0]


def _log_gates(z, is_forget):
    ls = jnp.minimum(z, 0.0) - jnp.log1p(jnp.exp(-jnp.abs(z)))
    return jnp.where(is_forget, ls, z)


def _split3(x):
    hi = x.astype(BF16)
    r1 = x - hi.astype(F32)
    mid = r1.astype(BF16)
    lo = (r1 - mid.astype(F32)).astype(BF16)
    return [hi, mid, lo]


def _evenproj_body(x_ref, mod_ref, g_ref, wna_ref, wq_ref, wk_ref, wo_ref, wvt_ref, wg_ref, wgt_ref, gbc_ref, gbr_ref, rowt_ref, colt_ref,
                   qa_ref, ka_ref, va_ref, qb_ref, kb_ref, vt_ref, ob_ref, gc_ref, gr_ref, *, d, rope):
    h = _modulated(x_ref[...], g_ref[...], mod_ref[...], d).astype(BF16)
    gates = _gate_stages(h, wg_ref, wgt_ref, gbc_ref, gbr_ref, gc_ref, gr_ref)
    nw, hd = NA_WIDTH, ML_HEAD_DIM
    next(gates, None)
    qa_ref[...] = (_dot(h, wna_ref[:, 0:nw]) * (NA_HEAD_DIM ** -0.5)).astype(BF16)
    next(gates, None)
    ka_ref[...] = _dot(h, wna_ref[:, nw:2 * nw]).astype(BF16)
    next(gates, None)
    va_ref[...] = _dot(h, wna_ref[:, 2 * nw:3 * nw]).astype(BF16)
    next(gates, None)
    qb = _dot(h, wq_ref[...])
    next(gates, None)
    kb = _dot(h, wk_ref[...])
    next(gates, None)
    if rope:
        tm = qb.shape[0]
        expand = lambda t: jnp.broadcast_to(t[:, None, :], (tm // GRID_W, GRID_W, hd)).reshape(tm, hd)
        cos = expand(rowt_ref[0]) + colt_ref[0]
        sin = expand(rowt_ref[1]) + colt_ref[1]
        for hh in range(ML_HEADS):
            sl = slice(hh * hd, (hh + 1) * hd)
            qh = qb[:, sl]
            kh = kb[:, sl]
            qb_ref[:, sl] = (qh * cos + pltpu.roll(qh, hd // 2, 1) * sin).astype(BF16)
            kb_ref[:, sl] = ((kh * cos + pltpu.roll(kh, hd // 2, 1) * sin) * (hd ** -0.5)).astype(BF16)
    else:
        qb_ref[...] = qb.astype(BF16)
        kb_ref[...] = (kb * (hd ** -0.5)).astype(BF16)
    vt_ref[...] = _dot_nt(wvt_ref[...], h).astype(BF16)
    next(gates, None)
    ob_ref[...] = _dot(h, wo_ref[...])
    for _ in gates:
        pass


def _gate_stages(h, wg_ref, wgt_ref, gbc_ref, gbr_ref, gc_ref, gr_ref):
    L = ML_CHUNK
    tm = h.shape[0]
    r_idx = lax.broadcasted_iota(jnp.int32, (L, L), 0)
    c_idx = lax.broadcasted_iota(jnp.int32, (L, L), 1)
    lower = (c_idx <= r_idx).astype(BF16)
    upper = (r_idx <= c_idx).astype(BF16)
    zc = _dot(h, wg_ref[...]) + gbc_ref[...]
    zr = _dot_nt(wgt_ref[...], h) + gbr_ref[...]
    yield
    lane = lax.broadcasted_iota(jnp.int32, zc.shape, 1)
    gates_c = _log_gates(zc, (lane % 2) == 1)
    split_c = jnp.concatenate(_split3(gates_c), axis=1)
    row = lax.broadcasted_iota(jnp.int32, zr.shape, 0)
    gates_r = _log_gates(zr, (row % 2) == 1)
    split_r = jnp.concatenate(_split3(gates_r), axis=0)
    yield
    bwd_lane = ((lax.broadcasted_iota(jnp.int32, (L, V7X_LANES), 1) // 2) % 2) == 1
    for c in range(tm // L):
        rs = slice(c * L, (c + 1) * L)
        r = _dot(lower, split_c[rs])
        prefix = r[:, 0:V7X_LANES] + r[:, V7X_LANES:2 * V7X_LANES] + r[:, 2 * V7X_LANES:]
        g = gates_c[rs]
        b = jnp.where(bwd_lane, prefix[L - 1:L, :] - prefix + g, prefix)
        gc_ref[rs, :] = g - pltpu.roll(b, V7X_LANES - 1, 1)
        if c % 2 == 1:
            yield
    row_l = lax.broadcasted_iota(jnp.int32, (N_GATES, L), 0)
    bwd_row = ((row_l // 2) % 2) == 1
    f_row = (row_l % 2) == 1
    for c in range(tm // L):
        ls = slice(c * L, (c + 1) * L)
        r = _dot(split_r[:, ls], upper)
        prefix = r[0:N_GATES] + r[N_GATES:2 * N_GATES] + r[2 * N_GATES:]
        g = gates_r[:, ls]
        b = jnp.where(bwd_row, prefix[:, L - 1:L] - prefix + g, prefix)
        gr_ref[:, ls] = jnp.where(f_row, b, g - pltpu.roll(b, N_GATES - 1, 0))
        if c % 2 == 1:
            yield


def _even_projection(x, mod3, mod_row, g, weights, rope_tables, tm=1024):
    w_na, w_q, w_k, w_o, w_vt, w_gc, w_gr, gb_col, gb_row = weights
    bsz, n, d = x.shape
    tm = min(tm, n)
    rope = rope_tables is not None
    if rope:
        row_t, col_t = rope_tables
        col_t = jnp.tile(col_t, (1, tm // GRID_W, 1))
    else:
        row_t = jnp.zeros((2, tm // GRID_W, ML_HEAD_DIM), F32)
        col_t = jnp.zeros((2, tm, ML_HEAD_DIM), F32)
    tok = lambda width, dt: jax.ShapeDtypeStruct((bsz, n, width), dt)
    tok_spec = lambda width: pl.BlockSpec((None, tm, width), lambda b, i: (b, i, 0))
    const = lambda shape: pl.BlockSpec(shape, lambda b, i: (0,) * len(shape))
    out_shape = [tok(NA_WIDTH, BF16)] * 3 + [tok(ML_WIDTH, BF16)] * 2 + [
        jax.ShapeDtypeStruct((bsz, ML_WIDTH, n), BF16), tok(ML_WIDTH, F32),
        tok(V7X_LANES, F32), jax.ShapeDtypeStruct((bsz, N_GATES, n), F32)]
    out_specs = [tok_spec(NA_WIDTH)] * 3 + [tok_spec(ML_WIDTH)] * 2 + [
        pl.BlockSpec((None, ML_WIDTH, tm), lambda b, i: (b, 0, i)), tok_spec(ML_WIDTH),
        tok_spec(V7X_LANES), pl.BlockSpec((None, N_GATES, tm), lambda b, i: (b, 0, i))]
    return pl.pallas_call(
        functools.partial(_evenproj_body, d=d, rope=rope),
        out_shape=out_shape,
        grid=(bsz, n // tm),
        in_specs=[
            tok_spec(d),
            pl.BlockSpec((None, 1, 3 * d), lambda b, i: (mod_row(b), 0, 0)),
            const((1, d)),
            const(w_na.shape), const(w_q.shape), const(w_k.shape), const(w_o.shape),
            const((ML_WIDTH, d)),
            const((d, V7X_LANES)),
            const((N_GATES, d)),
            const((1, V7X_LANES)),
            const((N_GATES, 1)),
            pl.BlockSpec((2, tm // GRID_W, ML_HEAD_DIM), lambda b, i: (0, i if rope else 0, 0)),
            pl.BlockSpec((2, tm, ML_HEAD_DIM), lambda b, i: (0, 0, 0)),
        ],
        out_specs=out_specs,
        compiler_params=_params(2),
        name="even_projection",
    )(x, mod3, g.reshape(1, d), w_na, w_q, w_k, w_o, w_vt, w_gc, w_gr, gb_col, gb_row, row_t, col_t)


def _na_body(q_ref, k_ref, v_ref, kx_ref, vx_ref, *rest, rows):
    rb = pl.program_id(2)
    sub, band = NA_SUB_ROWS, NA_BAND_ROWS
    n_sub = NA_ROWS_PER_STEP // sub
    tq = sub * GRID_W
    bias_refs, o_ref = rest[:n_sub], rest[n_sub]
    kx = kx_ref[...]
    vx = vx_ref[...]
    q = q_ref[...]
    lane = lax.broadcasted_iota(jnp.int32, q.shape, 1)
    bands = []
    for u in range(n_sub):
        band0 = jnp.clip((rb * n_sub + u) * sub - NA_KH // 2, 0, rows - band)
        start = pl.multiple_of(band0 * GRID_W, GRID_W)
        bands.append((k_ref[pl.ds(start, band * GRID_W), :], v_ref[pl.ds(start, band * GRID_W), :]))
    qms = [jnp.where((lane // NA_HEAD_DIM) == hh, q, jnp.zeros_like(q)) for hh in range(2)]
    items = [(hh, u) for hh in range(2) for u in range(n_sub)]

    def scores(item):
        hh, u = item
        qm = qms[hh][u * tq:(u + 1) * tq]
        return _dot_nt(qm, bands[u][0]) + bias_refs[u][hh], _dot_nt(qm, kx)

    def attend(item, s_loc, s_ctx):
        u = item[1]
        m = jnp.maximum(jnp.max(s_loc, axis=-1, keepdims=True), jnp.max(s_ctx, axis=-1, keepdims=True))
        p_loc = jnp.exp(s_loc - m)
        p_ctx = jnp.exp(s_ctx - m)
        denom = jnp.sum(p_loc, axis=-1, keepdims=True) + jnp.sum(p_ctx, axis=-1, keepdims=True)
        o = _dot(p_loc.astype(BF16), bands[u][1]) + _dot(p_ctx.astype(BF16), vx)
        return o * (1.0 / denom)

    outs = {}
    ready = scores(items[0])
    for k, item in enumerate(items):
        nxt = scores(items[k + 1]) if k + 1 < len(items) else None
        outs[item] = attend(item, *ready)
        ready = nxt
    per_head = [jnp.concatenate([outs[(hh, u)] for u in range(n_sub)], axis=0) for hh in range(2)]
    lane_o = lax.broadcasted_iota(jnp.int32, per_head[0].shape, 1)
    o_ref[...] = jnp.where(lane_o < NA_HEAD_DIM, per_head[0], per_head[1]).astype(BF16)


def _na_bias_tables(rpb, rows):
    assert NA_SUB_ROWS >= NA_KH // 2 and rows >= NA_BAND_ROWS + NA_SUB_ROWS
    sub, band, w = NA_SUB_ROWS, NA_BAND_ROWS, GRID_W
    kh, kw = NA_KH, NA_KW
    n_heads = rpb.shape[0]
    cols = np.arange(w)
    c0 = np.clip(cols - kw // 2, 0, w - kw)
    cc = np.arange(w)[None, :]
    col_ok = (cc >= c0[:, None]) & (cc < c0[:, None] + kw)
    period = 2 * w
    u = jnp.pad(rpb, ((0, 0), (0, 0), (w - kw, period - (w - kw) - rpb.shape[-1])))
    skew = jnp.tile(u, (1, 1, w))[..., :w * (period - 1)].reshape(n_heads, rpb.shape[1], w, period - 1)
    shifted = skew[..., w - 1:period - 1]
    t1 = jnp.where(col_ok[None, None], shifted, NEG_BIG)
    masked = jnp.full((n_heads, w, w), NEG_BIG, F32)
    blocks = []
    for i0, b0 in ((0, 0), (sub, sub - kh // 2), (rows - sub, rows - band)):
        for qi in range(i0, i0 + sub):
            r0 = min(max(qi - kh // 2, 0), rows - kh)
            row = [t1[:, r - qi + kh - 1] if r0 <= r < r0 + kh else masked for r in range(b0, b0 + band)]
            blocks.append(jnp.concatenate(row, axis=-1))
    return jnp.concatenate(blocks, axis=1)


def _neighbourhood_attention(q, k, v, kx, vx, bias):
    bsz, n, _ = q.shape
    lc = kx.shape[1]
    rows = n // GRID_W
    sub, band = NA_SUB_ROWS, NA_BAND_ROWS
    n_sub = NA_ROWS_PER_STEP // sub
    nrb = rows // NA_ROWS_PER_STEP
    tq = NA_ROWS_PER_STEP * GRID_W
    pair = 2 * NA_HEAD_DIM
    last = rows // sub - 1

    def bias_spec(u):
        def index(b, hp, rb):
            sb = rb * n_sub + u
            return (hp, jnp.where(sb == 0, 0, jnp.where(sb == last, 2, 1)), 0)
        return pl.BlockSpec((2, sub * GRID_W, band * GRID_W), index)

    return pl.pallas_call(
        functools.partial(_na_body, rows=rows),
        out_shape=jax.ShapeDtypeStruct((bsz, n, NA_WIDTH), BF16),
        grid=(bsz, NA_HEADS // 2, nrb),
        in_specs=[
            pl.BlockSpec((None, tq, pair), lambda b, hp, rb: (b, rb, hp)),
            pl.BlockSpec((None, n, pair), lambda b, hp, rb: (b, 0, hp)),
            pl.BlockSpec((None, n, pair), lambda b, hp, rb: (b, 0, hp)),
            pl.BlockSpec((None, lc, pair), lambda b, hp, rb: (b, 0, hp)),
            pl.BlockSpec((None, lc, pair), lambda b, hp, rb: (b, 0, hp)),
        ] + [bias_spec(u) for u in range(n_sub)],
        out_specs=pl.BlockSpec((None, tq, pair), lambda b, hp, rb: (b, rb, hp)),
        compiler_params=_params(3),
        name="neighbourhood_attention",
    )(q, k, v, kx, vx, *([bias] * n_sub))


def _mlstm_body(*refs, with_output):
    if with_output:
        (qf_ref, kf_ref, vtf_ref, gcf_ref, grf_ref, qr_ref, kr_ref, vtr_ref, gcr_ref, grr_ref,
         c0_ref, m0_ref, hf_ref, hr_ref, c_scr, m_scr) = refs
    else:
        (kf_ref, vtf_ref, grf_ref, kr_ref, vtr_ref, grr_ref,
         c0_ref, m0_ref, c_out_ref, m_out_ref, c_scr, m_scr) = refs
    step = pl.program_id(0)
    n_steps = pl.num_programs(0)
    L, hd = ML_CHUNK, ML_HEAD_DIM
    bsz = kf_ref.shape[0]

    @pl.when(step == 0)
    def _():
        c_scr[...] = c0_ref[...]
        m_scr[...] = m0_ref[...]

    s_idx = lax.broadcasted_iota(jnp.int32, (L, L), 0)
    t_idx = lax.broadcasted_iota(jnp.int32, (L, L), 1)
    ones_rows = (lax.broadcasted_iota(jnp.int32, (ML_AUG_ROWS, L), 0) == 0).astype(BF16)
    fwd_refs = (kf_ref, vtf_ref, grf_ref) + ((qf_ref, gcf_ref, hf_ref) if with_output else (None,) * 3)
    bwd_refs = (kr_ref, vtr_ref, grr_ref) + ((qr_ref, gcr_ref, hr_ref) if with_output else (None,) * 3)
    cps = kf_ref.shape[1] // L
    items = [(sub, bi, d, p) for sub in range(cps) for bi in range(bsz) for d in range(2)
             for p in range(ML_HEADS // 2)]
    lane2 = lax.broadcasted_iota(jnp.int32, (L, 2 * hd), 1)
    zero_ll = jnp.zeros((L, L), BF16)

    def refs_of(item):
        _, bi, d, _ = item
        return tuple(r if r is None else r.at[bi] for r in (fwd_refs if d == 0 else bwd_refs))

    def tokens_of(item):
        sub, _, d, _ = item
        first = sub if d == 0 else cps - 1 - sub
        return slice(first * L, (first + 1) * L)

    def block_diag(x2):
        zero = jnp.zeros_like(x2)
        return jnp.concatenate([jnp.where(lane2 < hd, x2, zero), jnp.where(lane2 >= hd, x2, zero)], axis=0)

    def front(item):
        _, bi, d, p = item
        k_ref, _, _, q_ref, _, _ = refs_of(item)
        tok = tokens_of(item)
        sl2 = slice(2 * p * hd, (2 * p + 2) * hd)
        qbd = block_diag(q_ref[tok, sl2])
        return _dot_nt(k_ref[tok, sl2], qbd), _dot_nt(c_scr[bi, d, p].astype(BF16), qbd)

    def finish(item, fronts):
        _, bi, d, p = item
        k_ref, vt_ref, gr_ref, _, gc_ref, h_ref = refs_of(item)
        tok = tokens_of(item)
        visible = (s_idx <= t_idx) if d == 0 else (s_idx >= t_idx)
        end = L - 1 if d == 0 else 0
        sl2 = slice(2 * p * hd, (2 * p + 2) * hd)
        a_prevs, a_toks, vt_augs, ps, w_inters, w_corrs, inv_floors = [], [], [], [], [], [], []
        for j in range(2):
            hh = 2 * p + j
            gi = hh * 4 + d * 2
            c_row = gr_ref[gi:gi + 1, tok]
            b_row = gr_ref[gi + 1:gi + 2, tok]
            b_end = b_row[:, end:end + 1]
            m_old = m_scr[bi, d, hh][0:1, 0:1]
            m_new = jnp.maximum(b_end + m_old, b_end + jnp.max(c_row, axis=-1, keepdims=True))
            a_prevs.append(jnp.broadcast_to(jnp.exp(b_end + m_old - m_new), (1, hd)))
            a_toks.append(jnp.exp(b_end + c_row - m_new))
            vt_augs.append(jnp.concatenate([vt_ref[hh * hd:(hh + 1) * hd, tok], ones_rows], axis=0))
            m_scr[bi, d, hh] = jnp.broadcast_to(m_new, (V7X_SUBLANES, V7X_LANES))
            if with_output:
                st = fronts[0][:, j * L:(j + 1) * L]
                log_w = jnp.where(visible, b_row + gc_ref[tok, gi:gi + 1], NEG_BIG)
                m_intra = jnp.max(log_w, axis=0, keepdims=True)
                ps.append((st * jnp.exp(log_w - m_intra)).astype(BF16))
                log_inter = b_row + m_old
                m_t = jnp.maximum(log_inter, m_intra)
                w_inters.append(jnp.exp(log_inter - m_t))
                w_corrs.append(jnp.exp(m_intra - m_t))
                inv_floors.append(jnp.exp(-m_t))
        cat = lambda parts: jnp.concatenate(parts, axis=1)
        vt2 = cat(vt_augs)
        c_old = c_scr[bi, d, p]
        if with_output:
            pbd = jnp.concatenate([cat([ps[0], zero_ll]), cat([zero_ll, ps[1]])], axis=0)
            intra = _dot(vt2, pbd)
            both = cat(w_inters) * fronts[1] + cat(w_corrs) * intra
            den = both[hd:hd + 1, :]
            ht = both[:hd, :] * (1.0 / jnp.maximum(jnp.abs(den), cat(inv_floors)))
            for j in range(2):
                h_ref[tok, (2 * p + j) * hd:(2 * p + j + 1) * hd] = ht[:, j * L:(j + 1) * L].T
        av = (vt2.astype(F32) * cat(a_toks)).astype(BF16)
        c_scr[bi, d, p] = cat(a_prevs) * c_old + _dot(av, block_diag(k_ref[tok, sl2]))

    ready = front(items[0]) if with_output else None
    for i, item in enumerate(items):
        nxt = front(items[i + 1]) if with_output and i + 1 < len(items) else None
        finish(item, ready)
        ready = nxt

    if not with_output:
        @pl.when(step == n_steps - 1)
        def _():
            c_out_ref[...] = c_scr[...]
            m_out_ref[...] = m_scr[...]


def _mlstm_scan(q, k, vt, gcol, grow, c0, m0, with_output):
    bsz, n, _ = k.shape
    L, hd = ML_CHUNK, ML_HEAD_DIM
    span = min(ML_CHUNKS_PER_STEP, n // L) * L
    nsteps = n // span
    fwd = lambda width: pl.BlockSpec((bsz, span, width), lambda c: (0, c, 0))
    rev = lambda width: pl.BlockSpec((bsz, span, width), lambda c: (0, nsteps - 1 - c, 0))
    fwd_t = lambda height: pl.BlockSpec((bsz, height, span), lambda c: (0, 0, c))
    rev_t = lambda height: pl.BlockSpec((bsz, height, span), lambda c: (0, 0, nsteps - 1 - c))
    c_spec = pl.BlockSpec((bsz, 2, ML_HEADS // 2, hd + ML_AUG_ROWS, 2 * hd), lambda c: (0, 0, 0, 0, 0))
    m_spec = pl.BlockSpec((bsz, 2, ML_HEADS, V7X_SUBLANES, V7X_LANES), lambda c: (0, 0, 0, 0, 0))
    if with_output:
        args = [q, k, vt, gcol, grow, q, k, vt, gcol, grow, c0, m0]
        in_specs = ([fwd(ML_WIDTH)] * 2 + [fwd_t(ML_WIDTH), fwd(V7X_LANES), fwd_t(N_GATES)]
                    + [rev(ML_WIDTH)] * 2 + [rev_t(ML_WIDTH), rev(V7X_LANES), rev_t(N_GATES)])
        out_shape = [jax.ShapeDtypeStruct((bsz, n, ML_WIDTH), F32)] * 2
        out_specs = [fwd(ML_WIDTH), rev(ML_WIDTH)]
    else:
        args = [k, vt, grow, k, vt, grow, c0, m0]
        in_specs = ([fwd(ML_WIDTH), fwd_t(ML_WIDTH), fwd_t(N_GATES)]
                    + [rev(ML_WIDTH), rev_t(ML_WIDTH), rev_t(N_GATES)])
        out_shape = [jax.ShapeDtypeStruct(c0.shape, F32), jax.ShapeDtypeStruct(m0.shape, F32)]
        out_specs = [c_spec, m_spec]
    return pl.pallas_call(
        functools.partial(_mlstm_body, with_output=with_output),
        out_shape=out_shape,
        grid=(nsteps,),
        in_specs=in_specs + [c_spec, m_spec],
        out_specs=out_specs,
        scratch_shapes=[pltpu.VMEM((bsz, 2, ML_HEADS // 2, hd + ML_AUG_ROWS, 2 * hd), F32),
                        pltpu.VMEM((bsz, 2, ML_HEADS, V7X_SUBLANES, V7X_LANES), F32)],
        compiler_params=_params(1),
        name="mlstm_scan_latent" if with_output else "mlstm_scan_context",
    )(*args)


def _gelu(x):
    k2 = 2.0 * (2.0 / np.pi) ** 0.5
    z = x * (k2 + (k2 * 0.044715) * (x * x))
    return x * jax.nn.sigmoid(z)


def _sg_body(x_ref, mod_ref, g_ref, win_ref, lng_ref, lnb_ref, ws_ref, bs_ref, wout_ref, o_ref, v_scr,
             *, d, width, tm):
    x = x_ref[...]
    mod = mod_ref[...]
    h = _modulated(x, g_ref[...], mod, d).astype(BF16)
    gw = width // SG_GROUPS
    groups = [slice(g * gw, (g + 1) * gw) for g in range(SG_GROUPS)]
    order = [width + cs.start for cs in groups] + [cs.start for cs in groups]
    project = lambda k: _dot(h, win_ref[:, order[k]:order[k] + gw])
    pre = project(0)
    total = None
    for k, cs in enumerate(groups):
        nxt = project(k + 1)
        vg = _gelu(pre)
        v_scr[:, cs] = vg
        part = jnp.sum(vg, axis=-1, keepdims=True)
        total = part if total is None else total + part
        pre = nxt
    mu = total * (1.0 / width)
    sq = None
    for cs in groups:
        vc = v_scr[:, cs] - mu
        part = jnp.sum(vc * vc, axis=-1, keepdims=True)
        sq = part if sq is None else sq + part
    rstd = lax.rsqrt(sq * (1.0 / width) + EPS)
    bs = bs_ref[...]
    y = None
    for g, cs in enumerate(groups):
        nxt = project(SG_GROUPS + g + 1) if g + 1 < SG_GROUPS else None
        vn = ((v_scr[:, cs] - mu) * rstd * lng_ref[:, cs] + lnb_ref[:, cs]).astype(BF16)
        u = _gelu(pre)
        pre = nxt
        parts = []
        for c in range(tm // SG_CHUNK):
            rs = slice(c * SG_CHUNK, (c + 1) * SG_CHUNK)
            mixed = _dot(ws_ref[g], vn[rs]) + bs[:, g:g + 1]
            parts.append((u[rs] * mixed).astype(BF16))
        yg = _dot(jnp.concatenate(parts, axis=0), wout_ref[cs, :])
        y = yg if y is None else y + yg
    o_ref[...] = x + mod[:, 2 * d:] * y


def _spatial_gating(x, mod3, g, w_in, ln_g, ln_b, w_s, b_s, w_out, tm=1024):
    bsz, n, d = x.shape
    width = w_out.shape[0]
    tm = min(tm, n)
    w_in, w_s, w_out = w_in.astype(BF16), w_s.astype(BF16), w_out.astype(BF16)
    const = lambda shape: pl.BlockSpec(shape, lambda b, i: (0,) * len(shape), pipeline_mode=pl.Buffered(1))
    return pl.pallas_call(
        functools.partial(_sg_body, d=d, width=width, tm=tm),
        out_shape=jax.ShapeDtypeStruct(x.shape, F32),
        grid=(bsz, n // tm),
        in_specs=[
            pl.BlockSpec((None, tm, d), lambda b, i: (b, i, 0)),
            pl.BlockSpec((None, 1, 3 * d), lambda b, i: (b, 0, 0)),
            const((1, d)), const(w_in.shape), const((1, width)), const((1, width)),
            const(w_s.shape), const((SG_CHUNK, SG_GROUPS)), const(w_out.shape),
        ],
        out_specs=pl.BlockSpec((None, tm, d), lambda b, i: (b, i, 0)),
        scratch_shapes=[pltpu.VMEM((tm, width), F32)],
        compiler_params=_params(2),
        name="spatial_gating",
    )(x, mod3, g.reshape(1, d), w_in, ln_g.reshape(1, width), ln_b.reshape(1, width), w_s, b_s.T, w_out)


def _rope_tables(n):
    hd = ML_HEAD_DIM
    n_pairs = hd // 4
    inv_freq = ROPE_THETA ** (-jnp.arange(n_pairs, dtype=F32) / n_pairs)
    row_ang = jnp.arange(n // GRID_W, dtype=F32)[:, None] * inv_freq
    col_ang = jnp.arange(GRID_W, dtype=F32)[:, None] * inv_freq

    def table(ang, is_row):
        zero = jnp.zeros_like(ang)
        cos, sin = jnp.cos(ang), jnp.sin(ang)
        half = lambda t: jnp.concatenate([t, zero] if is_row else [zero, t], axis=-1)
        return jnp.stack([jnp.concatenate([half(cos), half(cos)], axis=-1),
                          jnp.concatenate([half(-sin), half(sin)], axis=-1)])

    return table(row_ang, True), table(col_ang, False)


def _even_weights(w_in, gate_b):
    d = w_in.shape[0]
    hd = ML_HEAD_DIM
    base = 3 * NA_WIDTH
    w_in = w_in.astype(BF16)
    deint = lambda w: w.reshape(d, ML_HEADS, hd // 2, 2).transpose(0, 1, 3, 2).reshape(d, ML_WIDTH)
    qb = deint(w_in[:, base:base + ML_WIDTH])
    kb = deint(w_in[:, base + ML_WIDTH:base + 2 * ML_WIDTH])
    w_na = w_in[:, :base]
    w_o = w_in[:, base + 3 * ML_WIDTH:base + 4 * ML_WIDTH]
    w_vt = w_in[:, base + 2 * ML_WIDTH:base + 3 * ML_WIDTH].T
    wg = w_in[:, base + 4 * ML_WIDTH:]
    w_gc = jnp.pad(wg, ((0, 0), (0, V7X_LANES - N_GATES))).astype(BF16)
    w_gr = wg.T.astype(BF16)
    gb = gate_b.reshape(N_GATES).astype(F32)
    gb_col = jnp.pad(gb, (0, V7X_LANES - N_GATES)).reshape(1, V7X_LANES)
    gb_row = gb.reshape(N_GATES, 1)
    return w_na, qb, kb, w_o, w_vt, w_gc, w_gr, gb_col, gb_row


def kernel(x, c, ctx, c_ctx, w_mod, b_mod, norm_g, ffn_w_in, ffn_w_out, mix_w_in, na_rpb, ml_gate_b, ml_head_g,
           mix_w_out, sg_w_in, sg_ln_g, sg_ln_b, sg_w_s, sg_b_s, sg_w_out, final_g):
    bsz, n, d = x.shape
    depth = w_mod.shape[0]
    ctx_row = bsz
    cvec = jnp.zeros((V7X_SUBLANES, d), F32).at[:bsz].set(c).at[ctx_row].set(c_ctx)
    mod = _mod_vectors(cvec, w_mod, b_mod).reshape(depth, V7X_SUBLANES, 3, 1, 3 * d)
    batch_row = lambda b: b
    context_row = lambda b: ctx_row
    last_ctx_layer = ((depth - 1) // 2) * 2
    ffn_order = [(l, half) for l in range(depth) for half in range(2)]
    following = lambda l, half: dict(zip(ffn_order, ffn_order[1:])).get((l, half))
    cast_of = lambda nxt: None if nxt is None else (ffn_w_in, ffn_w_out) + nxt
    w_now = (ffn_w_in[0, 0].astype(BF16), ffn_w_out[0, 0].astype(BF16))
    lc = ctx.shape[1]
    xc = ctx.reshape(1, bsz * lc, d)
    per_batch_rows = lambda a: a.reshape(bsz, lc, a.shape[-1])
    per_batch_cols = lambda a: a.reshape(a.shape[1], bsz, lc).transpose(1, 0, 2)
    for l in range(depth):
        ctx_in = l <= last_ctx_layer
        ctx_out = l < last_ctx_layer
        if ctx_in:
            xc = _ffn(xc, mod[l, :, 0], context_row, norm_g[l, 0], *w_now)
        x, w_now = _ffn(x, mod[l, :, 0], batch_row, norm_g[l, 0], *w_now, cast_next=cast_of(following(l, 0)))
        if l % 2 == 0:
            e = l // 2
            weights = _even_weights(mix_w_in[e], ml_gate_b[e])
            qa, ka, va, qb, kb, vt, ob, gcol, grow = _even_projection(
                x, mod[l, :, 1], batch_row, norm_g[l, 1], weights, _rope_tables(n))
            _, kax, vax, _, kbx, vtx, _, _, growx = _even_projection(
                xc, mod[l, :, 1], context_row, norm_g[l, 1], weights, None)
            kax, vax, kbx = per_batch_rows(kax), per_batch_rows(vax), per_batch_rows(kbx)
            vtx, growx = per_batch_cols(vtx), per_batch_cols(growx)
            bias = _na_bias_tables(na_rpb[e], n // GRID_W)
            ya = _neighbourhood_attention(qa, ka, va, kax, vax, bias)
            c0 = jnp.zeros((bsz, 2, ML_HEADS // 2, ML_HEAD_DIM + ML_AUG_ROWS, 2 * ML_HEAD_DIM), F32)
            m0 = jnp.zeros((bsz, 2, ML_HEADS, V7X_SUBLANES, V7X_LANES), F32)
            c1, m1 = _mlstm_scan(None, kbx, vtx, None, growx, c0, m0, False)
            hf, hr = _mlstm_scan(qb, kb, vt, gcol, grow, c1, m1, True)
            mixer = (mod[l, :, 1], ya, hf, hr, ob, ml_head_g[e], mix_w_out[e].astype(BF16))
            assert not ctx_out, "context output path is not needed for this depth"
        else:
            o = l // 2
            mixer = None
            x = _spatial_gating(x, mod[l, :, 1], norm_g[l, 1], sg_w_in[o], sg_ln_g[o], sg_ln_b[o],
                                sg_w_s[o], sg_b_s[o], sg_w_out[o])
        fg = final_g if l == depth - 1 else None
        nxt = following(l, 1)
        out = _ffn(x, mod[l, :, 2], batch_row, norm_g[l, 2], *w_now, final_g=fg, mixer=mixer, cast_next=cast_of(nxt))
        x, w_now = out if nxt is not None else (out, None)
    return x
```

```python
import functools

import numpy as np
import jax
import jax.numpy as jnp
from jax import lax
from jax.experimental import pallas as pl
from jax.experimental.pallas import tpu as pltpu

GRID_W = 64
NA_HEADS = 8
NA_HEAD_DIM = 64
NA_KH = 8
NA_KW = 16
ML_HEADS = 4
ML_HEAD_DIM = 128
ML_CHUNK = 128
ROPE_THETA = 10000.0
SG_CHUNK = 128
SG_GROUPS = 8
EPS = 1e-6
NA_WIDTH = NA_HEADS * NA_HEAD_DIM
ML_WIDTH = ML_HEADS * ML_HEAD_DIM
N_GATES = 4 * ML_HEADS
ML_AUG_ROWS = 16
ML_CHUNKS_PER_STEP = 1

V7X_LANES = 128
V7X_SUBLANES = 8
V7X_MXU_COLUMNS = 256
V7X_VMEM_LIMIT_BYTES = 56 * 1024 * 1024

NEG_BIG = -1e30
FFN_TOKENS_PER_STEP = 1024
V7X_BF16_SUBLANES = 16
NA_ROWS_PER_STEP = 32
NA_SUB_ROWS = 4
NA_BAND_ROWS = 12

BF16 = jnp.bfloat16
F32 = jnp.float32


def _dot(a, b):
    return jnp.dot(a, b, preferred_element_type=F32)


def _dot_nt(a, b):
    return lax.dot_general(a, b, (((1,), (1,)), ((), ())), preferred_element_type=F32)


def _params(n_axes):
    return pltpu.CompilerParams(
        dimension_semantics=("arbitrary",) * n_axes,
        vmem_limit_bytes=V7X_VMEM_LIMIT_BYTES,
    )


def _modulated(x, g, mod, d):
    shift = mod[:, :d]
    scale = mod[:, d:2 * d]
    gs = g * (1.0 + scale)
    ms = jnp.mean(x * x, axis=-1, keepdims=True)
    return x * lax.rsqrt(ms + EPS) * gs + shift


def _mod_body(c_ref, w_ref, b_ref, o_ref):
    c = c_ref[...]
    s = (c * jax.nn.sigmoid(c)).astype(BF16)
    o_ref[...] = _dot(s, w_ref[...].astype(BF16)) + b_ref[...]


def _mod_vectors(cvec, w_mod, b_mod):
    depth, d, width = w_mod.shape
    tn = width // 4
    return pl.pallas_call(
        _mod_body,
        out_shape=jax.ShapeDtypeStruct((depth, V7X_SUBLANES, width), F32),
        grid=(depth, width // tn),
        in_specs=[
            pl.BlockSpec((V7X_SUBLANES, d), lambda l, j: (0, 0)),
            pl.BlockSpec((None, d, tn), lambda l, j: (l, 0, j)),
            pl.BlockSpec((None, 1, tn), lambda l, j: (l, 0, j)),
        ],
        out_specs=pl.BlockSpec((None, V7X_SUBLANES, tn), lambda l, j: (l, 0, j)),
        compiler_params=_params(2),
        name="mod_vectors",
    )(cvec, w_mod, b_mod.reshape(depth, 1, width))


def _mixer_output(ya_ref, hf_ref, hr_ref, ob_ref, hg_ref, w_ref):
    hd = ML_HEAD_DIM
    hb = hf_ref[...] + hr_ref[...]
    ob = ob_ref[...]
    hg = hg_ref[...]
    y = _dot(ya_ref[...], w_ref[0:NA_WIDTH, :])
    ybs = []
    for hh in range(ML_HEADS):
        sl = slice(hh * hd, (hh + 1) * hd)
        hs = hb[:, sl]
        ms = jnp.mean(hs * hs, axis=-1, keepdims=True)
        ybs.append((hs * lax.rsqrt(ms + EPS) * hg[:, sl] * jax.nn.sigmoid(ob[:, sl])).astype(BF16))
    return y + _dot(jnp.concatenate(ybs, axis=1), w_ref[NA_WIDTH:, :])


def _ffn_body(*refs, d, d_ff, chunk, final, mixer, cast_next):
    refs = list(refs)
    if cast_next:
        next_in_ref, next_out_ref, o_ref, cast_in_ref, cast_out_ref = refs[-5:]
        cast_in_ref[...] = next_in_ref[...].astype(BF16)
        cast_out_ref[...] = next_out_ref[...].astype(BF16)
        refs = refs[:-5]
    else:
        o_ref = refs.pop()
    x_ref, mod_ref, g_ref, win_ref, wout_ref = refs[:5]
    rest = refs[5:]
    x = x_ref[...]
    if mixer:
        mixmod_ref = rest[0]
        x = x + mixmod_ref[...][:, 2 * d:] * _mixer_output(*rest[1:7])
        rest = rest[7:]
    if final:
        (fg_ref,) = rest
    mod = mod_ref[...]
    h = _modulated(x, g_ref[...], mod, d).astype(BF16)
    acc = None
    for c in range(d_ff // chunk):
        a = _dot(h, win_ref[:, c * chunk:(c + 1) * chunk])
        b = _dot(h, win_ref[:, d_ff + c * chunk:d_ff + (c + 1) * chunk])
        t = (a * jax.nn.sigmoid(a) * b).astype(BF16)
        y = _dot(t, wout_ref[c * chunk:(c + 1) * chunk, :])
        acc = y if acc is None else acc + y
    out = x + (0.5 * mod[:, 2 * d:]) * acc
    if final:
        ms = jnp.mean(out * out, axis=-1, keepdims=True)
        out = out * lax.rsqrt(ms + EPS) * fg_ref[...]
    o_ref[...] = out


def _ffn(x, mod3, mod_row, g, w_in, w_out, final_g=None, mixer=None, cast_next=None, tm=None):
    bsz, n, d = x.shape
    d_ff = w_out.shape[-2]
    if tm is None:
        tm = FFN_TOKENS_PER_STEP // 2 if mixer is not None else FFN_TOKENS_PER_STEP
    tm = min(tm, n)
    steps = n // tm
    final = final_g is not None
    tok_spec = lambda width: pl.BlockSpec((None, tm, width), lambda b, i: (b, i, 0))
    resident = dict(pipeline_mode=pl.Buffered(1))
    in_specs = [
        tok_spec(d),
        pl.BlockSpec((None, 1, 3 * d), lambda b, i: (mod_row(b), 0, 0)),
        pl.BlockSpec((1, d), lambda b, i: (0, 0)),
        pl.BlockSpec((d, 2 * d_ff), lambda b, i: (0, 0), **resident),
        pl.BlockSpec((None,) * (w_out.ndim - 2) + (d_ff, d), lambda b, i: (0,) * w_out.ndim, **resident),
    ]
    args = [x, mod3, g.reshape(1, d), w_in, w_out]
    if mixer is not None:
        mix_mod3, ya, hf, hr, ob, head_g, w_mix = mixer
        in_specs += [
            pl.BlockSpec((None, 1, 3 * d), lambda b, i: (mod_row(b), 0, 0)),
            tok_spec(NA_WIDTH), tok_spec(ML_WIDTH), tok_spec(ML_WIDTH), tok_spec(ML_WIDTH),
            pl.BlockSpec((1, ML_WIDTH), lambda b, i: (0, 0)),
            pl.BlockSpec(w_mix.shape, lambda b, i: (0, 0), **resident),
        ]
        args += [mix_mod3, ya, hf, hr, ob, head_g.reshape(1, ML_WIDTH), w_mix]
    if final:
        in_specs.append(pl.BlockSpec((1, d), lambda b, i: (0, 0)))
        args.append(final_g.reshape(1, d))
    out_shape = [jax.ShapeDtypeStruct(x.shape, F32)]
    out_specs = [tok_spec(d)]
    if cast_next is not None:
        w_in_all, w_out_all, layer, half = cast_next
        rows_in = d // (bsz * steps)
        assert rows_in * bsz * steps == d and rows_in % V7X_BF16_SUBLANES == 0, (d, bsz, steps)
        tiles = d_ff // V7X_BF16_SUBLANES
        assert tiles * V7X_BF16_SUBLANES == d_ff, d_ff
        blocks = max(k for k in range(1, steps + 1) if tiles % k == 0)
        rows_out = d_ff // blocks
        in_row = lambda b, i: b * steps + i
        out_row = lambda b, i: jnp.minimum(i, blocks - 1)
        in_specs += [
            pl.BlockSpec((None, None, rows_in, 2 * d_ff), lambda b, i: (layer, half, in_row(b, i), 0)),
            pl.BlockSpec((None, None, rows_out, d), lambda b, i: (layer, half, out_row(b, i), 0)),
        ]
        args += [w_in_all, w_out_all]
        out_shape += [jax.ShapeDtypeStruct((d, 2 * d_ff), BF16), jax.ShapeDtypeStruct((bsz, d_ff, d), BF16)]
        out_specs += [pl.BlockSpec((rows_in, 2 * d_ff), lambda b, i: (in_row(b, i), 0)),
                      pl.BlockSpec((None, rows_out, d), lambda b, i: (b, out_row(b, i), 0))]
    outs = pl.pallas_call(
        functools.partial(_ffn_body, d=d, d_ff=d_ff, chunk=V7X_MXU_COLUMNS, final=final,
                          mixer=mixer is not None, cast_next=cast_next is not None),
        out_shape=out_shape,
        grid=(bsz, steps),
        in_specs=in_specs,
        out_specs=out_specs,
        compiler_params=_params(2),
        name="macaron_ffn",
    )(*args)
    return (outs[0], (outs[1], outs[2])) if cast_next is not None else outs[0]


def _log_gates(z, is_forget):
    ls = jnp.minimum(z, 0.0) - jnp.log1p(jnp.exp(-jnp.abs(z)))
    return jnp.where(is_forget, ls, z)


def _split3(x):
    hi = x.astype(BF16)
    r1 = x - hi.astype(F32)
    mid = r1.astype(BF16)
    lo = (r1 - mid.astype(F32)).astype(BF16)
    return [hi, mid, lo]


def _evenproj_body(x_ref, mod_ref, g_ref, wna_ref, wq_ref, wk_ref, wo_ref, wvt_ref, wg_ref, wgt_ref, gbc_ref, gbr_ref, rowt_ref, colt_ref,
                   qa_ref, ka_ref, va_ref, qb_ref, kb_ref, vt_ref, ob_ref, gc_ref, gr_ref, *, d, rope):
    h = _modulated(x_ref[...], g_ref[...], mod_ref[...], d).astype(BF16)
    gates = _gate_stages(h, wg_ref, wgt_ref, gbc_ref, gbr_ref, gc_ref, gr_ref)
    nw, hd = NA_WIDTH, ML_HEAD_DIM
    next(gates, None)
    qa_ref[...] = (_dot(h, wna_ref[:, 0:nw]) * (NA_HEAD_DIM ** -0.5)).astype(BF16)
    next(gates, None)
    ka_ref[...] = _dot(h, wna_ref[:, nw:2 * nw]).astype(BF16)
    next(gates, None)
    va_ref[...] = _dot(h, wna_ref[:, 2 * nw:3 * nw]).astype(BF16)
    next(gates, None)
    qb = _dot(h, wq_ref[...])
    next(gates, None)
    kb = _dot(h, wk_ref[...])
    next(gates, None)
    if rope:
        tm = qb.shape[0]
        expand = lambda t: jnp.broadcast_to(t[:, None, :], (tm // GRID_W, GRID_W, hd)).reshape(tm, hd)
        cos = expand(rowt_ref[0]) + colt_ref[0]
        sin = expand(rowt_ref[1]) + colt_ref[1]
        for hh in range(ML_HEADS):
            sl = slice(hh * hd, (hh + 1) * hd)
            qh = qb[:, sl]
            kh = kb[:, sl]
            qb_ref[:, sl] = (qh * cos + pltpu.roll(qh, hd // 2, 1) * sin).astype(BF16)
            kb_ref[:, sl] = ((kh * cos + pltpu.roll(kh, hd // 2, 1) * sin) * (hd ** -0.5)).astype(BF16)
    else:
        qb_ref[...] = qb.astype(BF16)
        kb_ref[...] = (kb * (hd ** -0.5)).astype(BF16)
    vt_ref[...] = _dot_nt(wvt_ref[...], h).astype(BF16)
    next(gates, None)
    ob_ref[...] = _dot(h, wo_ref[...])
    for _ in gates:
        pass


def _gate_stages(h, wg_ref, wgt_ref, gbc_ref, gbr_ref, gc_ref, gr_ref):
    L = ML_CHUNK
    tm = h.shape[0]
    r_idx = lax.broadcasted_iota(jnp.int32, (L, L), 0)
    c_idx = lax.broadcasted_iota(jnp.int32, (L, L), 1)
    lower = (c_idx <= r_idx).astype(BF16)
    upper = (r_idx <= c_idx).astype(BF16)
    zc = _dot(h, wg_ref[...]) + gbc_ref[...]
    zr = _dot_nt(wgt_ref[...], h) + gbr_ref[...]
    yield
    lane = lax.broadcasted_iota(jnp.int32, zc.shape, 1)
    gates_c = _log_gates(zc, (lane % 2) == 1)
    split_c = jnp.concatenate(_split3(gates_c), axis=1)
    row = lax.broadcasted_iota(jnp.int32, zr.shape, 0)
    gates_r = _log_gates(zr, (row % 2) == 1)
    split_r = jnp.concatenate(_split3(gates_r), axis=0)
    yield
    bwd_lane = ((lax.broadcasted_iota(jnp.int32, (L, V7X_LANES), 1) // 2) % 2) == 1
    for c in range(tm // L):
        rs = slice(c * L, (c + 1) * L)
        r = _dot(lower, split_c[rs])
        prefix = r[:, 0:V7X_LANES] + r[:, V7X_LANES:2 * V7X_LANES] + r[:, 2 * V7X_LANES:]
        g = gates_c[rs]
        b = jnp.where(bwd_lane, prefix[L - 1:L, :] - prefix + g, prefix)
        gc_ref[rs, :] = g - pltpu.roll(b, V7X_LANES - 1, 1)
        if c % 2 == 1:
            yield
    row_l = lax.broadcasted_iota(jnp.int32, (N_GATES, L), 0)
    bwd_row = ((row_l // 2) % 2) == 1
    f_row = (row_l % 2) == 1
    for c in range(tm // L):
        ls = slice(c * L, (c + 1) * L)
        r = _dot(split_r[:, ls], upper)
        prefix = r[0:N_GATES] + r[N_GATES:2 * N_GATES] + r[2 * N_GATES:]
        g = gates_r[:, ls]
        b = jnp.where(bwd_row, prefix[:, L - 1:L] - prefix + g, prefix)
        gr_ref[:, ls] = jnp.where(f_row, b, g - pltpu.roll(b, N_GATES - 1, 0))
        if c % 2 == 1:
            yield


def _even_projection(x, mod3, mod_row, g, weights, rope_tables, tm=1024):
    w_na, w_q, w_k, w_o, w_vt, w_gc, w_gr, gb_col, gb_row = weights
    bsz, n, d = x.shape
    tm = min(tm, n)
    rope = rope_tables is not None
    if rope:
        row_t, col_t = rope_tables
        col_t = jnp.tile(col_t, (1, tm // GRID_W, 1))
    else:
        row_t = jnp.zeros((2, tm // GRID_W, ML_HEAD_DIM), F32)
        col_t = jnp.zeros((2, tm, ML_HEAD_DIM), F32)
    tok = lambda width, dt: jax.ShapeDtypeStruct((bsz, n, width), dt)
    tok_spec = lambda width: pl.BlockSpec((None, tm, width), lambda b, i: (b, i, 0))
    const = lambda shape: pl.BlockSpec(shape, lambda b, i: (0,) * len(shape))
    out_shape = [tok(NA_WIDTH, BF16)] * 3 + [tok(ML_WIDTH, BF16)] * 2 + [
        jax.ShapeDtypeStruct((bsz, ML_WIDTH, n), BF16), tok(ML_WIDTH, F32),
        tok(V7X_LANES, F32), jax.ShapeDtypeStruct((bsz, N_GATES, n), F32)]
    out_specs = [tok_spec(NA_WIDTH)] * 3 + [tok_spec(ML_WIDTH)] * 2 + [
        pl.BlockSpec((None, ML_WIDTH, tm), lambda b, i: (b, 0, i)), tok_spec(ML_WIDTH),
        tok_spec(V7X_LANES), pl.BlockSpec((None, N_GATES, tm), lambda b, i: (b, 0, i))]
    return pl.pallas_call(
        functools.partial(_evenproj_body, d=d, rope=rope),
        out_shape=out_shape,
        grid=(bsz, n // tm),
        in_specs=[
            tok_spec(d),
            pl.BlockSpec((None, 1, 3 * d), lambda b, i: (mod_row(b), 0, 0)),
            const((1, d)),
            const(w_na.shape), const(w_q.shape), const(w_k.shape), const(w_o.shape),
            const((ML_WIDTH, d)),
            const((d, V7X_LANES)),
            const((N_GATES, d)),
            const((1, V7X_LANES)),
            const((N_GATES, 1)),
            pl.BlockSpec((2, tm // GRID_W, ML_HEAD_DIM), lambda b, i: (0, i if rope else 0, 0)),
            pl.BlockSpec((2, tm, ML_HEAD_DIM), lambda b, i: (0, 0, 0)),
        ],
        out_specs=out_specs,
        compiler_params=_params(2),
        name="even_projection",
    )(x, mod3, g.reshape(1, d), w_na, w_q, w_k, w_o, w_vt, w_gc, w_gr, gb_col, gb_row, row_t, col_t)


def _na_body(q_ref, k_ref, v_ref, kx_ref, vx_ref, *rest, rows):
    rb = pl.program_id(2)
    sub, band = NA_SUB_ROWS, NA_BAND_ROWS
    n_sub = NA_ROWS_PER_STEP // sub
    tq = sub * GRID_W
    bias_refs, o_ref = rest[:n_sub], rest[n_sub]
    kx = kx_ref[...]
    vx = vx_ref[...]
    q = q_ref[...]
    lane = lax.broadcasted_iota(jnp.int32, q.shape, 1)
    bands = []
    for u in range(n_sub):
        band0 = jnp.clip((rb * n_sub + u) * sub - NA_KH // 2, 0, rows - band)
        start = pl.multiple_of(band0 * GRID_W, GRID_W)
        bands.append((k_ref[pl.ds(start, band * GRID_W), :], v_ref[pl.ds(start, band * GRID_W), :]))
    qms = [jnp.where((lane // NA_HEAD_DIM) == hh, q, jnp.zeros_like(q)) for hh in range(2)]
    items = [(hh, u) for hh in range(2) for u in range(n_sub)]

    def context_part(hh, j):
        s = _dot_nt(qms[hh][2 * j * tq:2 * (j + 1) * tq], kx)
        m_c = jnp.max(s, axis=-1, keepdims=True)
        p = jnp.exp(s - m_c)
        return m_c, jnp.sum(p, axis=-1, keepdims=True), _dot(p.astype(BF16), vx)

    def scores(item):
        hh, u = item
        return _dot_nt(qms[hh][u * tq:(u + 1) * tq], bands[u][0]) + bias_refs[u][hh]

    def attend(item, s_loc, ctx):
        u = item[1]
        rows_u = slice((u % 2) * tq, (u % 2 + 1) * tq)
        m_c, l_c, o_c = (part[rows_u] for part in ctx)
        m = jnp.maximum(jnp.max(s_loc, axis=-1, keepdims=True), m_c)
        p_loc = jnp.exp(s_loc - m)
        w_c = jnp.exp(m_c - m)
        denom = jnp.sum(p_loc, axis=-1, keepdims=True) + w_c * l_c
        o = _dot(p_loc.astype(BF16), bands[u][1]) + w_c * o_c
        return o * (1.0 / denom)

    outs = {}
    ctx = {}

    def need_context(item):
        key = (item[0], item[1] // 2)
        if key not in ctx:
            ctx[key] = context_part(*key)
        return ctx[key]

    need_context(items[0])
    ready = scores(items[0])
    for k, item in enumerate(items):
        nxt = scores(items[k + 1]) if k + 1 < len(items) else None
        if k + 1 < len(items):
            need_context(items[k + 1])
        outs[item] = attend(item, ready, need_context(item))
        ready = nxt
    per_head = [jnp.concatenate([outs[(hh, u)] for u in range(n_sub)], axis=0) for hh in range(2)]
    lane_o = lax.broadcasted_iota(jnp.int32, per_head[0].shape, 1)
    o_ref[...] = jnp.where(lane_o < NA_HEAD_DIM, per_head[0], per_head[1]).astype(BF16)


def _na_bias_tables(rpb, rows):
    assert NA_SUB_ROWS >= NA_KH // 2 and rows >= NA_BAND_ROWS + NA_SUB_ROWS
    sub, band, w = NA_SUB_ROWS, NA_BAND_ROWS, GRID_W
    kh, kw = NA_KH, NA_KW
    n_heads = rpb.shape[0]
    cols = np.arange(w)
    c0 = np.clip(cols - kw // 2, 0, w - kw)
    cc = np.arange(w)[None, :]
    col_ok = (cc >= c0[:, None]) & (cc < c0[:, None] + kw)
    period = 2 * w
    u = jnp.pad(rpb, ((0, 0), (0, 0), (w - kw, period - (w - kw) - rpb.shape[-1])))
    skew = jnp.tile(u, (1, 1, w))[..., :w * (period - 1)].reshape(n_heads, rpb.shape[1], w, period - 1)
    shifted = skew[..., w - 1:period - 1]
    t1 = jnp.where(col_ok[None, None], shifted, NEG_BIG)
    masked = jnp.full((n_heads, w, w), NEG_BIG, F32)
    blocks = []
    for i0, b0 in ((0, 0), (sub, sub - kh // 2), (rows - sub, rows - band)):
        for qi in range(i0, i0 + sub):
            r0 = min(max(qi - kh // 2, 0), rows - kh)
            row = [t1[:, r - qi + kh - 1] if r0 <= r < r0 + kh else masked for r in range(b0, b0 + band)]
            blocks.append(jnp.concatenate(row, axis=-1))
    return jnp.concatenate(blocks, axis=1)


def _neighbourhood_attention(q, k, v, kx, vx, bias):
    bsz, n, _ = q.shape
    lc = kx.shape[1]
    rows = n // GRID_W
    sub, band = NA_SUB_ROWS, NA_BAND_ROWS
    n_sub = NA_ROWS_PER_STEP // sub
    nrb = rows // NA_ROWS_PER_STEP
    tq = NA_ROWS_PER_STEP * GRID_W
    pair = 2 * NA_HEAD_DIM
    last = rows // sub - 1

    def bias_spec(u):
        def index(b, hp, rb):
            sb = rb * n_sub + u
            return (hp, jnp.where(sb == 0, 0, jnp.where(sb == last, 2, 1)), 0)
        return pl.BlockSpec((2, sub * GRID_W, band * GRID_W), index)

    return pl.pallas_call(
        functools.partial(_na_body, rows=rows),
        out_shape=jax.ShapeDtypeStruct((bsz, n, NA_WIDTH), BF16),
        grid=(bsz, NA_HEADS // 2, nrb),
        in_specs=[
            pl.BlockSpec((None, tq, pair), lambda b, hp, rb: (b, rb, hp)),
            pl.BlockSpec((None, n, pair), lambda b, hp, rb: (b, 0, hp)),
            pl.BlockSpec((None, n, pair), lambda b, hp, rb: (b, 0, hp)),
            pl.BlockSpec((None, lc, pair), lambda b, hp, rb: (b, 0, hp)),
            pl.BlockSpec((None, lc, pair), lambda b, hp, rb: (b, 0, hp)),
        ] + [bias_spec(u) for u in range(n_sub)],
        out_specs=pl.BlockSpec((None, tq, pair), lambda b, hp, rb: (b, rb, hp)),
        compiler_params=_params(3),
        name="neighbourhood_attention",
    )(q, k, v, kx, vx, *([bias] * n_sub))


def _mlstm_body(*refs, with_output):
    if with_output:
        (qf_ref, kf_ref, vtf_ref, gcf_ref, grf_ref, qr_ref, kr_ref, vtr_ref, gcr_ref, grr_ref,
         c0_ref, m0_ref, hf_ref, hr_ref, c_scr, m_scr) = refs
    else:
        (kf_ref, vtf_ref, grf_ref, kr_ref, vtr_ref, grr_ref,
         c0_ref, m0_ref, c_out_ref, m_out_ref, c_scr, m_scr) = refs
    step = pl.program_id(0)
    n_steps = pl.num_programs(0)
    L, hd = ML_CHUNK, ML_HEAD_DIM
    bsz = kf_ref.shape[0]

    @pl.when(step == 0)
    def _():
        c_scr[...] = c0_ref[...]
        m_scr[...] = m0_ref[...]

    s_idx = lax.broadcasted_iota(jnp.int32, (L, L), 0)
    t_idx = lax.broadcasted_iota(jnp.int32, (L, L), 1)
    ones_rows = (lax.broadcasted_iota(jnp.int32, (ML_AUG_ROWS, L), 0) == 0).astype(BF16)
    fwd_refs = (kf_ref, vtf_ref, grf_ref) + ((qf_ref, gcf_ref, hf_ref) if with_output else (None,) * 3)
    bwd_refs = (kr_ref, vtr_ref, grr_ref) + ((qr_ref, gcr_ref, hr_ref) if with_output else (None,) * 3)
    cps = kf_ref.shape[1] // L
    items = [(sub, bi, d, p) for sub in range(cps) for bi in range(bsz) for d in range(2)
             for p in range(ML_HEADS // 2)]
    lane2 = lax.broadcasted_iota(jnp.int32, (L, 2 * hd), 1)
    zero_ll = jnp.zeros((L, L), BF16)

    def refs_of(item):
        _, bi, d, _ = item
        return tuple(r if r is None else r.at[bi] for r in (fwd_refs if d == 0 else bwd_refs))

    def tokens_of(item):
        sub, _, d, _ = item
        first = sub if d == 0 else cps - 1 - sub
        return slice(first * L, (first + 1) * L)

    def block_diag(x2):
        zero = jnp.zeros_like(x2)
        return jnp.concatenate([jnp.where(lane2 < hd, x2, zero), jnp.where(lane2 >= hd, x2, zero)], axis=0)

    def front(item):
        _, bi, d, p = item
        k_ref, _, _, q_ref, _, _ = refs_of(item)
        tok = tokens_of(item)
        sl2 = slice(2 * p * hd, (2 * p + 2) * hd)
        qbd = block_diag(q_ref[tok, sl2])
        return _dot_nt(k_ref[tok, sl2], qbd), _dot_nt(c_scr[bi, d, p].astype(BF16), qbd)

    def finish(item, fronts):
        _, bi, d, p = item
        k_ref, vt_ref, gr_ref, _, gc_ref, h_ref = refs_of(item)
        tok = tokens_of(item)
        visible = (s_idx <= t_idx) if d == 0 else (s_idx >= t_idx)
        end = L - 1 if d == 0 else 0
        sl2 = slice(2 * p * hd, (2 * p + 2) * hd)
        a_prevs, a_toks, vt_augs, ps, w_inters, w_corrs, inv_floors = [], [], [], [], [], [], []
        for j in range(2):
            hh = 2 * p + j
            gi = hh * 4 + d * 2
            c_row = gr_ref[gi:gi + 1, tok]
            b_row = gr_ref[gi + 1:gi + 2, tok]
            b_end = b_row[:, end:end + 1]
            m_old = m_scr[bi, d, hh][0:1, 0:1]
            m_new = jnp.maximum(b_end + m_old, b_end + jnp.max(c_row, axis=-1, keepdims=True))
            a_prevs.append(jnp.broadcast_to(jnp.exp(b_end + m_old - m_new), (1, hd)))
            a_toks.append(jnp.exp(b_end + c_row - m_new))
            vt_augs.append(jnp.concatenate([vt_ref[hh * hd:(hh + 1) * hd, tok], ones_rows], axis=0))
            m_scr[bi, d, hh] = jnp.broadcast_to(m_new, (V7X_SUBLANES, V7X_LANES))
            if with_output:
                st = fronts[0][:, j * L:(j + 1) * L]
                log_w = jnp.where(visible, b_row + gc_ref[tok, gi:gi + 1], NEG_BIG)
                m_intra = jnp.max(log_w, axis=0, keepdims=True)
                ps.append((st * jnp.exp(log_w - m_intra)).astype(BF16))
                log_inter = b_row + m_old
                m_t = jnp.maximum(log_inter, m_intra)
                w_inters.append(jnp.exp(log_inter - m_t))
                w_corrs.append(jnp.exp(m_intra - m_t))
                inv_floors.append(jnp.exp(-m_t))
        cat = lambda parts: jnp.concatenate(parts, axis=1)
        vt2 = cat(vt_augs)
        c_old = c_scr[bi, d, p]
        if with_output:
            pbd = jnp.concatenate([cat([ps[0], zero_ll]), cat([zero_ll, ps[1]])], axis=0)
            intra = _dot(vt2, pbd)
            both = cat(w_inters) * fronts[1] + cat(w_corrs) * intra
            den = both[hd:hd + 1, :]
            ht = both[:hd, :] * (1.0 / jnp.maximum(jnp.abs(den), cat(inv_floors)))
            for j in range(2):
                h_ref[tok, (2 * p + j) * hd:(2 * p + j + 1) * hd] = ht[:, j * L:(j + 1) * L].T
        av = (vt2.astype(F32) * cat(a_toks)).astype(BF16)
        c_scr[bi, d, p] = cat(a_prevs) * c_old + _dot(av, block_diag(k_ref[tok, sl2]))

    ready = front(items[0]) if with_output else None
    for i, item in enumerate(items):
        nxt = front(items[i + 1]) if with_output and i + 1 < len(items) else None
        finish(item, ready)
        ready = nxt

    if not with_output:
        @pl.when(step == n_steps - 1)
        def _():
            c_out_ref[...] = c_scr[...]
            m_out_ref[...] = m_scr[...]


def _mlstm_scan(q, k, vt, gcol, grow, c0, m0, with_output):
    bsz, n, _ = k.shape
    L, hd = ML_CHUNK, ML_HEAD_DIM
    span = min(ML_CHUNKS_PER_STEP, n // L) * L
    nsteps = n // span
    fwd = lambda width: pl.BlockSpec((bsz, span, width), lambda c: (0, c, 0))
    rev = lambda width: pl.BlockSpec((bsz, span, width), lambda c: (0, nsteps - 1 - c, 0))
    fwd_t = lambda height: pl.BlockSpec((bsz, height, span), lambda c: (0, 0, c))
    rev_t = lambda height: pl.BlockSpec((bsz, height, span), lambda c: (0, 0, nsteps - 1 - c))
    c_spec = pl.BlockSpec((bsz, 2, ML_HEADS // 2, hd + ML_AUG_ROWS, 2 * hd), lambda c: (0, 0, 0, 0, 0))
    m_spec = pl.BlockSpec((bsz, 2, ML_HEADS, V7X_SUBLANES, V7X_LANES), lambda c: (0, 0, 0, 0, 0))
    if with_output:
        args = [q, k, vt, gcol, grow, q, k, vt, gcol, grow, c0, m0]
        in_specs = ([fwd(ML_WIDTH)] * 2 + [fwd_t(ML_WIDTH), fwd(V7X_LANES), fwd_t(N_GATES)]
                    + [rev(ML_WIDTH)] * 2 + [rev_t(ML_WIDTH), rev(V7X_LANES), rev_t(N_GATES)])
        out_shape = [jax.ShapeDtypeStruct((bsz, n, ML_WIDTH), F32)] * 2
        out_specs = [fwd(ML_WIDTH), rev(ML_WIDTH)]
    else:
        args = [k, vt, grow, k, vt, grow, c0, m0]
        in_specs = ([fwd(ML_WIDTH), fwd_t(ML_WIDTH), fwd_t(N_GATES)]
                    + [rev(ML_WIDTH), rev_t(ML_WIDTH), rev_t(N_GATES)])
        out_shape = [jax.ShapeDtypeStruct(c0.shape, F32), jax.ShapeDtypeStruct(m0.shape, F32)]
        out_specs = [c_spec, m_spec]
    return pl.pallas_call(
        functools.partial(_mlstm_body, with_output=with_output),
        out_shape=out_shape,
        grid=(nsteps,),
        in_specs=in_specs + [c_spec, m_spec],
        out_specs=out_specs,
        scratch_shapes=[pltpu.VMEM((bsz, 2, ML_HEADS // 2, hd + ML_AUG_ROWS, 2 * hd), F32),
                        pltpu.VMEM((bsz, 2, ML_HEADS, V7X_SUBLANES, V7X_LANES), F32)],
        compiler_params=_params(1),
        name="mlstm_scan_latent" if with_output else "mlstm_scan_context",
    )(*args)


def _gelu(x):
    k2 = 2.0 * (2.0 / np.pi) ** 0.5
    z = x * (k2 + (k2 * 0.044715) * (x * x))
    return x * jax.nn.sigmoid(z)


def _sg_body(x_ref, mod_ref, g_ref, win_ref, lng_ref, lnb_ref, ws_ref, bs_ref, wout_ref, o_ref, v_scr,
             *, d, width, tm):
    x = x_ref[...]
    mod = mod_ref[...]
    h = _modulated(x, g_ref[...], mod, d).astype(BF16)
    gw = width // SG_GROUPS
    groups = [slice(g * gw, (g + 1) * gw) for g in range(SG_GROUPS)]
    order = [width + cs.start for cs in groups] + [cs.start for cs in groups]
    project = lambda k: _dot(h, win_ref[:, order[k]:order[k] + gw])
    pre = project(0)
    total = None
    for k, cs in enumerate(groups):
        nxt = project(k + 1)
        vg = _gelu(pre)
        v_scr[:, cs] = vg
        part = jnp.sum(vg, axis=-1, keepdims=True)
        total = part if total is None else total + part
        pre = nxt
    mu = total * (1.0 / width)
    sq = None
    for cs in groups:
        vc = v_scr[:, cs] - mu
        part = jnp.sum(vc * vc, axis=-1, keepdims=True)
        sq = part if sq is None else sq + part
    rstd = lax.rsqrt(sq * (1.0 / width) + EPS)
    bs = bs_ref[...]
    y = None
    for g, cs in enumerate(groups):
        nxt = project(SG_GROUPS + g + 1) if g + 1 < SG_GROUPS else None
        vn = ((v_scr[:, cs] - mu) * rstd * lng_ref[:, cs] + lnb_ref[:, cs]).astype(BF16)
        u = _gelu(pre)
        pre = nxt
        parts = []
        for c in range(tm // SG_CHUNK):
            rs = slice(c * SG_CHUNK, (c + 1) * SG_CHUNK)
            mixed = _dot(ws_ref[g], vn[rs]) + bs[:, g:g + 1]
            parts.append((u[rs] * mixed).astype(BF16))
        yg = _dot(jnp.concatenate(parts, axis=0), wout_ref[cs, :])
        y = yg if y is None else y + yg
    o_ref[...] = x + mod[:, 2 * d:] * y


def _spatial_gating(x, mod3, g, w_in, ln_g, ln_b, w_s, b_s, w_out, tm=1024):
    bsz, n, d = x.shape
    width = w_out.shape[0]
    tm = min(tm, n)
    w_in, w_s, w_out = w_in.astype(BF16), w_s.astype(BF16), w_out.astype(BF16)
    const = lambda shape: pl.BlockSpec(shape, lambda b, i: (0,) * len(shape), pipeline_mode=pl.Buffered(1))
    return pl.pallas_call(
        functools.partial(_sg_body, d=d, width=width, tm=tm),
        out_shape=jax.ShapeDtypeStruct(x.shape, F32),
        grid=(bsz, n // tm),
        in_specs=[
            pl.BlockSpec((None, tm, d), lambda b, i: (b, i, 0)),
            pl.BlockSpec((None, 1, 3 * d), lambda b, i: (b, 0, 0)),
            const((1, d)), const(w_in.shape), const((1, width)), const((1, width)),
            const(w_s.shape), const((SG_CHUNK, SG_GROUPS)), const(w_out.shape),
        ],
        out_specs=pl.BlockSpec((None, tm, d), lambda b, i: (b, i, 0)),
        scratch_shapes=[pltpu.VMEM((tm, width), F32)],
        compiler_params=_params(2),
        name="spatial_gating",
    )(x, mod3, g.reshape(1, d), w_in, ln_g.reshape(1, width), ln_b.reshape(1, width), w_s, b_s.T, w_out)


def _rope_tables(n):
    hd = ML_HEAD_DIM
    n_pairs = hd // 4
    inv_freq = ROPE_THETA ** (-jnp.arange(n_pairs, dtype=F32) / n_pairs)
    row_ang = jnp.arange(n // GRID_W, dtype=F32)[:, None] * inv_freq
    col_ang = jnp.arange(GRID_W, dtype=F32)[:, None] * inv_freq

    def table(ang, is_row):
        zero = jnp.zeros_like(ang)
        cos, sin = jnp.cos(ang), jnp.sin(ang)
        half = lambda t: jnp.concatenate([t, zero] if is_row else [zero, t], axis=-1)
        return jnp.stack([jnp.concatenate([half(cos), half(cos)], axis=-1),
                          jnp.concatenate([half(-sin), half(sin)], axis=-1)])

    return table(row_ang, True), table(col_ang, False)


def _even_weights(w_in, gate_b):
    d = w_in.shape[0]
    hd = ML_HEAD_DIM
    base = 3 * NA_WIDTH
    w_in = w_in.astype(BF16)
    deint = lambda w: w.reshape(d, ML_HEADS, hd // 2, 2).transpose(0, 1, 3, 2).reshape(d, ML_WIDTH)
    qb = deint(w_in[:, base:base + ML_WIDTH])
    kb = deint(w_in[:, base + ML_WIDTH:base + 2 * ML_WIDTH])
    w_na = w_in[:, :base]
    w_o = w_in[:, base + 3 * ML_WIDTH:base + 4 * ML_WIDTH]
    w_vt = w_in[:, base + 2 * ML_WIDTH:base + 3 * ML_WIDTH].T
    wg = w_in[:, base + 4 * ML_WIDTH:]
    w_gc = jnp.pad(wg, ((0, 0), (0, V7X_LANES - N_GATES))).astype(BF16)
    w_gr = wg.T.astype(BF16)
    gb = gate_b.reshape(N_GATES).astype(F32)
    gb_col = jnp.pad(gb, (0, V7X_LANES - N_GATES)).reshape(1, V7X_LANES)
    gb_row = gb.reshape(N_GATES, 1)
    return w_na, qb, kb, w_o, w_vt, w_gc, w_gr, gb_col, gb_row


def kernel(x, c, ctx, c_ctx, w_mod, b_mod, norm_g, ffn_w_in, ffn_w_out, mix_w_in, na_rpb, ml_gate_b, ml_head_g,
           mix_w_out, sg_w_in, sg_ln_g, sg_ln_b, sg_w_s, sg_b_s, sg_w_out, final_g):
    bsz, n, d = x.shape
    depth = w_mod.shape[0]
    ctx_row = bsz
    cvec = jnp.zeros((V7X_SUBLANES, d), F32).at[:bsz].set(c).at[ctx_row].set(c_ctx)
    mod = _mod_vectors(cvec, w_mod, b_mod).reshape(depth, V7X_SUBLANES, 3, 1, 3 * d)
    batch_row = lambda b: b
    context_row = lambda b: ctx_row
    last_ctx_layer = ((depth - 1) // 2) * 2
    ffn_order = [(l, half) for l in range(depth) for half in range(2)]
    following = lambda l, half: dict(zip(ffn_order, ffn_order[1:])).get((l, half))
    cast_of = lambda nxt: None if nxt is None else (ffn_w_in, ffn_w_out) + nxt
    w_now = (ffn_w_in[0, 0].astype(BF16), ffn_w_out[0, 0].astype(BF16))
    lc = ctx.shape[1]
    xc = ctx.reshape(1, bsz * lc, d)
    per_batch_rows = lambda a: a.reshape(bsz, lc, a.shape[-1])
    per_batch_cols = lambda a: a.reshape(a.shape[1], bsz, lc).transpose(1, 0, 2)
    for l in range(depth):
        ctx_in = l <= last_ctx_layer
        ctx_out = l < last_ctx_layer
        if ctx_in:
            xc = _ffn(xc, mod[l, :, 0], context_row, norm_g[l, 0], *w_now)
        x, w_now = _ffn(x, mod[l, :, 0], batch_row, norm_g[l, 0], *w_now, cast_next=cast_of(following(l, 0)))
        if l % 2 == 0:
            e = l // 2
            weights = _even_weights(mix_w_in[e], ml_gate_b[e])
            qa, ka, va, qb, kb, vt, ob, gcol, grow = _even_projection(
                x, mod[l, :, 1], batch_row, norm_g[l, 1], weights, _rope_tables(n))
            _, kax, vax, _, kbx, vtx, _, _, growx = _even_projection(
                xc, mod[l, :, 1], context_row, norm_g[l, 1], weights, None)
            kax, vax, kbx = per_batch_rows(kax), per_batch_rows(vax), per_batch_rows(kbx)
            vtx, growx = per_batch_cols(vtx), per_batch_cols(growx)
            bias = _na_bias_tables(na_rpb[e], n // GRID_W)
            ya = _neighbourhood_attention(qa, ka, va, kax, vax, bias)
            c0 = jnp.zeros((bsz, 2, ML_HEADS // 2, ML_HEAD_DIM + ML_AUG_ROWS, 2 * ML_HEAD_DIM), F32)
            m0 = jnp.zeros((bsz, 2, ML_HEADS, V7X_SUBLANES, V7X_LANES), F32)
            c1, m1 = _mlstm_scan(None, kbx, vtx, None, growx, c0, m0, False)
            hf, hr = _mlstm_scan(qb, kb, vt, gcol, grow, c1, m1, True)
            mixer = (mod[l, :, 1], ya, hf, hr, ob, ml_head_g[e], mix_w_out[e].astype(BF16))
            assert not ctx_out, "context output path is not needed for this depth"
        else:
            o = l // 2
            mixer = None
            x = _spatial_gating(x, mod[l, :, 1], norm_g[l, 1], sg_w_in[o], sg_ln_g[o], sg_ln_b[o],
                                sg_w_s[o], sg_b_s[o], sg_w_out[o])
        fg = final_g if l == depth - 1 else None
        nxt = following(l, 1)
        out = _ffn(x, mod[l, :, 2], batch_row, norm_g[l, 2], *w_now, final_g=fg, mixer=mixer, cast_next=cast_of(nxt))
        x, w_now = out if nxt is not None else (out, None)
    return x
```
